```python
import numpy as np
import jax, jax.numpy as jnp
from jax import lax

D_MODEL = 2048
BATCH = 4
SEQ = 2048
DEPTH = 2

GRID_W = 64
CTX_LEN = 256
D_BRANCH = D_MODEL // 2
N_BRANCHES = 3
HEAD_DIM = 128
N_HEADS = D_BRANCH // HEAD_DIM
NA_KH_MAX = 8
NA_KW = 16
ROPE_THETA = 10000.0
ROPE_AXIS_DIM = HEAD_DIM // 2
CONV_WIDTH = 3
CHUNK = 128
GMLP_GROUP_DIM = 128
GMLP_GROUPS = D_BRANCH // GMLP_GROUP_DIM
N_EXPERTS = 16
CAPACITY_FACTOR = 2
EXPERT_FF = D_MODEL
NORM_EPS = 1e-6
NEG_INF = -1e30
PROJ_SIZES = (D_BRANCH,) * 8 + (N_BRANCHES * D_MODEL,)
PROJ_SPLITS = tuple(int(s) for s in np.cumsum(PROJ_SIZES[:-1]))
D_PROJ = sum(PROJ_SIZES)

kernel_name = 'hybrid_na_conv_gmlp_ec_moe_prefix_dit'


def rms_norm(x, g):
    xf = x.astype(jnp.float32)
    xf = xf * lax.rsqrt(jnp.mean(xf * xf, axis=-1, keepdims=True) + NORM_EPS)
    return (xf * g.astype(jnp.float32)).astype(x.dtype)


def layer_norm(x, g):
    xf = x.astype(jnp.float32)
    mu = jnp.mean(xf, axis=-1, keepdims=True)
    var = jnp.mean(jnp.square(xf - mu), axis=-1, keepdims=True)
    return ((xf - mu) * lax.rsqrt(var + NORM_EPS) * g.astype(jnp.float32)).astype(x.dtype)


def modulate(x, g, shift, scale):
    return rms_norm(x, g) * (1 + scale) + shift


def split_heads(t):
    b, n, _ = t.shape
    return t.reshape(b, n, N_HEADS, HEAD_DIM).transpose(0, 2, 1, 3)


def merge_heads(t):
    b, h, n, hd = t.shape
    return t.transpose(0, 2, 1, 3).reshape(b, n, h * hd)


def axial_rope(t):
    b, h, n, hd = t.shape
    pos = jnp.arange(n)
    rc = jnp.stack([pos // GRID_W, pos % GRID_W], axis=-1).astype(jnp.float32)
    inv = ROPE_THETA ** (-jnp.arange(0, ROPE_AXIS_DIM, 2, dtype=jnp.float32) / ROPE_AXIS_DIM)
    ang = rc[:, :, None] * inv
    cos, sin = jnp.cos(ang), jnp.sin(ang)
    tf = t.astype(jnp.float32).reshape(b, h, n, 2, 2, ROPE_AXIS_DIM // 2)
    t1, t2 = tf[..., 0, :], tf[..., 1, :]
    out = jnp.stack([t1 * cos - t2 * sin, t1 * sin + t2 * cos], axis=-2)
    return out.reshape(b, h, n, hd).astype(t.dtype)


def dense_attention(q, k, v):
    s = jnp.einsum('bhqd,bhkd->bhqk', q, k).astype(jnp.float32) * (q.shape[-1] ** -0.5)
    p = jax.nn.softmax(s, axis=-1).astype(v.dtype)
    return jnp.einsum('bhqk,bhkd->bhqd', p, v)


def neighbourhood_attention(q, k, v, k_ctx, v_ctx, rpb):
    b, h, n, hd = q.shape
    rows = n // GRID_W
    kh = min(NA_KH_MAX, rows)
    n_cb = GRID_W // NA_KW
    kb = 2 * NA_KW
    scale = hd ** -0.5
    q_col = np.arange(GRID_W).reshape(n_cb, NA_KW)
    blk_start = np.clip(np.arange(n_cb) * NA_KW - NA_KW // 2, 0, GRID_W - kb)
    k_col = blk_start[:, None] + np.arange(kb)
    win_start = np.clip(q_col - NA_KW // 2, 0, GRID_W - NA_KW)
    col_ok = (k_col[:, None, :] >= win_start[..., None]) & (k_col[:, None, :] < win_start[..., None] + NA_KW)
    col_ok = np.broadcast_to(col_ok[:, :, None, :], (n_cb, NA_KW, kh, kb)).reshape(n_cb, NA_KW, kh * kb)
    dc_idx = np.clip(k_col[:, None, :] - q_col[..., None] + NA_KW - 1, 0, 2 * NA_KW - 2)
    qg = q.reshape(b, h, rows, n_cb, NA_KW, hd)
    kg = k.reshape(b, h, rows, GRID_W, hd)
    vg = v.reshape(b, h, rows, GRID_W, hd)

    def row_block(r):
        rs = jnp.clip(r - kh // 2, 0, rows - kh)

        def gather(t):
            band = lax.dynamic_slice_in_dim(t, rs, kh, axis=2)
            blk = band[:, :, :, k_col]
            return blk.transpose(0, 1, 3, 2, 4, 5).reshape(b, h, n_cb, kh * kb, hd)

        kblk, vblk = gather(kg), gather(vg)
        qr = lax.dynamic_index_in_dim(qg, r, axis=2, keepdims=False)
        dr_idx = rs + jnp.arange(kh) - r + NA_KH_MAX - 1
        bias = rpb[:, dr_idx[None, None, :, None], dc_idx[:, :, None, :]]
        bias = jnp.where(col_ok, bias.reshape(h, n_cb, NA_KW, kh * kb).astype(jnp.float32), NEG_INF)
        s_loc = jnp.einsum('bhcqd,bhckd->bhcqk', qr, kblk).astype(jnp.float32) * scale + bias
        s_ctx = jnp.einsum('bhcqd,bhld->bhcql', qr, k_ctx).astype(jnp.float32) * scale
        p = jax.nn.softmax(jnp.concatenate([s_loc, s_ctx], axis=-1), axis=-1).astype(v.dtype)
        return (jnp.einsum('bhcqk,bhckd->bhcqd', p[..., :kh * kb], vblk)
                + jnp.einsum('bhcql,bhld->bhcqd', p[..., kh * kb:], v_ctx))

    out = lax.map(row_block, jnp.arange(rows))
    return out.transpose(1, 2, 0, 3, 4, 5).reshape(b, h, n, hd)


def short_conv(xc, bg, cg, w):
    z = cg * xc
    zp = jnp.pad(z, ((0, 0), (1, 1), (0, 0)))
    y = zp[:, :-2] * w[0] + zp[:, 1:-1] * w[1] + zp[:, 2:] * w[2]
    return bg * y


def chunk_gmlp(u, v, ln_g, w_s, b_s):
    b, n, _ = u.shape
    u = jax.nn.gelu(u)
    v = layer_norm(jax.nn.gelu(v), ln_g)
    vc = v.reshape(b, n // CHUNK, CHUNK, GMLP_GROUPS, GMLP_GROUP_DIM)
    s = jnp.einsum('gpq,bnqgc->bnpgc', w_s, vc) + b_s.T[None, None, :, :, None]
    return u * s.reshape(b, n, D_BRANCH)


def merge_branches(attn_o, parts, conv_w, ln_g, w_s, b_s, w_branch, w_out):
    _, _, _, xc, bg, cg, u, v, gate_logits = parts
    conv_o = short_conv(xc, bg, cg, conv_w)
    gmlp_o = chunk_gmlp(u, v, ln_g, w_s, b_s)
    b, n, _ = gate_logits.shape
    gates = jax.nn.sigmoid(gate_logits.astype(jnp.float32)).astype(attn_o.dtype).reshape(b, n, N_BRANCHES, D_MODEL)
    br = jnp.stack([attn_o, conv_o, gmlp_o], axis=2)
    proj = jnp.einsum('bnic,icd->bnid', br, w_branch)
    return jnp.sum(gates * proj, axis=2) @ w_out


def expert_choice_moe(h, w_router, b_router, w_gate, w_up, w_down):
    b, n, _ = h.shape
    cap = CAPACITY_FACTOR * n // N_EXPERTS
    logits = (h @ w_router).astype(jnp.float32) + b_router.astype(jnp.float32)
    aff = jax.nn.softmax(logits, axis=-1)
    gate, idx = lax.top_k(jnp.swapaxes(aff, 1, 2), cap)
    bidx = jnp.arange(b)[:, None, None]
    xs = h[bidx, idx]
    hg = jnp.einsum('becd,edf->becf', xs, w_gate)
    hu = jnp.einsum('becd,edf->becf', xs, w_up)
    y = jnp.einsum('becf,efd->becd', jax.nn.silu(hg) * hu, w_down)
    y = y * gate[..., None].astype(y.dtype)
    return jnp.zeros_like(h).at[bidx, idx].add(y)


def project(h, w_in):
    q, k, v, xc, bg, cg, u, vs, g = jnp.split(h @ w_in, PROJ_SPLITS, axis=-1)
    return (split_heads(q), split_heads(k), split_heads(v), xc, bg, cg, u, vs, g)


def setup_inputs(seed: int = 0) -> dict:
    key = jax.random.key(seed)
    ks = jax.random.split(key, 24)

    def nrm(k, shape, scale):
        return jax.random.normal(k, shape, dtype=jnp.float32) * scale

    L, D, E, F = DEPTH, D_MODEL, N_EXPERTS, EXPERT_FF
    return {
        'x': nrm(ks[0], (BATCH, SEQ, D), 1.0),
        'c': nrm(ks[1], (BATCH, D), 1.0),
        'ctx': nrm(ks[2], (BATCH, CTX_LEN, D), 1.0),
        'c_ctx': nrm(ks[3], (D,), 1.0),
        'w_mod': nrm(ks[4], (L, D, 6 * D), D ** -0.5),
        'b_mod': nrm(ks[5], (L, 6 * D), 0.02),
        'norm1_g': 1.0 + nrm(ks[6], (L, D), 0.02),
        'w_in': nrm(ks[7], (L, D, D_PROJ), D ** -0.5),
        'na_rpb': nrm(ks[8], (L, N_HEADS, 2 * NA_KH_MAX - 1, 2 * NA_KW - 1), 0.1),
        'conv_w': nrm(ks[9], (L, CONV_WIDTH, D_BRANCH), CONV_WIDTH ** -0.5),
        'gmlp_ln_g': 1.0 + nrm(ks[10], (L, D_BRANCH), 0.02),
        'w_spatial': nrm(ks[11], (L, GMLP_GROUPS, CHUNK, CHUNK), CHUNK ** -0.5),
        'b_spatial': 1.0 + nrm(ks[12], (L, GMLP_GROUPS, CHUNK), 0.02),
        'w_branch': nrm(ks[13], (L, N_BRANCHES, D_BRANCH, D), D_BRANCH ** -0.5),
        'w_out': nrm(ks[14], (L, D, D), D ** -0.5),
        'norm2_g': 1.0 + nrm(ks[15], (L, D), 0.02),
        'w_router': nrm(ks[16], (L, D, E), D ** -0.5),
        'b_router': nrm(ks[17], (L, E), 0.01),
        'w_e_gate': nrm(ks[18], (L, E, D, F), D ** -0.5),
        'w_e_up': nrm(ks[19], (L, E, D, F), D ** -0.5),
        'w_e_down': nrm(ks[20], (L, E, F, D), F ** -0.5),
        'final_g': 1.0 + nrm(ks[21], (D,), 0.02),
    }


def reference(x, c, ctx, c_ctx, w_mod, b_mod, norm1_g, w_in, na_rpb, conv_w, gmlp_ln_g,
              w_spatial, b_spatial, w_branch, w_out, norm2_g, w_router, b_router,
              w_e_gate, w_e_up, w_e_down, final_g):
    b = x.shape[0]
    ctx_s = ctx
    for layer in range(DEPTH):
        last = layer == DEPTH - 1
        mod = (jax.nn.silu(c) @ w_mod[layer] + b_mod[layer]).reshape(b, 6, D_MODEL)[:, :, None, :]
        mod_c = (jax.nn.silu(c_ctx) @ w_mod[layer] + b_mod[layer]).reshape(6, D_MODEL)
        mix_args = (conv_w[layer], gmlp_ln_g[layer], w_spatial[layer], b_spatial[layer],
                    w_branch[layer], w_out[layer])
        moe_args = (w_router[layer], b_router[layer], w_e_gate[layer], w_e_up[layer], w_e_down[layer])

        hc = modulate(ctx_s, norm1_g[layer], mod_c[0], mod_c[1])
        if last:
            kv_c = hc @ w_in[layer][:, D_BRANCH:3 * D_BRANCH]
            k_c, v_c = split_heads(kv_c[..., :D_BRANCH]), split_heads(kv_c[..., D_BRANCH:])
        else:
            parts_c = project(hc, w_in[layer])
            q_c, k_c, v_c = parts_c[0], parts_c[1], parts_c[2]
            attn_c = merge_heads(dense_attention(q_c, k_c, v_c))
            ctx_new = ctx_s + mod_c[2] * merge_branches(attn_c, parts_c, *mix_args)
            hc2 = modulate(ctx_new, norm2_g[layer], mod_c[3], mod_c[4])
            ctx_new = ctx_new + mod_c[5] * expert_choice_moe(hc2, *moe_args)

        h = modulate(x, norm1_g[layer], mod[:, 0], mod[:, 1])
        parts = project(h, w_in[layer])
        q, k, v = axial_rope(parts[0]), axial_rope(parts[1]), parts[2]
        attn = merge_heads(neighbourhood_attention(q, k, v, k_c, v_c, na_rpb[layer]))
        x = x + mod[:, 2] * merge_branches(attn, parts, *mix_args)
        h2 = modulate(x, norm2_g[layer], mod[:, 3], mod[:, 4])
        x = x + mod[:, 5] * expert_choice_moe(h2, *moe_args)

        if not last:
            ctx_s = ctx_new
    return rms_norm(x, final_g)
```

```python
import functools

import numpy as np
import jax
import jax.numpy as jnp
from jax import lax
from jax.experimental import pallas as pl
from jax.experimental.pallas import tpu as pltpu

D_MODEL = 2048
DEPTH = 2
GRID_W = 64
D_BRANCH = D_MODEL // 2
N_BRANCHES = 3
HEAD_DIM = 128
N_HEADS = D_BRANCH // HEAD_DIM
NA_KH = 8
NA_KW = 16
ROPE_THETA = 10000.0
ROPE_AXIS_DIM = HEAD_DIM // 2
CHUNK = 128
GMLP_GROUPS = D_BRANCH // 128
N_EXPERTS = 16
CAPACITY_FACTOR = 2
NORM_EPS = 1e-6
NEG_INF = -1e30
D_PROJ = 8 * D_BRANCH + N_BRANCHES * D_MODEL

COL_Q, COL_K, COL_V, COL_XC, COL_BG, COL_CG, COL_U, COL_VS = (i * D_BRANCH for i in range(8))
COL_GATE = 8 * D_BRANCH

NA_Q_ROWS = 4
NA_BAND_ROWS = 12
NA_TQ = NA_Q_ROWS * GRID_W
NA_TK = NA_BAND_ROWS * GRID_W

V7X_VMEM_LIMIT = 56 * 1024 * 1024

BF16 = jnp.bfloat16
F32 = jnp.float32


def _params(n_axes, vmem=V7X_VMEM_LIMIT):
    return pltpu.CompilerParams(dimension_semantics=("arbitrary",) * n_axes, vmem_limit_bytes=vmem)


def _modvec_kernel(c_ref, w_ref, b_ref, o_ref):
    cv = c_ref[...]
    s = jax.nn.silu(cv).astype(BF16)
    o_ref[0] = jnp.dot(s, w_ref[0].astype(BF16), preferred_element_type=F32) + b_ref[0]


def modvec(cvec, w_mod, b_mod, tn=1024):
    nl, d, n6 = w_mod.shape
    return pl.pallas_call(
        _modvec_kernel,
        grid=(nl, n6 // tn),
        in_specs=[
            pl.BlockSpec((8, d), lambda l, j: (0, 0)),
            pl.BlockSpec((1, d, tn), lambda l, j: (l, 0, j)),
            pl.BlockSpec((1, 1, tn), lambda l, j: (l, 0, j)),
        ],
        out_specs=pl.BlockSpec((1, 8, tn), lambda l, j: (l, 0, j)),
        out_shape=jax.ShapeDtypeStruct((nl, 8, n6), F32),
        compiler_params=_params(2),
        name="modvec",
    )(cvec, w_mod, b_mod.reshape(nl, 1, n6))


def _rms(x, g):
    ms = jnp.mean(x * x, axis=-1, keepdims=True)
    return x * lax.rsqrt(ms + NORM_EPS) * g


def _norm_mod_kernel(x_ref, g_ref, sh_ref, sc_ref, o_ref):
    y = _rms(x_ref[0], g_ref[...])
    o_ref[0] = (y * (1 + sc_ref[0]) + sh_ref[0]).astype(o_ref.dtype)


def _norm_kernel(x_ref, g_ref, o_ref):
    o_ref[0] = _rms(x_ref[0], g_ref[...]).astype(o_ref.dtype)


def norm_mod(x, g, shift, scale, out_dtype=BF16, tm=256):
    b, n, d = x.shape
    tm = min(tm, n)
    return pl.pallas_call(
        _norm_mod_kernel,
        grid=(b, n // tm),
        in_specs=[
            pl.BlockSpec((1, tm, d), lambda i, j: (i, j, 0)),
            pl.BlockSpec((1, d), lambda i, j: (0, 0)),
            pl.BlockSpec((1, 1, d), lambda i, j: (i, 0, 0)),
            pl.BlockSpec((1, 1, d), lambda i, j: (i, 0, 0)),
        ],
        out_specs=pl.BlockSpec((1, tm, d), lambda i, j: (i, j, 0)),
        out_shape=jax.ShapeDtypeStruct((b, n, d), out_dtype),
        compiler_params=_params(2),
        name="norm_mod",
    )(x, g.reshape(1, d), shift.reshape(b, 1, d), scale.reshape(b, 1, d))


def final_norm(x, g, tm=256):
    b, n, d = x.shape
    return pl.pallas_call(
        _norm_kernel,
        grid=(b, n // tm),
        in_specs=[
            pl.BlockSpec((1, tm, d), lambda i, j: (i, j, 0)),
            pl.BlockSpec((1, d), lambda i, j: (0, 0)),
        ],
        out_specs=pl.BlockSpec((1, tm, d), lambda i, j: (i, j, 0)),
        out_shape=jax.ShapeDtypeStruct((b, n, d), F32),
        compiler_params=_params(2),
        name="final_norm",
    )(x, g.reshape(1, d))


def _mm_kernel(x_ref, w_ref, o_ref, wbf_ref):
    @pl.when(pl.program_id(1) == 0)
    def _():
        wbf_ref[...] = w_ref[...].astype(BF16)

    o_ref[...] = jnp.dot(x_ref[...], wbf_ref[...], preferred_element_type=F32).astype(o_ref.dtype)


def matmul(x, w, col_off=0, n_cols=None, out_dtype=F32, tm=1024, tn=512):
    m, k = x.shape
    n_cols = w.shape[1] - col_off if n_cols is None else n_cols
    tm = min(tm, m)
    off = col_off // tn
    return pl.pallas_call(
        _mm_kernel,
        grid=(n_cols // tn, m // tm),
        in_specs=[
            pl.BlockSpec((tm, k), lambda j, i: (i, 0)),
            pl.BlockSpec((k, tn), lambda j, i: (0, off + j)),
        ],
        out_specs=pl.BlockSpec((tm, tn), lambda j, i: (i, j)),
        out_shape=jax.ShapeDtypeStruct((m, n_cols), out_dtype),
        scratch_shapes=[pltpu.VMEM((k, tn), BF16)],
        compiler_params=_params(2),
        name="matmul",
    )(x, w)


def _qkv_kernel(p_ref, cos_ref, sin_ref, o_ref, *, rope):
    c = pl.program_id(1)
    tm = p_ref.shape[0]

    def plain():
        o_ref[...] = p_ref[...].astype(BF16)

    if not rope:
        plain()
        return

    @pl.when(c < 2)
    def _():
        cs = cos_ref[...]
        sn = sin_ref[...]
        lane = lax.broadcasted_iota(jnp.int32, (tm, HEAD_DIM), 1)
        first_half = (lane % ROPE_AXIS_DIM) < (ROPE_AXIS_DIM // 2)
        for h in range(N_HEADS):
            t = p_ref[:, h * HEAD_DIM:(h + 1) * HEAD_DIM]
            partner = jnp.where(first_half,
                                pltpu.roll(t, HEAD_DIM - ROPE_AXIS_DIM // 2, 1),
                                pltpu.roll(t, ROPE_AXIS_DIM // 2, 1))
            o_ref[:, h * HEAD_DIM:(h + 1) * HEAD_DIM] = (t * cs + partner * sn).astype(BF16)

    @pl.when(c == 2)
    def _():
        plain()


def rope_tables(n):
    pos = jnp.arange(n)
    rc = jnp.stack([pos // GRID_W, pos % GRID_W], axis=-1).astype(F32)
    inv = ROPE_THETA ** (-jnp.arange(0, ROPE_AXIS_DIM, 2, dtype=F32) / ROPE_AXIS_DIM)
    ang = rc[:, :, None] * inv
    cos, sin = jnp.cos(ang), jnp.sin(ang)
    cos_t = jnp.concatenate([cos, cos], axis=-1).reshape(n, HEAD_DIM)
    sin_t = jnp.concatenate([-sin, sin], axis=-1).reshape(n, HEAD_DIM)
    return cos_t, sin_t


def qkv_cast(p, n, cos_t, sin_t, rope, tm=512):
    m = p.shape[0]
    tm = min(tm, n)
    nt = n // tm
    return pl.pallas_call(
        functools.partial(_qkv_kernel, rope=rope),
        grid=(m // tm, 3),
        in_specs=[
            pl.BlockSpec((tm, D_BRANCH), lambda i, c: (i, c)),
            pl.BlockSpec((tm, HEAD_DIM), lambda i, c: (i % nt, 0)),
            pl.BlockSpec((tm, HEAD_DIM), lambda i, c: (i % nt, 0)),
        ],
        out_specs=pl.BlockSpec((tm, D_BRANCH), lambda i, c: (i, c)),
        out_shape=jax.ShapeDtypeStruct((m, 3 * D_BRANCH), BF16),
        compiler_params=_params(2),
        name="qkv_cast",
    )(p, cos_t, sin_t)


def _nt_dot(a, b):
    return lax.dot_general(a, b, (((1,), (1,)), ((), ())), preferred_element_type=F32)


def _na_kernel(q_ref, k_ref, v_ref, kc_ref, vc_ref, bias_ref, o_ref):
    g = pl.program_id(2)
    n_groups = k_ref.shape[1] // NA_TQ
    band_row = jnp.clip(g * NA_Q_ROWS - NA_KH // 2, 0, n_groups * NA_Q_ROWS - NA_BAND_ROWS)
    start = pl.multiple_of(band_row * GRID_W, NA_TQ)
    pat = jnp.where(g == 0, 0, jnp.where(g == n_groups - 1, 2, 1))
    scale = HEAD_DIM ** -0.5
    q = q_ref[0]
    kb = k_ref[0, pl.ds(start, NA_TK), :]
    vb = v_ref[0, pl.ds(start, NA_TK), :]
    s_loc = _nt_dot(q, kb) * scale + bias_ref[pat, 0]
    s_ctx = _nt_dot(q, kc_ref[0]) * scale
    m = jnp.maximum(jnp.max(s_loc, axis=1, keepdims=True), jnp.max(s_ctx, axis=1, keepdims=True))
    e_loc = jnp.exp(s_loc - m)
    e_ctx = jnp.exp(s_ctx - m)
    inv = 1.0 / (jnp.sum(e_loc, axis=1, keepdims=True) + jnp.sum(e_ctx, axis=1, keepdims=True))
    o = jnp.dot((e_loc * inv).astype(BF16), vb, preferred_element_type=F32)
    o = o + jnp.dot((e_ctx * inv).astype(BF16), vc_ref[0], preferred_element_type=F32)
    o_ref[0] = o.astype(o_ref.dtype)


def na_bias(rpb, rows):
    n_groups = rows // NA_Q_ROWS
    pats = []
    for g in (0, 1, n_groups - 1):
        band_row = int(np.clip(g * NA_Q_ROWS - NA_KH // 2, 0, rows - NA_BAND_ROWS))
        qi = np.arange(NA_TQ)
        qr, qc = g * NA_Q_ROWS + qi // GRID_W, qi % GRID_W
        kj = np.arange(NA_TK)
        kr, kc = band_row + kj // GRID_W, kj % GRID_W
        rs = np.clip(qr - NA_KH // 2, 0, rows - NA_KH)
        ws = np.clip(qc - NA_KW // 2, 0, GRID_W - NA_KW)
        ok = ((kr[None, :] >= rs[:, None]) & (kr[None, :] < rs[:, None] + NA_KH)
              & (kc[None, :] >= ws[:, None]) & (kc[None, :] < ws[:, None] + NA_KW))
        dr = np.clip(kr[None, :] - qr[:, None] + NA_KH - 1, 0, 2 * NA_KH - 2)
        dc = np.clip(kc[None, :] - qc[:, None] + NA_KW - 1, 0, 2 * NA_KW - 2)
        pats.append(jnp.where(ok[None], rpb[:, dr, dc].astype(F32), NEG_INF))
    return jnp.stack(pats, axis=0)


def na_attention(qkv, kvc, kc_blk, bias, b, n):
    lc = kvc.shape[0] // b
    qkv3 = qkv.reshape(b, n, 3 * D_BRANCH)
    kvc3 = kvc.reshape(b, lc, kvc.shape[1])
    n_groups = n // NA_TQ
    out = pl.pallas_call(
        _na_kernel,
        grid=(N_HEADS, b, n_groups),
        in_specs=[
            pl.BlockSpec((1, NA_TQ, HEAD_DIM), lambda h, i, g: (i, g, h)),
            pl.BlockSpec((1, n, HEAD_DIM), lambda h, i, g: (i, 0, N_HEADS + h)),
            pl.BlockSpec((1, n, HEAD_DIM), lambda h, i, g: (i, 0, 2 * N_HEADS + h)),
            pl.BlockSpec((1, lc, HEAD_DIM), lambda h, i, g: (i, 0, kc_blk + h)),
            pl.BlockSpec((1, lc, HEAD_DIM), lambda h, i, g: (i, 0, kc_blk + N_HEADS + h)),
            pl.BlockSpec((3, 1, NA_TQ, NA_TK), lambda h, i, g: (0, h, 0, 0)),
        ],
        out_specs=pl.BlockSpec((1, NA_TQ, HEAD_DIM), lambda h, i, g: (i, g, h)),
        out_shape=jax.ShapeDtypeStruct((b, n, D_BRANCH), BF16),
        compiler_params=_params(3),
        name="na_attention",
    )(qkv3, qkv3, qkv3, kvc3, kvc3, bias)
    return out.reshape(b * n, D_BRANCH)


def _ctx_attn_kernel(q_ref, k_ref, v_ref, o_ref):
    s = _nt_dot(q_ref[0], k_ref[0]) * (HEAD_DIM ** -0.5)
    m = jnp.max(s, axis=1, keepdims=True)
    e = jnp.exp(s - m)
    p = e * (1.0 / jnp.sum(e, axis=1, keepdims=True))
    o_ref[0] = jnp.dot(p.astype(BF16), v_ref[0], preferred_element_type=F32).astype(o_ref.dtype)


def ctx_attention(qkv, b, lc):
    qkv3 = qkv.reshape(b, lc, 3 * D_BRANCH)
    out = pl.pallas_call(
        _ctx_attn_kernel,
        grid=(b, N_HEADS),
        in_specs=[
            pl.BlockSpec((1, lc, HEAD_DIM), lambda i, h: (i, 0, h)),
            pl.BlockSpec((1, lc, HEAD_DIM), lambda i, h: (i, 0, N_HEADS + h)),
            pl.BlockSpec((1, lc, HEAD_DIM), lambda i, h: (i, 0, 2 * N_HEADS + h)),
        ],
        out_specs=pl.BlockSpec((1, lc, HEAD_DIM), lambda i, h: (i, 0, h)),
        out_shape=jax.ShapeDtypeStruct((b, lc, D_BRANCH), BF16),
        compiler_params=_params(2),
        name="ctx_attention",
    )(qkv3, qkv3, qkv3)
    return out.reshape(b * lc, D_BRANCH)


def _conv_kernel(xc_ref, bg_ref, cg_ref, w_ref, o_ref):
    z = cg_ref[0] * xc_ref[0]
    n = z.shape[0]
    row = lax.broadcasted_iota(jnp.int32, z.shape, 0)
    z_prev = jnp.where(row == 0, 0.0, pltpu.roll(z, 1, 0))
    z_next = jnp.where(row == n - 1, 0.0, pltpu.roll(z, n - 1, 0))
    y = z_prev * w_ref[0:1, :] + z * w_ref[1:2, :] + z_next * w_ref[2:3, :]
    o_ref[0] = (bg_ref[0] * y).astype(o_ref.dtype)


def short_conv(p, conv_w, b, n, tc=256):
    p3 = p.reshape(b, n, p.shape[1])
    blk = lambda col: pl.BlockSpec((1, n, tc), lambda i, j: (i, 0, col // tc + j))
    out = pl.pallas_call(
        _conv_kernel,
        grid=(b, D_BRANCH // tc),
        in_specs=[blk(COL_XC), blk(COL_BG), blk(COL_CG),
                  pl.BlockSpec((3, tc), lambda i, j: (0, j))],
        out_specs=pl.BlockSpec((1, n, tc), lambda i, j: (i, 0, j)),
        out_shape=jax.ShapeDtypeStruct((b, n, D_BRANCH), BF16),
        compiler_params=_params(2),
        name="short_conv",
    )(p3, p3, p3, conv_w)
    return out.reshape(b * n, D_BRANCH)


def _gmlp_kernel(u_ref, v_ref, g_ref, ws_ref, bs_ref, o_ref):
    v = jax.nn.gelu(v_ref[...])
    mu = jnp.mean(v, axis=-1, keepdims=True)
    var = jnp.mean(jnp.square(v - mu), axis=-1, keepdims=True)
    vn = ((v - mu) * lax.rsqrt(var + NORM_EPS) * g_ref[...]).astype(BF16)
    for grp in range(GMLP_GROUPS):
        cols = slice(grp * 128, (grp + 1) * 128)
        s = jnp.dot(ws_ref[grp].astype(BF16), vn[:, cols], preferred_element_type=F32)
        s = s + bs_ref[:, grp:grp + 1]
        o_ref[:, cols] = (jax.nn.gelu(u_ref[:, cols]) * s).astype(o_ref.dtype)


def chunk_gmlp(p, ln_g, w_s, b_s):
    m = p.shape[0]
    return pl.pallas_call(
        _gmlp_kernel,
        grid=(m // CHUNK,),
        in_specs=[
            pl.BlockSpec((CHUNK, D_BRANCH), lambda i: (i, COL_U // D_BRANCH)),
            pl.BlockSpec((CHUNK, D_BRANCH), lambda i: (i, COL_VS // D_BRANCH)),
            pl.BlockSpec((1, D_BRANCH), lambda i: (0, 0)),
            pl.BlockSpec((GMLP_GROUPS, CHUNK, CHUNK), lambda i: (0, 0, 0)),
            pl.BlockSpec((CHUNK, GMLP_GROUPS), lambda i: (0, 0)),
        ],
        out_specs=pl.BlockSpec((CHUNK, D_BRANCH), lambda i: (i, 0)),
        out_shape=jax.ShapeDtypeStruct((m, D_BRANCH), BF16),
        compiler_params=_params(1),
        name="chunk_gmlp",
    )(p, p, ln_g.reshape(1, D_BRANCH), w_s, b_s.T)


def _merge_kernel(a_ref, c_ref, m_ref, g0_ref, g1_ref, g2_ref, w_ref, o_ref, wbf_ref):
    @pl.when(pl.program_id(1) == 0)
    def _():
        wbf_ref[...] = w_ref[...].astype(BF16)

    acc = None
    for i, (br, gl) in enumerate(((a_ref, g0_ref), (c_ref, g1_ref), (m_ref, g2_ref))):
        proj = jnp.dot(br[...], wbf_ref[i], preferred_element_type=F32)
        term = jax.nn.sigmoid(gl[...]) * proj
        acc = term if acc is None else acc + term
    o_ref[...] = acc.astype(o_ref.dtype)


def merge_branches(attn, conv, gm, p, w_branch, tm=512, tn=512):
    m = attn.shape[0]
    br = pl.BlockSpec((tm, D_BRANCH), lambda j, i: (i, 0))
    gate = lambda k: pl.BlockSpec((tm, tn), lambda j, i: (i, (COL_GATE + k * D_MODEL) // tn + j))
    return pl.pallas_call(
        _merge_kernel,
        grid=(D_MODEL // tn, m // tm),
        in_specs=[br, br, br, gate(0), gate(1), gate(2),
                  pl.BlockSpec((N_BRANCHES, D_BRANCH, tn), lambda j, i: (0, 0, j))],
        out_specs=pl.BlockSpec((tm, tn), lambda j, i: (i, j)),
        out_shape=jax.ShapeDtypeStruct((m, D_MODEL), BF16),
        scratch_shapes=[pltpu.VMEM((N_BRANCHES, D_BRANCH, tn), BF16)],
        compiler_params=_params(2),
        name="merge_branches",
    )(attn, conv, gm, p, p, p, w_branch)


def _out_proj_kernel(h_ref, w_ref, x_ref, mod_ref, o_ref, wbf_ref):
    @pl.when(pl.program_id(1) == 0)
    def _():
        wbf_ref[...] = w_ref[...].astype(BF16)

    y = jnp.dot(h_ref[...], wbf_ref[...], preferred_element_type=F32)
    o_ref[...] = x_ref[...] + mod_ref[0] * y


def out_proj_residual(h, w_out, x, gate_vec, n, tm=512, tn=512):
    m, d = x.shape
    tm = min(tm, n)
    per_b = n // tm
    return pl.pallas_call(
        _out_proj_kernel,
        grid=(d // tn, m // tm),
        in_specs=[
            pl.BlockSpec((tm, d), lambda j, i: (i, 0)),
            pl.BlockSpec((d, tn), lambda j, i: (0, j)),
            pl.BlockSpec((tm, tn), lambda j, i: (i, j)),
            pl.BlockSpec((1, 1, tn), lambda j, i: (i // per_b, 0, j)),
        ],
        out_specs=pl.BlockSpec((tm, tn), lambda j, i: (i, j)),
        out_shape=jax.ShapeDtypeStruct((m, d), F32),
        scratch_shapes=[pltpu.VMEM((d, tn), BF16)],
        compiler_params=_params(2),
        name="out_proj_residual",
    )(h, w_out, x, gate_vec.reshape(-1, 1, d))


def _router_kernel(x_ref, g_ref, sh_ref, sc_ref, wr_ref, br_ref, h_ref, aff_ref):
    y = _rms(x_ref[0], g_ref[...])
    h = (y * (1 + sc_ref[0]) + sh_ref[0]).astype(BF16)
    h_ref[0] = h
    logits = _nt_dot(wr_ref[...].astype(BF16), h) + br_ref[...]
    mx = jnp.max(logits, axis=0, keepdims=True)
    e = jnp.exp(logits - mx)
    aff_ref[0] = e / jnp.sum(e, axis=0, keepdims=True)


def router(x, g, shift, scale, w_router, b_router, tm=256):
    b, n, d = x.shape
    tm = min(tm, n)
    return pl.pallas_call(
        _router_kernel,
        grid=(b, n // tm),
        in_specs=[
            pl.BlockSpec((1, tm, d), lambda i, j: (i, j, 0)),
            pl.BlockSpec((1, d), lambda i, j: (0, 0)),
            pl.BlockSpec((1, 1, d), lambda i, j: (i, 0, 0)),
            pl.BlockSpec((1, 1, d), lambda i, j: (i, 0, 0)),
            pl.BlockSpec((N_EXPERTS, d), lambda i, j: (0, 0)),
            pl.BlockSpec((N_EXPERTS, 1), lambda i, j: (0, 0)),
        ],
        out_specs=[
            pl.BlockSpec((1, tm, d), lambda i, j: (i, j, 0)),
            pl.BlockSpec((1, N_EXPERTS, tm), lambda i, j: (i, 0, j)),
        ],
        out_shape=[jax.ShapeDtypeStruct((b, n, d), BF16),
                   jax.ShapeDtypeStruct((b, N_EXPERTS, n), F32)],
        compiler_params=_params(2),
        name="router",
    )(x, g.reshape(1, d), shift.reshape(b, 1, d), scale.reshape(b, 1, d),
      w_router.T, b_router.reshape(N_EXPERTS, 1))


def _select_kernel(aff_ref, tri_ref, slot_ref, *, cap):
    bits = pltpu.bitcast(aff_ref[0], jnp.int32)

    def step(i, t):
        cand = t | lax.shift_left(jnp.int32(1), 30 - i)
        cnt = jnp.sum(jnp.where(bits >= cand, 1.0, 0.0), axis=1, keepdims=True)
        return jnp.where(cnt >= cap, cand, t)

    t = lax.fori_loop(0, 31, step, jnp.zeros((bits.shape[0], 1), jnp.int32))
    gt = bits > t
    eq = bits == t
    need = cap - jnp.sum(jnp.where(gt, 1.0, 0.0), axis=1, keepdims=True)
    tri = tri_ref[...]
    eq_rank = jnp.dot(jnp.where(eq, 1.0, 0.0).astype(BF16), tri, preferred_element_type=F32)
    sel = gt | (eq & (eq_rank < need))
    slot = jnp.dot(jnp.where(sel, 1.0, 0.0).astype(BF16), tri, preferred_element_type=F32)
    slot_ref[0] = jnp.where(sel, slot.astype(jnp.int32), -1)


def select_tokens(aff, cap):
    b, e, n = aff.shape
    tri = jnp.triu(jnp.ones((n, n), BF16), k=1)
    return pl.pallas_call(
        functools.partial(_select_kernel, cap=cap),
        grid=(b,),
        in_specs=[
            pl.BlockSpec((1, e, n), lambda i: (i, 0, 0)),
            pl.BlockSpec((n, n), lambda i: (0, 0)),
        ],
        out_specs=pl.BlockSpec((1, e, n), lambda i: (i, 0, 0)),
        out_shape=jax.ShapeDtypeStruct((b, e, n), jnp.int32),
        compiler_params=_params(1),
        name="select_tokens",
    )(aff, tri)


def _gather_kernel(h_ref, slot_ref, aff_ref, xs_ref, gate_ref, *, cap):
    slot = slot_ref[0]
    n = slot.shape[1]
    hit = lax.broadcasted_iota(jnp.int32, (cap, n), 0) == slot
    onehot = jnp.where(hit, 1.0, 0.0).astype(BF16)
    xs_ref[0] = jnp.dot(onehot, h_ref[0], preferred_element_type=F32).astype(BF16)
    gate_ref[0] = jnp.sum(jnp.where(hit, aff_ref[0], 0.0), axis=1, keepdims=True)


def gather_tokens(h, slot, aff, cap):
    b, n, d = h.shape
    e = N_EXPERTS
    row = pl.BlockSpec((1, 1, n), lambda i, j: (i * e + j, 0, 0))
    return pl.pallas_call(
        functools.partial(_gather_kernel, cap=cap),
        grid=(b, e),
        in_specs=[pl.BlockSpec((1, n, d), lambda i, j: (i, 0, 0)), row, row],
        out_specs=[
            pl.BlockSpec((1, cap, d), lambda i, j: (j, i, 0)),
            pl.BlockSpec((1, cap, 1), lambda i, j: (j, i, 0)),
        ],
        out_shape=[jax.ShapeDtypeStruct((e, b * cap, d), BF16),
                   jax.ShapeDtypeStruct((e, b * cap, 1), F32)],
        compiler_params=_params(2),
        name="gather_tokens",
    )(h, slot.reshape(b * e, 1, n), aff.reshape(b * e, 1, n))


def _ffn_kernel(xs_ref, gate_ref, wg_ref, wu_ref, wd_ref, y_ref, acc_ref):
    f = pl.program_id(1)

    @pl.when(f == 0)
    def _():
        acc_ref[...] = jnp.zeros_like(acc_ref)

    xs = xs_ref[0]
    hg = jnp.dot(xs, wg_ref[0].astype(BF16), preferred_element_type=F32)
    hu = jnp.dot(xs, wu_ref[0].astype(BF16), preferred_element_type=F32)
    act = (jax.nn.silu(hg) * hu).astype(BF16)
    acc_ref[...] += jnp.dot(act, wd_ref[0].astype(BF16), preferred_element_type=F32)

    @pl.when(f == pl.num_programs(1) - 1)
    def _():
        y_ref[0] = (acc_ref[...] * gate_ref[0]).astype(y_ref.dtype)


def expert_ffn(xs, gates, w_gate, w_up, w_down, tf=256):
    e, r, d = xs.shape
    ff = w_gate.shape[2]
    return pl.pallas_call(
        _ffn_kernel,
        grid=(e, ff // tf),
        in_specs=[
            pl.BlockSpec((1, r, d), lambda i, f: (i, 0, 0)),
            pl.BlockSpec((1, r, 1), lambda i, f: (i, 0, 0)),
            pl.BlockSpec((1, d, tf), lambda i, f: (i, 0, f)),
            pl.BlockSpec((1, d, tf), lambda i, f: (i, 0, f)),
            pl.BlockSpec((1, tf, d), lambda i, f: (i, f, 0)),
        ],
        out_specs=pl.BlockSpec((1, r, d), lambda i, f: (i, 0, 0)),
        out_shape=jax.ShapeDtypeStruct((e, r, d), BF16),
        scratch_shapes=[pltpu.VMEM((r, d), F32)],
        compiler_params=_params(2),
        name="expert_ffn",
    )(xs, gates, w_gate, w_up, w_down)


def _combine_kernel(y_ref, slot_ref, x_ref, mod_ref, o_ref, acc_ref, *, cap):
    e = pl.program_id(2)

    @pl.when(e == 0)
    def _():
        acc_ref[...] = jnp.zeros_like(acc_ref)

    slot = slot_ref[0]
    n = slot.shape[1]
    hit = lax.broadcasted_iota(jnp.int32, (cap, n), 0) == slot
    onehot = jnp.where(hit, 1.0, 0.0).astype(BF16)
    acc_ref[...] += lax.dot_general(onehot, y_ref[0], (((0,), (0,)), ((), ())),
                                    preferred_element_type=F32)

    @pl.when(e == pl.num_programs(2) - 1)
    def _():
        o_ref[0] = x_ref[0] + mod_ref[0] * acc_ref[...]


def combine(y, slot, x, gate_vec, cap, tn=512):
    b, n, d = x.shape
    e = N_EXPERTS
    return pl.pallas_call(
        functools.partial(_combine_kernel, cap=cap),
        grid=(b, d // tn, e),
        in_specs=[
            pl.BlockSpec((1, cap, tn), lambda i, j, k: (k, i, j)),
            pl.BlockSpec((1, 1, n), lambda i, j, k: (i * e + k, 0, 0)),
            pl.BlockSpec((1, n, tn), lambda i, j, k: (i, 0, j)),
            pl.BlockSpec((1, 1, tn), lambda i, j, k: (i, 0, j)),
        ],
        out_specs=pl.BlockSpec((1, n, tn), lambda i, j, k: (i, 0, j)),
        out_shape=jax.ShapeDtypeStruct((b, n, d), F32),
        scratch_shapes=[pltpu.VMEM((n, tn), F32)],
        compiler_params=_params(3),
        name="combine",
    )(y, slot.reshape(b * e, 1, n), x, gate_vec.reshape(b, 1, d))


def moe_block(x, g, mod, w_router, b_router, w_gate, w_up, w_down):
    n = x.shape[1]
    cap = CAPACITY_FACTOR * n // N_EXPERTS
    h, aff = router(x, g, mod[:, 3], mod[:, 4], w_router, b_router)
    slot = select_tokens(aff, cap)
    xs, gates = gather_tokens(h, slot, aff, cap)
    y = expert_ffn(xs, gates, w_gate, w_up, w_down)
    return combine(y, slot, x, mod[:, 5], cap)


def mixer_block(x, p, attn, mod, n, conv_w, ln_g, w_s, b_s, w_branch, w_out):
    b = x.shape[0] // n
    conv = short_conv(p, conv_w, b, n)
    gm = chunk_gmlp(p, ln_g, w_s, b_s)
    merged = merge_branches(attn, conv, gm, p, w_branch)
    return out_proj_residual(merged, w_out, x, mod[:, 2], n)


def kernel(x, c, ctx, c_ctx, w_mod, b_mod, norm1_g, w_in, na_rpb, conv_w, gmlp_ln_g, w_spatial, b_spatial,
           w_branch, w_out, norm2_g, w_router, b_router, w_e_gate, w_e_up, w_e_down, final_g):
    b, n, d = x.shape
    lc = ctx.shape[1]
    cvec = jnp.concatenate([c, c_ctx[None], jnp.zeros((8 - b - 1, d), F32)], axis=0)
    modv = modvec(cvec, w_mod, b_mod).reshape(DEPTH, 8, 6, d)
    cos_t, sin_t = rope_tables(n)
    ctx_s = ctx
    for layer in range(DEPTH):
        last = layer == DEPTH - 1
        mod = modv[layer, :b]
        mod_c = jnp.broadcast_to(modv[layer, b][None], (b, 6, d))
        mix_args = (conv_w[layer], gmlp_ln_g[layer], w_spatial[layer], b_spatial[layer],
                    w_branch[layer], w_out[layer])
        moe_args = (w_router[layer], b_router[layer], w_e_gate[layer], w_e_up[layer], w_e_down[layer])

        hc = norm_mod(ctx_s, norm1_g[layer], mod_c[:, 0], mod_c[:, 1]).reshape(b * lc, d)
        if last:
            kvc = matmul(hc, w_in[layer], col_off=COL_K, n_cols=2 * D_BRANCH, out_dtype=BF16)
            kc_blk = 0
        else:
            pc = matmul(hc, w_in[layer])
            kvc = qkv_cast(pc, lc, cos_t, sin_t, rope=False)
            kc_blk = N_HEADS
            attn_c = ctx_attention(kvc, b, lc)
            ctx_new = mixer_block(ctx_s.reshape(b * lc, d), pc, attn_c, mod_c, lc, *mix_args)
            ctx_new = moe_block(ctx_new.reshape(b, lc, d), norm2_g[layer], mod_c, *moe_args)

        h = norm_mod(x, norm1_g[layer], mod[:, 0], mod[:, 1]).reshape(b * n, d)
        p = matmul(h, w_in[layer])
        qkv = qkv_cast(p, n, cos_t, sin_t, rope=True)
        attn = na_attention(qkv, kvc, kc_blk, na_bias(na_rpb[layer], n // GRID_W), b, n)
        x = mixer_block(x.reshape(b * n, d), p, attn, mod, n, *mix_args)
        x = moe_block(x.reshape(b, n, d), norm2_g[layer], mod, *moe_args)
        if not last:
            ctx_s = ctx_new
    return final_norm(x, final_g)
```

```python
import functools

import numpy as np
import jax
import jax.numpy as jnp
from jax import lax
from jax.experimental import pallas as pl
from jax.experimental.pallas import tpu as pltpu

D_MODEL = 2048
DEPTH = 2
GRID_W = 64
D_BRANCH = D_MODEL // 2
N_BRANCHES = 3
HEAD_DIM = 128
N_HEADS = D_BRANCH // HEAD_DIM
NA_KH = 8
NA_KW = 16
ROPE_THETA = 10000.0
ROPE_AXIS_DIM = HEAD_DIM // 2
CHUNK = 128
GMLP_GROUPS = D_BRANCH // 128
N_EXPERTS = 16
CAPACITY_FACTOR = 2
NORM_EPS = 1e-6
NEG_INF = -1e30

QKV_COLS = 3 * D_BRANCH
COL_XC, COL_BG, COL_CG, COL_U, COL_VS = (i * D_BRANCH for i in range(5))
COL_GATE = 5 * D_BRANCH

NA_Q_ROWS = 4
NA_BAND_ROWS = 12
NA_TQ = NA_Q_ROWS * GRID_W
NA_TK = NA_BAND_ROWS * GRID_W
NA_HEADS_PER_STEP = 4
NA_HB = NA_HEADS_PER_STEP * HEAD_DIM

V7X_VMEM_LIMIT = 56 * 1024 * 1024

BF16 = jnp.bfloat16
F32 = jnp.float32


def _params(n_axes, vmem=V7X_VMEM_LIMIT):
    return pltpu.CompilerParams(dimension_semantics=("arbitrary",) * n_axes, vmem_limit_bytes=vmem)


def _modvec_kernel(c_ref, w_ref, b_ref, o_ref):
    s = jax.nn.silu(c_ref[...]).astype(BF16)
    o_ref[0] = jnp.dot(s, w_ref[0].astype(BF16), preferred_element_type=F32) + b_ref[0]


def modvec(cvec, w_mod, b_mod, tn=1024):
    nl, d, n6 = w_mod.shape
    return pl.pallas_call(
        _modvec_kernel,
        grid=(nl, n6 // tn),
        in_specs=[
            pl.BlockSpec((8, d), lambda l, j: (0, 0)),
            pl.BlockSpec((1, d, tn), lambda l, j: (l, 0, j)),
            pl.BlockSpec((1, 1, tn), lambda l, j: (l, 0, j)),
        ],
        out_specs=pl.BlockSpec((1, 8, tn), lambda l, j: (l, 0, j)),
        out_shape=jax.ShapeDtypeStruct((nl, 8, n6), F32),
        compiler_params=_params(2),
        name="modvec",
    )(cvec, w_mod, b_mod.reshape(nl, 1, n6))


def _rms(x, g):
    ms = jnp.mean(x * x, axis=-1, keepdims=True)
    return x * lax.rsqrt(ms + NORM_EPS) * g


def _norm_mod_kernel(x_ref, g_ref, sh_ref, sc_ref, o_ref):
    y = _rms(x_ref[0], g_ref[...])
    o_ref[0] = (y * (1 + sc_ref[0]) + sh_ref[0]).astype(o_ref.dtype)


def _norm_kernel(x_ref, g_ref, o_ref):
    o_ref[0] = _rms(x_ref[0], g_ref[...]).astype(o_ref.dtype)


def norm_mod(x, g, shift, scale, out_dtype=BF16, tm=256):
    b, n, d = x.shape
    tm = min(tm, n)
    return pl.pallas_call(
        _norm_mod_kernel,
        grid=(b, n // tm),
        in_specs=[
            pl.BlockSpec((1, tm, d), lambda i, j: (i, j, 0)),
            pl.BlockSpec((1, d), lambda i, j: (0, 0)),
            pl.BlockSpec((1, 1, d), lambda i, j: (i, 0, 0)),
            pl.BlockSpec((1, 1, d), lambda i, j: (i, 0, 0)),
        ],
        out_specs=pl.BlockSpec((1, tm, d), lambda i, j: (i, j, 0)),
        out_shape=jax.ShapeDtypeStruct((b, n, d), out_dtype),
        compiler_params=_params(2),
        name="norm_mod",
    )(x, g.reshape(1, d), shift.reshape(b, 1, d), scale.reshape(b, 1, d))


def final_norm(x, g, tm=256):
    b, n, d = x.shape
    return pl.pallas_call(
        _norm_kernel,
        grid=(b, n // tm),
        in_specs=[
            pl.BlockSpec((1, tm, d), lambda i, j: (i, j, 0)),
            pl.BlockSpec((1, d), lambda i, j: (0, 0)),
        ],
        out_specs=pl.BlockSpec((1, tm, d), lambda i, j: (i, j, 0)),
        out_shape=jax.ShapeDtypeStruct((b, n, d), F32),
        compiler_params=_params(2),
        name="final_norm",
    )(x, g.reshape(1, d))


def _mm_kernel(x_ref, w_ref, o_ref, wbf_ref):
    @pl.when(pl.program_id(1) == 0)
    def _():
        wbf_ref[...] = w_ref[...].astype(BF16)

    o_ref[...] = jnp.dot(x_ref[...], wbf_ref[...], preferred_element_type=F32).astype(o_ref.dtype)


def matmul(x, w, layer, col_off, n_cols, out_dtype=F32, tm=1024, tn=1024):
    m, k = x.shape
    tm = min(tm, m)
    off = col_off // tn
    return pl.pallas_call(
        _mm_kernel,
        grid=(n_cols // tn, m // tm),
        in_specs=[
            pl.BlockSpec((tm, k), lambda j, i: (i, 0)),
            pl.BlockSpec((None, k, tn), lambda j, i: (layer, 0, off + j)),
        ],
        out_specs=pl.BlockSpec((tm, tn), lambda j, i: (i, j)),
        out_shape=jax.ShapeDtypeStruct((m, n_cols), out_dtype),
        scratch_shapes=[pltpu.VMEM((k, tn), BF16)],
        compiler_params=_params(2),
        name="matmul",
    )(x, w)


def _qkv_kernel(x_ref, w_ref, cos_ref, sin_ref, o_ref, wbf_ref, *, rope):
    @pl.when(pl.program_id(1) == 0)
    def _():
        wbf_ref[...] = w_ref[...].astype(BF16)

    acc = jnp.dot(x_ref[...], wbf_ref[...], preferred_element_type=F32)
    if not rope:
        o_ref[...] = acc.astype(BF16)
        return
    c = pl.program_id(0)

    @pl.when(c < 2)
    def _():
        cs = cos_ref[...]
        sn = sin_ref[...]
        lane = lax.broadcasted_iota(jnp.int32, cs.shape, 1)
        first_half = (lane % ROPE_AXIS_DIM) < (ROPE_AXIS_DIM // 2)
        for h in range(N_HEADS):
            t = acc[:, h * HEAD_DIM:(h + 1) * HEAD_DIM]
            partner = jnp.where(first_half,
                                pltpu.roll(t, HEAD_DIM - ROPE_AXIS_DIM // 2, 1),
                                pltpu.roll(t, ROPE_AXIS_DIM // 2, 1))
            o_ref[:, h * HEAD_DIM:(h + 1) * HEAD_DIM] = (t * cs + partner * sn).astype(BF16)

    @pl.when(c == 2)
    def _():
        o_ref[...] = acc.astype(BF16)


def rope_tables(n):
    pos = jnp.arange(n)
    rc = jnp.stack([pos // GRID_W, pos % GRID_W], axis=-1).astype(F32)
    inv = ROPE_THETA ** (-jnp.arange(0, ROPE_AXIS_DIM, 2, dtype=F32) / ROPE_AXIS_DIM)
    ang = rc[:, :, None] * inv
    cos, sin = jnp.cos(ang), jnp.sin(ang)
    cos_t = jnp.concatenate([cos, cos], axis=-1).reshape(n, HEAD_DIM)
    sin_t = jnp.concatenate([-sin, sin], axis=-1).reshape(n, HEAD_DIM)
    return cos_t, sin_t


def qkv_proj(x, w, layer, n, cos_t, sin_t, rope, first_tile=0, n_tiles=3, tm=1024):
    m, k = x.shape
    tm = min(tm, n if rope else m)
    nt = n // tm if rope else 1
    tab = pl.BlockSpec((tm, HEAD_DIM), lambda c, i: (i % nt, 0))
    return pl.pallas_call(
        functools.partial(_qkv_kernel, rope=rope),
        grid=(n_tiles, m // tm),
        in_specs=[
            pl.BlockSpec((tm, k), lambda c, i: (i, 0)),
            pl.BlockSpec((None, k, D_BRANCH), lambda c, i: (layer, 0, first_tile + c)),
            tab, tab,
        ],
        out_specs=pl.BlockSpec((tm, D_BRANCH), lambda c, i: (i, c)),
        out_shape=jax.ShapeDtypeStruct((m, n_tiles * D_BRANCH), BF16),
        scratch_shapes=[pltpu.VMEM((k, D_BRANCH), BF16)],
        compiler_params=_params(2),
        name="qkv_proj",
    )(x, w, cos_t, sin_t)


def _nt_dot(a, b):
    return lax.dot_general(a, b, (((1,), (1,)), ((), ())), preferred_element_type=F32)


def _na_pattern_rows(g, rows):
    band_row = int(np.clip(g * NA_Q_ROWS - NA_KH // 2, 0, rows - NA_BAND_ROWS))
    qr = g * NA_Q_ROWS + np.arange(NA_Q_ROWS)
    rs = np.clip(qr - NA_KH // 2, 0, rows - NA_KH)
    return band_row, qr, rs


def _na_bias_kernel(rpb_ref, o_ref, *, rows):
    shape = (GRID_W, 2 * GRID_W)
    qc = lax.broadcasted_iota(jnp.int32, shape, 0)
    lane = lax.broadcasted_iota(jnp.int32, shape, 1)
    low = lane < GRID_W
    kc = lane % GRID_W
    ws = jnp.clip(qc - NA_KW // 2, 0, GRID_W - NA_KW)
    col_ok = (kc >= ws) & (kc < ws + NA_KW)
    neg = jnp.full(shape, NEG_INF, F32)

    tiles = []
    for dr in range(2 * NA_KH - 1):
        t = jnp.broadcast_to(rpb_ref[0, 0, dr:dr + 1, :], shape)
        t = pltpu.roll(t, 2 * GRID_W - (NA_KW - 1), 1)
        for bit in range(6):
            t = jnp.where(((qc >> bit) & 1) == 1, pltpu.roll(t, 1 << bit, 1), t)
        t = jnp.where(low, t, pltpu.roll(t, GRID_W, 1))
        tiles.append(jnp.where(col_ok, t, NEG_INF))

    n_groups = rows // NA_Q_ROWS
    for p, g in enumerate((0, 1, n_groups - 1)):
        band_row, qr, rs = _na_pattern_rows(g, rows)
        for qi in range(NA_Q_ROWS):
            def half(krl):
                kr = band_row + krl
                if rs[qi] <= kr < rs[qi] + NA_KH:
                    return tiles[kr - qr[qi] + NA_KH - 1]
                return neg
            for pair in range(NA_BAND_ROWS // 2):
                blk = jnp.where(low, half(2 * pair), half(2 * pair + 1))
                o_ref[0, p, 0, qi * GRID_W:(qi + 1) * GRID_W, pair * 128:(pair + 1) * 128] = blk


def na_bias(na_rpb, rows):
    nl, nh, ndr, ndc = na_rpb.shape
    band1, qr1, rs1 = _na_pattern_rows(1, rows)
    for g in range(2, rows // NA_Q_ROWS - 1):
        band, qr, rs = _na_pattern_rows(g, rows)
        assert (qr - band == qr1 - band1).all() and (rs - band == rs1 - band1).all()
    rpb = jnp.pad(na_rpb, ((0, 0), (0, 0), (0, 16 - ndr), (0, 128 - ndc)))
    return pl.pallas_call(
        functools.partial(_na_bias_kernel, rows=rows),
        grid=(nl, nh),
        in_specs=[pl.BlockSpec((1, 1, 16, 128), lambda l, h: (l, h, 0, 0))],
        out_specs=pl.BlockSpec((1, 3, 1, NA_TQ, NA_TK), lambda l, h: (l, 0, h, 0, 0)),
        out_shape=jax.ShapeDtypeStruct((nl, 3, nh, NA_TQ, NA_TK), F32),
        compiler_params=_params(2),
        name="na_bias",
    )(rpb)


def _na_kernel(q_ref, k_ref, v_ref, kc_ref, vc_ref, bias_ref, o_ref):
    g = pl.program_id(2)
    n_groups = k_ref.shape[1] // NA_TQ
    band_row = jnp.clip(g * NA_Q_ROWS - NA_KH // 2, 0, n_groups * NA_Q_ROWS - NA_BAND_ROWS)
    start = pl.multiple_of(band_row * GRID_W, NA_TQ)
    pat = jnp.where(g == 0, 0, jnp.where(g == n_groups - 1, 2, 1))
    scale = HEAD_DIM ** -0.5
    for hh in range(NA_HEADS_PER_STEP):
        cols = slice(hh * HEAD_DIM, (hh + 1) * HEAD_DIM)
        q = q_ref[0, :, cols]
        kb = k_ref[0, pl.ds(start, NA_TK), cols]
        vb = v_ref[0, pl.ds(start, NA_TK), cols]
        s_loc = _nt_dot(q, kb) * scale + bias_ref[0, pat, hh]
        s_ctx = _nt_dot(q, kc_ref[0, :, cols]) * scale
        m = jnp.maximum(jnp.max(s_loc, axis=1, keepdims=True), jnp.max(s_ctx, axis=1, keepdims=True))
        e_loc = jnp.exp(s_loc - m)
        e_ctx = jnp.exp(s_ctx - m)
        inv = 1.0 / (jnp.sum(e_loc, axis=1, keepdims=True) + jnp.sum(e_ctx, axis=1, keepdims=True))
        o = jnp.dot((e_loc * inv).astype(BF16), vb, preferred_element_type=F32)
        o = o + jnp.dot((e_ctx * inv).astype(BF16), vc_ref[0, :, cols], preferred_element_type=F32)
        o_ref[0, :, cols] = o.astype(o_ref.dtype)


def na_attention(qkv, kvc, kc_col, bias, layer, b, n):
    lc = kvc.shape[0] // b
    qkv3 = qkv.reshape(b, n, QKV_COLS)
    kvc3 = kvc.reshape(b, lc, kvc.shape[1])
    hbs = D_BRANCH // NA_HB
    kcb = kc_col // NA_HB
    out = pl.pallas_call(
        _na_kernel,
        grid=(hbs, b, n // NA_TQ),
        in_specs=[
            pl.BlockSpec((1, NA_TQ, NA_HB), lambda h, i, g: (i, g, h)),
            pl.BlockSpec((1, n, NA_HB), lambda h, i, g: (i, 0, hbs + h)),
            pl.BlockSpec((1, n, NA_HB), lambda h, i, g: (i, 0, 2 * hbs + h)),
            pl.BlockSpec((1, lc, NA_HB), lambda h, i, g: (i, 0, kcb + h)),
            pl.BlockSpec((1, lc, NA_HB), lambda h, i, g: (i, 0, kcb + hbs + h)),
            pl.BlockSpec((1, 3, NA_HEADS_PER_STEP, NA_TQ, NA_TK), lambda h, i, g: (layer, 0, h, 0, 0)),
        ],
        out_specs=pl.BlockSpec((1, NA_TQ, NA_HB), lambda h, i, g: (i, g, h)),
        out_shape=jax.ShapeDtypeStruct((b, n, D_BRANCH), BF16),
        compiler_params=_params(3),
        name="na_attention",
    )(qkv3, qkv3, qkv3, kvc3, kvc3, bias)
    return out.reshape(b * n, D_BRANCH)


def _ctx_attn_kernel(q_ref, k_ref, v_ref, o_ref):
    s = _nt_dot(q_ref[0], k_ref[0]) * (HEAD_DIM ** -0.5)
    m = jnp.max(s, axis=1, keepdims=True)
    e = jnp.exp(s - m)
    p = e * (1.0 / jnp.sum(e, axis=1, keepdims=True))
    o_ref[0] = jnp.dot(p.astype(BF16), v_ref[0], preferred_element_type=F32).astype(o_ref.dtype)


def ctx_attention(qkv, b, lc):
    qkv3 = qkv.reshape(b, lc, QKV_COLS)
    out = pl.pallas_call(
        _ctx_attn_kernel,
        grid=(b, N_HEADS),
        in_specs=[
            pl.BlockSpec((1, lc, HEAD_DIM), lambda i, h: (i, 0, h)),
            pl.BlockSpec((1, lc, HEAD_DIM), lambda i, h: (i, 0, N_HEADS + h)),
            pl.BlockSpec((1, lc, HEAD_DIM), lambda i, h: (i, 0, 2 * N_HEADS + h)),
        ],
        out_specs=pl.BlockSpec((1, lc, HEAD_DIM), lambda i, h: (i, 0, h)),
        out_shape=jax.ShapeDtypeStruct((b, lc, D_BRANCH), BF16),
        compiler_params=_params(2),
        name="ctx_attention",
    )(qkv3, qkv3, qkv3)
    return out.reshape(b * lc, D_BRANCH)


def _conv_kernel(xc_ref, bg_ref, cg_ref, w_ref, o_ref):
    z = cg_ref[0] * xc_ref[0]
    n = z.shape[0]
    row = lax.broadcasted_iota(jnp.int32, z.shape, 0)
    z_prev = jnp.where(row == 0, 0.0, pltpu.roll(z, 1, 0))
    z_next = jnp.where(row == n - 1, 0.0, pltpu.roll(z, n - 1, 0))
    y = z_prev * w_ref[0, 0:1, :] + z * w_ref[0, 1:2, :] + z_next * w_ref[0, 2:3, :]
    o_ref[0] = (bg_ref[0] * y).astype(o_ref.dtype)


def short_conv(p, conv_w, layer, b, n, tc=256):
    p3 = p.reshape(b, n, p.shape[1])
    blk = lambda col: pl.BlockSpec((1, n, tc), lambda i, j: (i, 0, col // tc + j))
    out = pl.pallas_call(
        _conv_kernel,
        grid=(b, D_BRANCH // tc),
        in_specs=[blk(COL_XC), blk(COL_BG), blk(COL_CG),
                  pl.BlockSpec((1, 3, tc), lambda i, j: (layer, 0, j))],
        out_specs=pl.BlockSpec((1, n, tc), lambda i, j: (i, 0, j)),
        out_shape=jax.ShapeDtypeStruct((b, n, D_BRANCH), BF16),
        compiler_params=_params(2),
        name="short_conv",
    )(p3, p3, p3, conv_w)
    return out.reshape(b * n, D_BRANCH)


def _gmlp_kernel(u_ref, v_ref, g_ref, ws_ref, bs_ref, o_ref):
    v = jax.nn.gelu(v_ref[...])
    mu = jnp.mean(v, axis=-1, keepdims=True)
    var = jnp.mean(jnp.square(v - mu), axis=-1, keepdims=True)
    vn = ((v - mu) * lax.rsqrt(var + NORM_EPS) * g_ref[...]).astype(BF16)
    for grp in range(GMLP_GROUPS):
        cols = slice(grp * 128, (grp + 1) * 128)
        s = jnp.dot(ws_ref[0, grp].astype(BF16), vn[:, cols], preferred_element_type=F32)
        s = s + bs_ref[:, grp:grp + 1]
        o_ref[:, cols] = (jax.nn.gelu(u_ref[:, cols]) * s).astype(o_ref.dtype)


def chunk_gmlp(p, ln_g, w_s, b_s, layer):
    m = p.shape[0]
    return pl.pallas_call(
        _gmlp_kernel,
        grid=(m // CHUNK,),
        in_specs=[
            pl.BlockSpec((CHUNK, D_BRANCH), lambda i: (i, COL_U // D_BRANCH)),
            pl.BlockSpec((CHUNK, D_BRANCH), lambda i: (i, COL_VS // D_BRANCH)),
            pl.BlockSpec((1, D_BRANCH), lambda i: (0, 0)),
            pl.BlockSpec((1, GMLP_GROUPS, CHUNK, CHUNK), lambda i: (layer, 0, 0, 0)),
            pl.BlockSpec((CHUNK, GMLP_GROUPS), lambda i: (0, 0)),
        ],
        out_specs=pl.BlockSpec((CHUNK, D_BRANCH), lambda i: (i, 0)),
        out_shape=jax.ShapeDtypeStruct((m, D_BRANCH), BF16),
        compiler_params=_params(1),
        name="chunk_gmlp",
    )(p, p, ln_g[layer].reshape(1, D_BRANCH), w_s, b_s[layer].T)


def _merge_kernel(a_ref, c_ref, m_ref, g0_ref, g1_ref, g2_ref, w_ref, o_ref, wbf_ref):
    @pl.when(pl.program_id(1) == 0)
    def _():
        wbf_ref[...] = w_ref[...].astype(BF16)

    acc = None
    for i, (br, gl) in enumerate(((a_ref, g0_ref), (c_ref, g1_ref), (m_ref, g2_ref))):
        proj = jnp.dot(br[...], wbf_ref[i], preferred_element_type=F32)
        term = jax.nn.sigmoid(gl[...]) * proj
        acc = term if acc is None else acc + term
    o_ref[...] = acc.astype(o_ref.dtype)


def merge_branches(attn, conv, gm, p, w_branch, layer, tm=512, tn=512):
    m = attn.shape[0]
    br = pl.BlockSpec((tm, D_BRANCH), lambda j, i: (i, 0))
    gate = lambda k: pl.BlockSpec((tm, tn), lambda j, i: (i, (COL_GATE + k * D_MODEL) // tn + j))
    return pl.pallas_call(
        _merge_kernel,
        grid=(D_MODEL // tn, m // tm),
        in_specs=[br, br, br, gate(0), gate(1), gate(2),
                  pl.BlockSpec((None, N_BRANCHES, D_BRANCH, tn), lambda j, i: (layer, 0, 0, j))],
        out_specs=pl.BlockSpec((tm, tn), lambda j, i: (i, j)),
        out_shape=jax.ShapeDtypeStruct((m, D_MODEL), BF16),
        scratch_shapes=[pltpu.VMEM((N_BRANCHES, D_BRANCH, tn), BF16)],
        compiler_params=_params(2),
        name="merge_branches",
    )(attn, conv, gm, p, p, p, w_branch)


def _out_proj_kernel(h_ref, w_ref, x_ref, mod_ref, o_ref, wbf_ref):
    @pl.when(pl.program_id(1) == 0)
    def _():
        wbf_ref[...] = w_ref[...].astype(BF16)

    y = jnp.dot(h_ref[...], wbf_ref[...], preferred_element_type=F32)
    o_ref[...] = x_ref[...] + mod_ref[0] * y


def out_proj_residual(h, w_out, layer, x, gate_vec, n, tm=512, tn=512):
    m, d = x.shape
    tm = min(tm, n)
    per_b = n // tm
    return pl.pallas_call(
        _out_proj_kernel,
        grid=(d // tn, m // tm),
        in_specs=[
            pl.BlockSpec((tm, d), lambda j, i: (i, 0)),
            pl.BlockSpec((None, d, tn), lambda j, i: (layer, 0, j)),
            pl.BlockSpec((tm, tn), lambda j, i: (i, j)),
            pl.BlockSpec((1, 1, tn), lambda j, i: (i // per_b, 0, j)),
        ],
        out_specs=pl.BlockSpec((tm, tn), lambda j, i: (i, j)),
        out_shape=jax.ShapeDtypeStruct((m, d), F32),
        scratch_shapes=[pltpu.VMEM((d, tn), BF16)],
        compiler_params=_params(2),
        name="out_proj_residual",
    )(h, w_out, x, gate_vec.reshape(-1, 1, d))


def _router_kernel(x_ref, g_ref, sh_ref, sc_ref, wr_ref, br_ref, h_ref, aff_ref):
    y = _rms(x_ref[0], g_ref[...])
    h = (y * (1 + sc_ref[0]) + sh_ref[0]).astype(BF16)
    h_ref[0] = h
    logits = _nt_dot(wr_ref[...].astype(BF16), h) + br_ref[...]
    mx = jnp.max(logits, axis=0, keepdims=True)
    e = jnp.exp(logits - mx)
    aff_ref[0] = e / jnp.sum(e, axis=0, keepdims=True)


def router(x, g, shift, scale, w_router, b_router, tm=256):
    b, n, d = x.shape
    tm = min(tm, n)
    return pl.pallas_call(
        _router_kernel,
        grid=(b, n // tm),
        in_specs=[
            pl.BlockSpec((1, tm, d), lambda i, j: (i, j, 0)),
            pl.BlockSpec((1, d), lambda i, j: (0, 0)),
            pl.BlockSpec((1, 1, d), lambda i, j: (i, 0, 0)),
            pl.BlockSpec((1, 1, d), lambda i, j: (i, 0, 0)),
            pl.BlockSpec((N_EXPERTS, d), lambda i, j: (0, 0)),
            pl.BlockSpec((N_EXPERTS, 1), lambda i, j: (0, 0)),
        ],
        out_specs=[
            pl.BlockSpec((1, tm, d), lambda i, j: (i, j, 0)),
            pl.BlockSpec((1, N_EXPERTS, tm), lambda i, j: (i, 0, j)),
        ],
        out_shape=[jax.ShapeDtypeStruct((b, n, d), BF16),
                   jax.ShapeDtypeStruct((b, N_EXPERTS, n), F32)],
        compiler_params=_params(2),
        name="router",
    )(x, g.reshape(1, d), shift.reshape(b, 1, d), scale.reshape(b, 1, d),
      w_router.T, b_router.reshape(N_EXPERTS, 1))


def _select_kernel(aff_ref, tri_ref, slot_ref, *, cap):
    bits = pltpu.bitcast(aff_ref[0], jnp.int32)

    def step(i, t):
        cand = t | lax.shift_left(jnp.int32(1), 30 - i)
        cnt = jnp.sum(jnp.where(bits >= cand, 1.0, 0.0), axis=1, keepdims=True)
        return jnp.where(cnt >= cap, cand, t)

    t = lax.fori_loop(0, 31, step, jnp.zeros((bits.shape[0], 1), jnp.int32))
    gt = bits > t
    eq = bits == t
    need = cap - jnp.sum(jnp.where(gt, 1.0, 0.0), axis=1, keepdims=True)
    tri = tri_ref[...]
    eq_rank = jnp.dot(jnp.where(eq, 1.0, 0.0).astype(BF16), tri, preferred_element_type=F32)
    sel = gt | (eq & (eq_rank < need))
    slot = jnp.dot(jnp.where(sel, 1.0, 0.0).astype(BF16), tri, preferred_element_type=F32)
    slot_ref[0] = jnp.where(sel, slot.astype(jnp.int32), -1)


def select_tokens(aff, cap):
    b, e, n = aff.shape
    tri = jnp.triu(jnp.ones((n, n), BF16), k=1)
    return pl.pallas_call(
        functools.partial(_select_kernel, cap=cap),
        grid=(b,),
        in_specs=[
            pl.BlockSpec((1, e, n), lambda i: (i, 0, 0)),
            pl.BlockSpec((n, n), lambda i: (0, 0)),
        ],
        out_specs=pl.BlockSpec((1, e, n), lambda i: (i, 0, 0)),
        out_shape=jax.ShapeDtypeStruct((b, e, n), jnp.int32),
        compiler_params=_params(1),
        name="select_tokens",
    )(aff, tri)


def _onehot(slot, cap):
    hit = lax.broadcasted_iota(jnp.int32, (cap, slot.shape[1]), 0) == slot
    return hit, jnp.where(hit, 1.0, 0.0)


def _gather_kernel(h_ref, slot_ref, aff_ref, xs_ref, gate_ref, *, cap):
    hit, onehot = _onehot(slot_ref[0], cap)
    xs_ref[0] = jnp.dot(onehot.astype(BF16), h_ref[0], preferred_element_type=F32).astype(BF16)
    gate_ref[0] = jnp.sum(jnp.where(hit, aff_ref[0], 0.0), axis=1, keepdims=True)


def gather_tokens(h, slot, aff, cap):
    b, n, d = h.shape
    e = N_EXPERTS
    row = pl.BlockSpec((1, 1, n), lambda i, j: (i * e + j, 0, 0))
    return pl.pallas_call(
        functools.partial(_gather_kernel, cap=cap),
        grid=(b, e),
        in_specs=[pl.BlockSpec((1, n, d), lambda i, j: (i, 0, 0)), row, row],
        out_specs=[
            pl.BlockSpec((1, cap, d), lambda i, j: (j, i, 0)),
            pl.BlockSpec((1, cap, 1), lambda i, j: (j, i, 0)),
        ],
        out_shape=[jax.ShapeDtypeStruct((e, b * cap, d), BF16),
                   jax.ShapeDtypeStruct((e, b * cap, 1), F32)],
        compiler_params=_params(2),
        name="gather_tokens",
    )(h, slot.reshape(b * e, 1, n), aff.reshape(b * e, 1, n))


def _ffn_kernel(*refs, n_sets):
    xs_refs, gate_refs = refs[0:2 * n_sets:2], refs[1:2 * n_sets:2]
    wg_ref, wu_ref, wd_ref = refs[2 * n_sets:2 * n_sets + 3]
    y_refs = refs[2 * n_sets + 3:3 * n_sets + 3]
    acc_refs = refs[3 * n_sets + 3:]
    f = pl.program_id(1)
    wg = wg_ref[...].astype(BF16)
    wu = wu_ref[...].astype(BF16)
    wd = wd_ref[...].astype(BF16)
    for xs_ref, gate_ref, y_ref, acc_ref in zip(xs_refs, gate_refs, y_refs, acc_refs):
        @pl.when(f == 0)
        def _():
            acc_ref[...] = jnp.zeros_like(acc_ref)

        xs = xs_ref[0]
        hg = jnp.dot(xs, wg, preferred_element_type=F32)
        hu = jnp.dot(xs, wu, preferred_element_type=F32)
        act = (jax.nn.silu(hg) * hu).astype(BF16)
        acc_ref[...] += jnp.dot(act, wd, preferred_element_type=F32)

        @pl.when(f == pl.num_programs(1) - 1)
        def _():
            y_ref[0] = (acc_ref[...] * gate_ref[0]).astype(y_ref.dtype)


def expert_ffn(sets, w_gate, w_up, w_down, layer, tf=256):
    d, ff = w_gate.shape[2], w_gate.shape[3]
    in_specs, operands, out_specs, out_shape, scratch = [], [], [], [], []
    for xs, gates in sets:
        r = xs.shape[1]
        in_specs += [pl.BlockSpec((1, r, d), lambda i, f: (i, 0, 0)),
                     pl.BlockSpec((1, r, 1), lambda i, f: (i, 0, 0))]
        operands += [xs, gates]
        out_specs.append(pl.BlockSpec((1, r, d), lambda i, f: (i, 0, 0)))
        out_shape.append(jax.ShapeDtypeStruct(xs.shape, BF16))
        scratch.append(pltpu.VMEM((r, d), F32))
    in_specs += [
        pl.BlockSpec((None, None, d, tf), lambda i, f: (layer, i, 0, f)),
        pl.BlockSpec((None, None, d, tf), lambda i, f: (layer, i, 0, f)),
        pl.BlockSpec((None, None, tf, d), lambda i, f: (layer, i, f, 0)),
    ]
    return pl.pallas_call(
        functools.partial(_ffn_kernel, n_sets=len(sets)),
        grid=(N_EXPERTS, ff // tf),
        in_specs=in_specs,
        out_specs=out_specs,
        out_shape=out_shape,
        scratch_shapes=scratch,
        compiler_params=_params(2),
        name="expert_ffn",
    )(*operands, w_gate, w_up, w_down)


def _combine_kernel(y_ref, slot_ref, x_ref, mod_ref, o_ref, onehot_ref, *, cap):
    @pl.when(pl.program_id(1) == 0)
    def _():
        for e in range(N_EXPERTS):
            onehot_ref[e * cap:(e + 1) * cap, :] = _onehot(slot_ref[e], cap)[1].astype(BF16)

    y = y_ref[...].reshape(N_EXPERTS * cap, y_ref.shape[2])
    moe = lax.dot_general(onehot_ref[...], y, (((0,), (0,)), ((), ())), preferred_element_type=F32)
    o_ref[0] = x_ref[0] + mod_ref[0] * moe


def combine(y, slot, x, gate_vec, cap, tn=512):
    b, n, d = x.shape
    e = N_EXPERTS
    return pl.pallas_call(
        functools.partial(_combine_kernel, cap=cap),
        grid=(b, d // tn),
        in_specs=[
            pl.BlockSpec((e, cap, tn), lambda i, j: (0, i, j)),
            pl.BlockSpec((e, 1, n), lambda i, j: (i, 0, 0)),
            pl.BlockSpec((1, n, tn), lambda i, j: (i, 0, j)),
            pl.BlockSpec((1, 1, tn), lambda i, j: (i, 0, j)),
        ],
        out_specs=pl.BlockSpec((1, n, tn), lambda i, j: (i, 0, j)),
        out_shape=jax.ShapeDtypeStruct((b, n, d), F32),
        scratch_shapes=[pltpu.VMEM((e * cap, n), BF16)],
        compiler_params=_params(2),
        name="combine",
    )(y, slot.reshape(b * e, 1, n), x, gate_vec.reshape(b, 1, d))


def moe_route(x, g, mod, w_router, b_router):
    n = x.shape[1]
    cap = CAPACITY_FACTOR * n // N_EXPERTS
    h, aff = router(x, g, mod[:, 3], mod[:, 4], w_router, b_router)
    slot = select_tokens(aff, cap)
    xs, gates = gather_tokens(h, slot, aff, cap)
    return slot, cap, (xs, gates)


def mixer_block(x, p, attn, mod, n, layer, conv_w, ln_g, w_s, b_s, w_branch, w_out):
    b = x.shape[0] // n
    conv = short_conv(p, conv_w, layer, b, n)
    gm = chunk_gmlp(p, ln_g, w_s, b_s, layer)
    merged = merge_branches(attn, conv, gm, p, w_branch, layer)
    return out_proj_residual(merged, w_out, layer, x, mod[:, 2], n)


def kernel(x, c, ctx, c_ctx, w_mod, b_mod, norm1_g, w_in, na_rpb, conv_w, gmlp_ln_g, w_spatial, b_spatial,
           w_branch, w_out, norm2_g, w_router, b_router, w_e_gate, w_e_up, w_e_down, final_g):
    b, n, d = x.shape
    lc = ctx.shape[1]
    rest_cols = w_in.shape[2] - QKV_COLS
    cvec = jnp.concatenate([c, c_ctx[None], jnp.zeros((8 - b - 1, d), F32)], axis=0)
    modv = modvec(cvec, w_mod, b_mod).reshape(DEPTH, 8, 6, d)
    cos_t, sin_t = rope_tables(n)
    bias = na_bias(na_rpb, n // GRID_W)
    mix_w = (conv_w, gmlp_ln_g, w_spatial, b_spatial, w_branch, w_out)
    ctx_s = ctx
    for layer in range(DEPTH):
        last = layer == DEPTH - 1
        mod = modv[layer, :b]
        mod_c = jnp.broadcast_to(modv[layer, b][None], (b, 6, d))
        route_w = (w_router[layer], b_router[layer])

        hc = norm_mod(ctx_s, norm1_g[layer], mod_c[:, 0], mod_c[:, 1]).reshape(b * lc, d)
        if last:
            kvc = qkv_proj(hc, w_in, layer, lc, cos_t, sin_t, rope=False, first_tile=1, n_tiles=2)
            kc_col = 0
        else:
            kvc = qkv_proj(hc, w_in, layer, lc, cos_t, sin_t, rope=False)
            kc_col = D_BRANCH
            pc = matmul(hc, w_in, layer, QKV_COLS, rest_cols)
            attn_c = ctx_attention(kvc, b, lc)
            ctx_mid = mixer_block(ctx_s.reshape(b * lc, d), pc, attn_c, mod_c, lc, layer, *mix_w)
            ctx_mid = ctx_mid.reshape(b, lc, d)

        h = norm_mod(x, norm1_g[layer], mod[:, 0], mod[:, 1]).reshape(b * n, d)
        qkv = qkv_proj(h, w_in, layer, n, cos_t, sin_t, rope=True)
        p = matmul(h, w_in, layer, QKV_COLS, rest_cols)
        attn = na_attention(qkv, kvc, kc_col, bias, layer, b, n)
        x = mixer_block(x.reshape(b * n, d), p, attn, mod, n, layer, *mix_w).reshape(b, n, d)

        slot, cap, rows = moe_route(x, norm2_g[layer], mod, *route_w)
        if last:
            (y,) = expert_ffn([rows], w_e_gate, w_e_up, w_e_down, layer)
        else:
            slot_c, cap_c, rows_c = moe_route(ctx_mid, norm2_g[layer], mod_c, *route_w)
            y, y_c = expert_ffn([rows, rows_c], w_e_gate, w_e_up, w_e_down, layer)
            ctx_s = combine(y_c, slot_c, ctx_mid, mod_c[:, 5], cap_c)
        x = combine(y, slot, x, mod[:, 5], cap)
    return final_norm(x, final_g)
```

```python
import functools

import numpy as np
import jax
import jax.numpy as jnp
from jax import lax
from jax.experimental import pallas as pl
from jax.experimental.pallas import tpu as pltpu

D_MODEL = 2048
DEPTH = 2
GRID_W = 64
D_BRANCH = D_MODEL // 2
N_BRANCHES = 3
HEAD_DIM = 128
N_HEADS = D_BRANCH // HEAD_DIM
NA_KH = 8
NA_KW = 16
ROPE_THETA = 10000.0
ROPE_AXIS_DIM = HEAD_DIM // 2
CHUNK = 128
GMLP_GROUPS = D_BRANCH // 128
N_EXPERTS = 16
CAPACITY_FACTOR = 2
NORM_EPS = 1e-6
NEG_INF = -1e30
LOG2_E = float(np.log2(np.e))

QKV_COLS = 3 * D_BRANCH
COL_XC, COL_BG, COL_CG, COL_U, COL_VS = (i * D_BRANCH for i in range(5))
COL_GATE = 5 * D_BRANCH

NA_Q_ROWS = 4
NA_BAND_ROWS = 12
NA_TQ = NA_Q_ROWS * GRID_W
NA_TK = NA_BAND_ROWS * GRID_W
NA_HEADS_PER_STEP = 4
NA_HB = NA_HEADS_PER_STEP * HEAD_DIM

V7X_VMEM_LIMIT = 56 * 1024 * 1024

BF16 = jnp.bfloat16
F32 = jnp.float32


def _params(n_axes, vmem=V7X_VMEM_LIMIT):
    return pltpu.CompilerParams(dimension_semantics=("arbitrary",) * n_axes, vmem_limit_bytes=vmem)


def _modvec_kernel(c_ref, w_ref, b_ref, o_ref):
    s = jax.nn.silu(c_ref[...]).astype(BF16)
    o_ref[0] = jnp.dot(s, w_ref[0].astype(BF16), preferred_element_type=F32) + b_ref[0]


def modvec(cvec, w_mod, b_mod, tn=1024):
    nl, d, n6 = w_mod.shape
    return pl.pallas_call(
        _modvec_kernel,
        grid=(nl, n6 // tn),
        in_specs=[
            pl.BlockSpec((8, d), lambda l, j: (0, 0)),
            pl.BlockSpec((1, d, tn), lambda l, j: (l, 0, j)),
            pl.BlockSpec((1, 1, tn), lambda l, j: (l, 0, j)),
        ],
        out_specs=pl.BlockSpec((1, 8, tn), lambda l, j: (l, 0, j)),
        out_shape=jax.ShapeDtypeStruct((nl, 8, n6), F32),
        compiler_params=_params(2),
        name="modvec",
    )(cvec, w_mod, b_mod.reshape(nl, 1, n6))


def _rms(x, g):
    ms = jnp.mean(x * x, axis=-1, keepdims=True)
    return x * lax.rsqrt(ms + NORM_EPS) * g


def _norm_mod_kernel(x_ref, g_ref, sh_ref, sc_ref, o_ref):
    y = _rms(x_ref[0], g_ref[...])
    o_ref[0] = (y * (1 + sc_ref[0]) + sh_ref[0]).astype(o_ref.dtype)


def _norm_kernel(x_ref, g_ref, o_ref):
    o_ref[0] = _rms(x_ref[0], g_ref[...]).astype(o_ref.dtype)


def norm_mod(x, g, shift, scale, out_dtype=BF16, tm=256):
    b, n, d = x.shape
    tm = min(tm, n)
    return pl.pallas_call(
        _norm_mod_kernel,
        grid=(b, n // tm),
        in_specs=[
            pl.BlockSpec((1, tm, d), lambda i, j: (i, j, 0)),
            pl.BlockSpec((1, d), lambda i, j: (0, 0)),
            pl.BlockSpec((1, 1, d), lambda i, j: (i, 0, 0)),
            pl.BlockSpec((1, 1, d), lambda i, j: (i, 0, 0)),
        ],
        out_specs=pl.BlockSpec((1, tm, d), lambda i, j: (i, j, 0)),
        out_shape=jax.ShapeDtypeStruct((b, n, d), out_dtype),
        compiler_params=_params(2),
        name="norm_mod",
    )(x, g.reshape(1, d), shift.reshape(b, 1, d), scale.reshape(b, 1, d))


def final_norm(x, g, tm=256):
    b, n, d = x.shape
    return pl.pallas_call(
        _norm_kernel,
        grid=(b, n // tm),
        in_specs=[
            pl.BlockSpec((1, tm, d), lambda i, j: (i, j, 0)),
            pl.BlockSpec((1, d), lambda i, j: (0, 0)),
        ],
        out_specs=pl.BlockSpec((1, tm, d), lambda i, j: (i, j, 0)),
        out_shape=jax.ShapeDtypeStruct((b, n, d), F32),
        compiler_params=_params(2),
        name="final_norm",
    )(x, g.reshape(1, d))


def _mm_kernel(x_ref, w_ref, o_ref, wbf_ref):
    @pl.when(pl.program_id(1) == 0)
    def _():
        wbf_ref[...] = w_ref[...].astype(BF16)

    o_ref[...] = jnp.dot(x_ref[...], wbf_ref[...], preferred_element_type=F32).astype(o_ref.dtype)


def matmul(x, w, layer, col_off, n_cols, out_dtype=F32, tm=1024, tn=1024):
    m, k = x.shape
    tm = min(tm, m)
    off = col_off // tn
    return pl.pallas_call(
        _mm_kernel,
        grid=(n_cols // tn, m // tm),
        in_specs=[
            pl.BlockSpec((tm, k), lambda j, i: (i, 0)),
            pl.BlockSpec((None, k, tn), lambda j, i: (layer, 0, off + j)),
        ],
        out_specs=pl.BlockSpec((tm, tn), lambda j, i: (i, j)),
        out_shape=jax.ShapeDtypeStruct((m, n_cols), out_dtype),
        scratch_shapes=[pltpu.VMEM((k, tn), BF16)],
        compiler_params=_params(2),
        name="matmul",
    )(x, w)


def _qkv_kernel(x_ref, w_ref, cos_ref, sin_ref, o_ref, wbf_ref, *, rope):
    @pl.when(pl.program_id(1) == 0)
    def _():
        wbf_ref[...] = w_ref[...].astype(BF16)

    acc = jnp.dot(x_ref[...], wbf_ref[...], preferred_element_type=F32)
    if not rope:
        o_ref[...] = acc.astype(BF16)
        return
    c = pl.program_id(0)

    @pl.when(c < 2)
    def _():
        cs = cos_ref[...]
        sn = sin_ref[...]
        lane = lax.broadcasted_iota(jnp.int32, cs.shape, 1)
        first_half = (lane % ROPE_AXIS_DIM) < (ROPE_AXIS_DIM // 2)
        for h in range(N_HEADS):
            t = acc[:, h * HEAD_DIM:(h + 1) * HEAD_DIM]
            partner = jnp.where(first_half,
                                pltpu.roll(t, HEAD_DIM - ROPE_AXIS_DIM // 2, 1),
                                pltpu.roll(t, ROPE_AXIS_DIM // 2, 1))
            o_ref[:, h * HEAD_DIM:(h + 1) * HEAD_DIM] = (t * cs + partner * sn).astype(BF16)

    @pl.when(c == 2)
    def _():
        o_ref[...] = acc.astype(BF16)


def rope_tables(n):
    pos = jnp.arange(n)
    rc = jnp.stack([pos // GRID_W, pos % GRID_W], axis=-1).astype(F32)
    inv = ROPE_THETA ** (-jnp.arange(0, ROPE_AXIS_DIM, 2, dtype=F32) / ROPE_AXIS_DIM)
    ang = rc[:, :, None] * inv
    cos, sin = jnp.cos(ang), jnp.sin(ang)
    cos_t = jnp.concatenate([cos, cos], axis=-1).reshape(n, HEAD_DIM)
    sin_t = jnp.concatenate([-sin, sin], axis=-1).reshape(n, HEAD_DIM)
    return cos_t, sin_t


def qkv_proj(x, w, layer, n, cos_t, sin_t, rope, first_tile=0, n_tiles=3, tm=1024):
    m, k = x.shape
    tm = min(tm, n if rope else m)
    nt = n // tm if rope else 1
    tab = pl.BlockSpec((tm, HEAD_DIM), lambda c, i: (i % nt, 0))
    return pl.pallas_call(
        functools.partial(_qkv_kernel, rope=rope),
        grid=(n_tiles, m // tm),
        in_specs=[
            pl.BlockSpec((tm, k), lambda c, i: (i, 0)),
            pl.BlockSpec((None, k, D_BRANCH), lambda c, i: (layer, 0, first_tile + c)),
            tab, tab,
        ],
        out_specs=pl.BlockSpec((tm, D_BRANCH), lambda c, i: (i, c)),
        out_shape=jax.ShapeDtypeStruct((m, n_tiles * D_BRANCH), BF16),
        scratch_shapes=[pltpu.VMEM((k, D_BRANCH), BF16)],
        compiler_params=_params(2),
        name="qkv_proj",
    )(x, w, cos_t, sin_t)


def _nt_dot(a, b):
    return lax.dot_general(a, b, (((1,), (1,)), ((), ())), preferred_element_type=F32)


def _na_pattern_rows(g, rows):
    band_row = int(np.clip(g * NA_Q_ROWS - NA_KH // 2, 0, rows - NA_BAND_ROWS))
    qr = g * NA_Q_ROWS + np.arange(NA_Q_ROWS)
    rs = np.clip(qr - NA_KH // 2, 0, rows - NA_KH)
    return band_row, qr, rs


def _na_bias_kernel(rpb_ref, o_ref, *, rows):
    shape = (GRID_W, 2 * GRID_W)
    qc = lax.broadcasted_iota(jnp.int32, shape, 0)
    lane = lax.broadcasted_iota(jnp.int32, shape, 1)
    low = lane < GRID_W
    kc = lane % GRID_W
    ws = jnp.clip(qc - NA_KW // 2, 0, GRID_W - NA_KW)
    col_ok = (kc >= ws) & (kc < ws + NA_KW)
    neg = jnp.full(shape, NEG_INF, F32)

    tiles = []
    for dr in range(2 * NA_KH - 1):
        t = jnp.broadcast_to(rpb_ref[0, 0, dr:dr + 1, :], shape)
        t = pltpu.roll(t, 2 * GRID_W - (NA_KW - 1), 1)
        for bit in range(6):
            t = jnp.where(((qc >> bit) & 1) == 1, pltpu.roll(t, 1 << bit, 1), t)
        t = jnp.where(low, t, pltpu.roll(t, GRID_W, 1))
        tiles.append(jnp.where(col_ok, t * LOG2_E, NEG_INF))

    n_groups = rows // NA_Q_ROWS
    for p, g in enumerate((0, 1, n_groups - 1)):
        band_row, qr, rs = _na_pattern_rows(g, rows)
        for qi in range(NA_Q_ROWS):
            def half(krl):
                kr = band_row + krl
                if rs[qi] <= kr < rs[qi] + NA_KH:
                    return tiles[kr - qr[qi] + NA_KH - 1]
                return neg
            for pair in range(NA_BAND_ROWS // 2):
                blk = jnp.where(low, half(2 * pair), half(2 * pair + 1))
                o_ref[0, p, 0, qi * GRID_W:(qi + 1) * GRID_W, pair * 128:(pair + 1) * 128] = blk


def na_bias(na_rpb, rows):
    nl, nh, ndr, ndc = na_rpb.shape
    band1, qr1, rs1 = _na_pattern_rows(1, rows)
    for g in range(2, rows // NA_Q_ROWS - 1):
        band, qr, rs = _na_pattern_rows(g, rows)
        assert (qr - band == qr1 - band1).all() and (rs - band == rs1 - band1).all()
    rpb = jnp.pad(na_rpb, ((0, 0), (0, 0), (0, 16 - ndr), (0, 128 - ndc)))
    return pl.pallas_call(
        functools.partial(_na_bias_kernel, rows=rows),
        grid=(nl, nh),
        in_specs=[pl.BlockSpec((1, 1, 16, 128), lambda l, h: (l, h, 0, 0))],
        out_specs=pl.BlockSpec((1, 3, 1, NA_TQ, NA_TK), lambda l, h: (l, 0, h, 0, 0)),
        out_shape=jax.ShapeDtypeStruct((nl, 3, nh, NA_TQ, NA_TK), F32),
        compiler_params=_params(2),
        name="na_bias",
    )(rpb)


def _na_kernel(q_ref, k_ref, v_ref, kc_ref, vc_ref, bias_ref, o_ref):
    g = pl.program_id(2)
    n_groups = k_ref.shape[1] // NA_TQ
    band_row = jnp.clip(g * NA_Q_ROWS - NA_KH // 2, 0, n_groups * NA_Q_ROWS - NA_BAND_ROWS)
    start = pl.multiple_of(band_row * GRID_W, NA_TQ)
    pat = jnp.where(g == 0, 0, jnp.where(g == n_groups - 1, 2, 1))
    scale = HEAD_DIM ** -0.5 * LOG2_E
    for hh in range(NA_HEADS_PER_STEP):
        cols = slice(hh * HEAD_DIM, (hh + 1) * HEAD_DIM)
        q = q_ref[0, :, cols]
        kb = k_ref[0, pl.ds(start, NA_TK), cols]
        vb = v_ref[0, pl.ds(start, NA_TK), cols]
        s_loc = _nt_dot(q, kb) * scale + bias_ref[0, pat, hh]
        s_ctx = _nt_dot(q, kc_ref[0, :, cols]) * scale
        m = jnp.maximum(jnp.max(s_loc, axis=1, keepdims=True), jnp.max(s_ctx, axis=1, keepdims=True))
        e_loc = jnp.exp2(s_loc - m)
        e_ctx = jnp.exp2(s_ctx - m)
        inv = 1.0 / (jnp.sum(e_loc, axis=1, keepdims=True) + jnp.sum(e_ctx, axis=1, keepdims=True))
        o = jnp.dot(e_loc.astype(BF16), vb, preferred_element_type=F32)
        o = o + jnp.dot(e_ctx.astype(BF16), vc_ref[0, :, cols], preferred_element_type=F32)
        o_ref[0, :, cols] = (o * inv).astype(o_ref.dtype)


def na_attention(qkv, kvc, kc_col, bias, layer, b, n):
    lc = kvc.shape[0] // b
    qkv3 = qkv.reshape(b, n, QKV_COLS)
    kvc3 = kvc.reshape(b, lc, kvc.shape[1])
    hbs = D_BRANCH // NA_HB
    kcb = kc_col // NA_HB
    out = pl.pallas_call(
        _na_kernel,
        grid=(hbs, b, n // NA_TQ),
        in_specs=[
            pl.BlockSpec((1, NA_TQ, NA_HB), lambda h, i, g: (i, g, h)),
            pl.BlockSpec((1, n, NA_HB), lambda h, i, g: (i, 0, hbs + h)),
            pl.BlockSpec((1, n, NA_HB), lambda h, i, g: (i, 0, 2 * hbs + h)),
            pl.BlockSpec((1, lc, NA_HB), lambda h, i, g: (i, 0, kcb + h)),
            pl.BlockSpec((1, lc, NA_HB), lambda h, i, g: (i, 0, kcb + hbs + h)),
            pl.BlockSpec((1, 3, NA_HEADS_PER_STEP, NA_TQ, NA_TK), lambda h, i, g: (layer, 0, h, 0, 0)),
        ],
        out_specs=pl.BlockSpec((1, NA_TQ, NA_HB), lambda h, i, g: (i, g, h)),
        out_shape=jax.ShapeDtypeStruct((b, n, D_BRANCH), BF16),
        compiler_params=_params(3),
        name="na_attention",
    )(qkv3, qkv3, qkv3, kvc3, kvc3, bias)
    return out.reshape(b * n, D_BRANCH)


def _ctx_attn_kernel(q_ref, k_ref, v_ref, o_ref):
    s = _nt_dot(q_ref[0], k_ref[0]) * (HEAD_DIM ** -0.5)
    m = jnp.max(s, axis=1, keepdims=True)
    e = jnp.exp(s - m)
    p = e * (1.0 / jnp.sum(e, axis=1, keepdims=True))
    o_ref[0] = jnp.dot(p.astype(BF16), v_ref[0], preferred_element_type=F32).astype(o_ref.dtype)


def ctx_attention(qkv, b, lc):
    qkv3 = qkv.reshape(b, lc, QKV_COLS)
    out = pl.pallas_call(
        _ctx_attn_kernel,
        grid=(b, N_HEADS),
        in_specs=[
            pl.BlockSpec((1, lc, HEAD_DIM), lambda i, h: (i, 0, h)),
            pl.BlockSpec((1, lc, HEAD_DIM), lambda i, h: (i, 0, N_HEADS + h)),
            pl.BlockSpec((1, lc, HEAD_DIM), lambda i, h: (i, 0, 2 * N_HEADS + h)),
        ],
        out_specs=pl.BlockSpec((1, lc, HEAD_DIM), lambda i, h: (i, 0, h)),
        out_shape=jax.ShapeDtypeStruct((b, lc, D_BRANCH), BF16),
        compiler_params=_params(2),
        name="ctx_attention",
    )(qkv3, qkv3, qkv3)
    return out.reshape(b * lc, D_BRANCH)


def _conv_kernel(xc_ref, bg_ref, cg_ref, w_ref, o_ref):
    z = cg_ref[0] * xc_ref[0]
    n = z.shape[0]
    row = lax.broadcasted_iota(jnp.int32, z.shape, 0)
    z_prev = jnp.where(row == 0, 0.0, pltpu.roll(z, 1, 0))
    z_next = jnp.where(row == n - 1, 0.0, pltpu.roll(z, n - 1, 0))
    y = z_prev * w_ref[0, 0:1, :] + z * w_ref[0, 1:2, :] + z_next * w_ref[0, 2:3, :]
    o_ref[0] = (bg_ref[0] * y).astype(o_ref.dtype)


def short_conv(p, conv_w, layer, b, n, tc=256):
    p3 = p.reshape(b, n, p.shape[1])
    blk = lambda col: pl.BlockSpec((1, n, tc), lambda i, j: (i, 0, col // tc + j))
    out = pl.pallas_call(
        _conv_kernel,
        grid=(b, D_BRANCH // tc),
        in_specs=[blk(COL_XC), blk(COL_BG), blk(COL_CG),
                  pl.BlockSpec((1, 3, tc), lambda i, j: (layer, 0, j))],
        out_specs=pl.BlockSpec((1, n, tc), lambda i, j: (i, 0, j)),
        out_shape=jax.ShapeDtypeStruct((b, n, D_BRANCH), BF16),
        compiler_params=_params(2),
        name="short_conv",
    )(p3, p3, p3, conv_w)
    return out.reshape(b * n, D_BRANCH)


def _gmlp_kernel(u_ref, v_ref, g_ref, ws_ref, bs_ref, o_ref):
    v = jax.nn.gelu(v_ref[...])
    mu = jnp.mean(v, axis=-1, keepdims=True)
    var = jnp.mean(jnp.square(v - mu), axis=-1, keepdims=True)
    vn = ((v - mu) * lax.rsqrt(var + NORM_EPS) * g_ref[...]).astype(BF16)
    for grp in range(GMLP_GROUPS):
        cols = slice(grp * 128, (grp + 1) * 128)
        s = jnp.dot(ws_ref[0, grp].astype(BF16), vn[:, cols], preferred_element_type=F32)
        s = s + bs_ref[:, grp:grp + 1]
        o_ref[:, cols] = (jax.nn.gelu(u_ref[:, cols]) * s).astype(o_ref.dtype)


def chunk_gmlp(p, ln_g, w_s, b_s, layer):
    m = p.shape[0]
    return pl.pallas_call(
        _gmlp_kernel,
        grid=(m // CHUNK,),
        in_specs=[
            pl.BlockSpec((CHUNK, D_BRANCH), lambda i: (i, COL_U // D_BRANCH)),
            pl.BlockSpec((CHUNK, D_BRANCH), lambda i: (i, COL_VS // D_BRANCH)),
            pl.BlockSpec((1, D_BRANCH), lambda i: (0, 0)),
            pl.BlockSpec((1, GMLP_GROUPS, CHUNK, CHUNK), lambda i: (layer, 0, 0, 0)),
            pl.BlockSpec((CHUNK, GMLP_GROUPS), lambda i: (0, 0)),
        ],
        out_specs=pl.BlockSpec((CHUNK, D_BRANCH), lambda i: (i, 0)),
        out_shape=jax.ShapeDtypeStruct((m, D_BRANCH), BF16),
        compiler_params=_params(1),
        name="chunk_gmlp",
    )(p, p, ln_g[layer].reshape(1, D_BRANCH), w_s, b_s[layer].T)


def _merge_kernel(a_ref, c_ref, m_ref, g0_ref, g1_ref, g2_ref, w_ref, o_ref, wbf_ref):
    @pl.when(pl.program_id(1) == 0)
    def _():
        wbf_ref[...] = w_ref[...].astype(BF16)

    acc = None
    for i, (br, gl) in enumerate(((a_ref, g0_ref), (c_ref, g1_ref), (m_ref, g2_ref))):
        proj = jnp.dot(br[...], wbf_ref[i], preferred_element_type=F32)
        term = jax.nn.sigmoid(gl[...]) * proj
        acc = term if acc is None else acc + term
    o_ref[...] = acc.astype(o_ref.dtype)


def merge_branches(attn, conv, gm, p, w_branch, layer, tm=512, tn=1024):
    m = attn.shape[0]
    br = pl.BlockSpec((tm, D_BRANCH), lambda j, i: (i, 0))
    gate = lambda k: pl.BlockSpec((tm, tn), lambda j, i: (i, (COL_GATE + k * D_MODEL) // tn + j))
    return pl.pallas_call(
        _merge_kernel,
        grid=(D_MODEL // tn, m // tm),
        in_specs=[br, br, br, gate(0), gate(1), gate(2),
                  pl.BlockSpec((None, N_BRANCHES, D_BRANCH, tn), lambda j, i: (layer, 0, 0, j))],
        out_specs=pl.BlockSpec((tm, tn), lambda j, i: (i, j)),
        out_shape=jax.ShapeDtypeStruct((m, D_MODEL), BF16),
        scratch_shapes=[pltpu.VMEM((N_BRANCHES, D_BRANCH, tn), BF16)],
        compiler_params=_params(2),
        name="merge_branches",
    )(attn, conv, gm, p, p, p, w_branch)


def _out_proj_kernel(h_ref, w_ref, x_ref, mod_ref, o_ref, wbf_ref):
    @pl.when(pl.program_id(1) == 0)
    def _():
        wbf_ref[...] = w_ref[...].astype(BF16)

    y = jnp.dot(h_ref[...], wbf_ref[...], preferred_element_type=F32)
    o_ref[...] = x_ref[...] + mod_ref[0] * y


def out_proj_residual(h, w_out, layer, x, gate_vec, n, tm=1024, tn=1024):
    m, d = x.shape
    tm = min(tm, n)
    per_b = n // tm
    return pl.pallas_call(
        _out_proj_kernel,
        grid=(d // tn, m // tm),
        in_specs=[
            pl.BlockSpec((tm, d), lambda j, i: (i, 0)),
            pl.BlockSpec((None, d, tn), lambda j, i: (layer, 0, j)),
            pl.BlockSpec((tm, tn), lambda j, i: (i, j)),
            pl.BlockSpec((1, 1, tn), lambda j, i: (i // per_b, 0, j)),
        ],
        out_specs=pl.BlockSpec((tm, tn), lambda j, i: (i, j)),
        out_shape=jax.ShapeDtypeStruct((m, d), F32),
        scratch_shapes=[pltpu.VMEM((d, tn), BF16)],
        compiler_params=_params(2),
        name="out_proj_residual",
    )(h, w_out, x, gate_vec.reshape(-1, 1, d))


def _router_kernel(x_ref, g_ref, sh_ref, sc_ref, wr_ref, br_ref, h_ref, aff_ref):
    y = _rms(x_ref[0], g_ref[...])
    h = (y * (1 + sc_ref[0]) + sh_ref[0]).astype(BF16)
    h_ref[0] = h
    logits = _nt_dot(wr_ref[...].astype(BF16), h) + br_ref[...]
    mx = jnp.max(logits, axis=0, keepdims=True)
    e = jnp.exp(logits - mx)
    aff_ref[0] = e / jnp.sum(e, axis=0, keepdims=True)


def router(x, g, shift, scale, w_router, b_router, tm=256):
    b, n, d = x.shape
    tm = min(tm, n)
    return pl.pallas_call(
        _router_kernel,
        grid=(b, n // tm),
        in_specs=[
            pl.BlockSpec((1, tm, d), lambda i, j: (i, j, 0)),
            pl.BlockSpec((1, d), lambda i, j: (0, 0)),
            pl.BlockSpec((1, 1, d), lambda i, j: (i, 0, 0)),
            pl.BlockSpec((1, 1, d), lambda i, j: (i, 0, 0)),
            pl.BlockSpec((N_EXPERTS, d), lambda i, j: (0, 0)),
            pl.BlockSpec((N_EXPERTS, 1), lambda i, j: (0, 0)),
        ],
        out_specs=[
            pl.BlockSpec((1, tm, d), lambda i, j: (i, j, 0)),
            pl.BlockSpec((1, N_EXPERTS, tm), lambda i, j: (i, 0, j)),
        ],
        out_shape=[jax.ShapeDtypeStruct((b, n, d), BF16),
                   jax.ShapeDtypeStruct((b, N_EXPERTS, n), F32)],
        compiler_params=_params(2),
        name="router",
    )(x, g.reshape(1, d), shift.reshape(b, 1, d), scale.reshape(b, 1, d),
      w_router.T, b_router.reshape(N_EXPERTS, 1))


def _select_kernel(aff_ref, tri_ref, slot_ref, *, cap):
    bits = pltpu.bitcast(aff_ref[0], jnp.int32)

    def step(i, t):
        cand = t | lax.shift_left(jnp.int32(1), 30 - i)
        cnt = jnp.sum(jnp.where(bits >= cand, 1.0, 0.0), axis=1, keepdims=True)
        return jnp.where(cnt >= cap, cand, t)

    t = lax.fori_loop(0, 31, step, jnp.zeros((bits.shape[0], 1), jnp.int32))
    gt = bits > t
    eq = bits == t
    need = cap - jnp.sum(jnp.where(gt, 1.0, 0.0), axis=1, keepdims=True)
    tri = tri_ref[...]
    eq_rank = jnp.dot(jnp.where(eq, 1.0, 0.0).astype(BF16), tri, preferred_element_type=F32)
    sel = gt | (eq & (eq_rank < need))
    slot = jnp.dot(jnp.where(sel, 1.0, 0.0).astype(BF16), tri, preferred_element_type=F32)
    slot_ref[0] = jnp.where(sel, slot.astype(jnp.int32), -1)


def select_tokens(aff, cap):
    b, e, n = aff.shape
    tri = jnp.triu(jnp.ones((n, n), BF16), k=1)
    return pl.pallas_call(
        functools.partial(_select_kernel, cap=cap),
        grid=(b,),
        in_specs=[
            pl.BlockSpec((1, e, n), lambda i: (i, 0, 0)),
            pl.BlockSpec((n, n), lambda i: (0, 0)),
        ],
        out_specs=pl.BlockSpec((1, e, n), lambda i: (i, 0, 0)),
        out_shape=jax.ShapeDtypeStruct((b, e, n), jnp.int32),
        compiler_params=_params(1),
        name="select_tokens",
    )(aff, tri)


def _onehot(slot, cap):
    hit = lax.broadcasted_iota(jnp.int32, (cap, slot.shape[1]), 0) == slot
    return hit, jnp.where(hit, 1.0, 0.0)


def _gather_kernel(h_ref, slot_ref, aff_ref, xs_ref, gate_ref, *, cap):
    hit, onehot = _onehot(slot_ref[0], cap)
    xs_ref[0] = jnp.dot(onehot.astype(BF16), h_ref[0], preferred_element_type=F32).astype(BF16)
    gate_ref[0] = jnp.sum(jnp.where(hit, aff_ref[0], 0.0), axis=1, keepdims=True)


def gather_tokens(h, slot, aff, cap):
    b, n, d = h.shape
    e = N_EXPERTS
    row = pl.BlockSpec((1, 1, n), lambda i, j: (i * e + j, 0, 0))
    return pl.pallas_call(
        functools.partial(_gather_kernel, cap=cap),
        grid=(b, e),
        in_specs=[pl.BlockSpec((1, n, d), lambda i, j: (i, 0, 0)), row, row],
        out_specs=[
            pl.BlockSpec((1, cap, d), lambda i, j: (j, i, 0)),
            pl.BlockSpec((1, cap, 1), lambda i, j: (j, i, 0)),
        ],
        out_shape=[jax.ShapeDtypeStruct((e, b * cap, d), BF16),
                   jax.ShapeDtypeStruct((e, b * cap, 1), F32)],
        compiler_params=_params(2),
        name="gather_tokens",
    )(h, slot.reshape(b * e, 1, n), aff.reshape(b * e, 1, n))


def _ffn_kernel(*refs, n_sets, n_f):
    xs_refs, gate_refs = refs[0:2 * n_sets:2], refs[1:2 * n_sets:2]
    wg_ref, wu_ref, wd_ref = refs[2 * n_sets:2 * n_sets + 3]
    y_refs = refs[2 * n_sets + 3:3 * n_sets + 3]
    act_refs = refs[3 * n_sets + 3:]
    s = pl.program_id(1)
    tf = wg_ref.shape[1]

    @pl.when(s < n_f)
    def _():
        wg = wg_ref[...].astype(BF16)
        wu = wu_ref[...].astype(BF16)
        for xs_ref, act_ref in zip(xs_refs, act_refs):
            xs = xs_ref[0]
            hg = jnp.dot(xs, wg, preferred_element_type=F32)
            hu = jnp.dot(xs, wu, preferred_element_type=F32)
            act_ref[s] = (jax.nn.silu(hg) * hu).astype(BF16)

    @pl.when(s >= n_f)
    def _():
        wd = wd_ref[...].astype(BF16)
        for gate_ref, y_ref, act_ref in zip(gate_refs, y_refs, act_refs):
            y = None
            for f in range(n_f):
                part = jnp.dot(act_ref[f], wd[f * tf:(f + 1) * tf, :], preferred_element_type=F32)
                y = part if y is None else y + part
            y_ref[0] = (y * gate_ref[0]).astype(y_ref.dtype)


def expert_ffn(sets, w_gate, w_up, w_down, layer, tf=512, tn=512):
    d, ff = w_gate.shape[2], w_gate.shape[3]
    n_f, n_d = ff // tf, d // tn
    f_idx = lambda s: jnp.minimum(s, n_f - 1)
    d_idx = lambda s: jnp.maximum(s - n_f, 0)
    in_specs, operands, out_specs, out_shape, scratch = [], [], [], [], []
    for xs, gates in sets:
        r = xs.shape[1]
        in_specs += [pl.BlockSpec((1, r, d), lambda i, s: (i, 0, 0)),
                     pl.BlockSpec((1, r, 1), lambda i, s: (i, 0, 0))]
        operands += [xs, gates]
        out_specs.append(pl.BlockSpec((1, r, tn), lambda i, s: (i, 0, d_idx(s))))
        out_shape.append(jax.ShapeDtypeStruct(xs.shape, BF16))
        scratch.append(pltpu.VMEM((n_f, r, tf), BF16))
    in_specs += [
        pl.BlockSpec((None, None, d, tf), lambda i, s: (layer, i, 0, f_idx(s))),
        pl.BlockSpec((None, None, d, tf), lambda i, s: (layer, i, 0, f_idx(s))),
        pl.BlockSpec((None, None, ff, tn), lambda i, s: (layer, i, 0, d_idx(s))),
    ]
    return pl.pallas_call(
        functools.partial(_ffn_kernel, n_sets=len(sets), n_f=n_f),
        grid=(N_EXPERTS, n_f + n_d),
        in_specs=in_specs,
        out_specs=out_specs,
        out_shape=out_shape,
        scratch_shapes=scratch,
        compiler_params=_params(2),
        name="expert_ffn",
    )(*operands, w_gate, w_up, w_down)


def _combine_kernel(y_ref, slot_ref, x_ref, mod_ref, o_ref, onehot_ref, *, cap):
    @pl.when(pl.program_id(1) == 0)
    def _():
        for e in range(N_EXPERTS):
            onehot_ref[e * cap:(e + 1) * cap, :] = _onehot(slot_ref[e], cap)[1].astype(BF16)

    y = y_ref[...].reshape(N_EXPERTS * cap, y_ref.shape[2])
    moe = lax.dot_general(onehot_ref[...], y, (((0,), (0,)), ((), ())), preferred_element_type=F32)
    o_ref[0] = x_ref[0] + mod_ref[0] * moe


def combine(y, slot, x, gate_vec, cap, tn=512):
    b, n, d = x.shape
    e = N_EXPERTS
    return pl.pallas_call(
        functools.partial(_combine_kernel, cap=cap),
        grid=(b, d // tn),
        in_specs=[
            pl.BlockSpec((e, cap, tn), lambda i, j: (0, i, j)),
            pl.BlockSpec((e, 1, n), lambda i, j: (i, 0, 0)),
            pl.BlockSpec((1, n, tn), lambda i, j: (i, 0, j)),
            pl.BlockSpec((1, 1, tn), lambda i, j: (i, 0, j)),
        ],
        out_specs=pl.BlockSpec((1, n, tn), lambda i, j: (i, 0, j)),
        out_shape=jax.ShapeDtypeStruct((b, n, d), F32),
        scratch_shapes=[pltpu.VMEM((e * cap, n), BF16)],
        compiler_params=_params(2),
        name="combine",
    )(y, slot.reshape(b * e, 1, n), x, gate_vec.reshape(b, 1, d))


def moe_route(x, g, mod, w_router, b_router):
    n = x.shape[1]
    cap = CAPACITY_FACTOR * n // N_EXPERTS
    h, aff = router(x, g, mod[:, 3], mod[:, 4], w_router, b_router)
    slot = select_tokens(aff, cap)
    xs, gates = gather_tokens(h, slot, aff, cap)
    return slot, cap, (xs, gates)


def mixer_block(x, p, attn, mod, n, layer, conv_w, ln_g, w_s, b_s, w_branch, w_out):
    b = x.shape[0] // n
    conv = short_conv(p, conv_w, layer, b, n)
    gm = chunk_gmlp(p, ln_g, w_s, b_s, layer)
    merged = merge_branches(attn, conv, gm, p, w_branch, layer)
    return out_proj_residual(merged, w_out, layer, x, mod[:, 2], n)


def kernel(x, c, ctx, c_ctx, w_mod, b_mod, norm1_g, w_in, na_rpb, conv_w, gmlp_ln_g, w_spatial, b_spatial,
           w_branch, w_out, norm2_g, w_router, b_router, w_e_gate, w_e_up, w_e_down, final_g):
    b, n, d = x.shape
    lc = ctx.shape[1]
    rest_cols = w_in.shape[2] - QKV_COLS
    cvec = jnp.concatenate([c, c_ctx[None], jnp.zeros((8 - b - 1, d), F32)], axis=0)
    modv = modvec(cvec, w_mod, b_mod).reshape(DEPTH, 8, 6, d)
    cos_t, sin_t = rope_tables(n)
    bias = na_bias(na_rpb, n // GRID_W)
    mix_w = (conv_w, gmlp_ln_g, w_spatial, b_spatial, w_branch, w_out)
    ctx_s = ctx
    for layer in range(DEPTH):
        last = layer == DEPTH - 1
        mod = modv[layer, :b]
        mod_c = jnp.broadcast_to(modv[layer, b][None], (b, 6, d))
        route_w = (w_router[layer], b_router[layer])

        hc = norm_mod(ctx_s, norm1_g[layer], mod_c[:, 0], mod_c[:, 1]).reshape(b * lc, d)
        if last:
            kvc = qkv_proj(hc, w_in, layer, lc, cos_t, sin_t, rope=False, first_tile=1, n_tiles=2)
            kc_col = 0
        else:
            kvc = qkv_proj(hc, w_in, layer, lc, cos_t, sin_t, rope=False)
            kc_col = D_BRANCH
            pc = matmul(hc, w_in, layer, QKV_COLS, rest_cols)
            attn_c = ctx_attention(kvc, b, lc)
            ctx_mid = mixer_block(ctx_s.reshape(b * lc, d), pc, attn_c, mod_c, lc, layer, *mix_w)
            ctx_mid = ctx_mid.reshape(b, lc, d)

        h = norm_mod(x, norm1_g[layer], mod[:, 0], mod[:, 1]).reshape(b * n, d)
        qkv = qkv_proj(h, w_in, layer, n, cos_t, sin_t, rope=True)
        p = matmul(h, w_in, layer, QKV_COLS, rest_cols)
        attn = na_attention(qkv, kvc, kc_col, bias, layer, b, n)
        x = mixer_block(x.reshape(b * n, d), p, attn, mod, n, layer, *mix_w).reshape(b, n, d)

        slot, cap, rows = moe_route(x, norm2_g[layer], mod, *route_w)
        if last:
            (y,) = expert_ffn([rows], w_e_gate, w_e_up, w_e_down, layer)
        else:
            slot_c, cap_c, rows_c = moe_route(ctx_mid, norm2_g[layer], mod_c, *route_w)
            y, y_c = expert_ffn([rows, rows_c], w_e_gate, w_e_up, w_e_down, layer)
            ctx_s = combine(y_c, slot_c, ctx_mid, mod_c[:, 5], cap_c)
        x = combine(y, slot, x, mod[:, 5], cap)
    return final_norm(x, final_g)
```

```python
import functools

import numpy as np
import jax
import jax.numpy as jnp
from jax import lax
from jax.experimental import pallas as pl
from jax.experimental.pallas import tpu as pltpu

D_MODEL = 2048
DEPTH = 2
GRID_W = 64
D_BRANCH = D_MODEL // 2
N_BRANCHES = 3
HEAD_DIM = 128
N_HEADS = D_BRANCH // HEAD_DIM
NA_KH = 8
NA_KW = 16
ROPE_THETA = 10000.0
ROPE_AXIS_DIM = HEAD_DIM // 2
CHUNK = 128
GMLP_GROUPS = D_BRANCH // 128
N_EXPERTS = 16
CAPACITY_FACTOR = 2
NORM_EPS = 1e-6
NEG_INF = -1e30
LOG2_E = float(np.log2(np.e))

QKV_COLS = 3 * D_BRANCH
COL_XC, COL_BG, COL_CG = (QKV_COLS + i * D_BRANCH for i in range(3))
P_COL0 = QKV_COLS + 3 * D_BRANCH
COL_U, COL_VS, COL_GATE = 0, D_BRANCH, 2 * D_BRANCH

NA_Q_ROWS = 4
NA_BAND_ROWS = 12
NA_TQ = NA_Q_ROWS * GRID_W
NA_TK = NA_BAND_ROWS * GRID_W
NA_HEADS_PER_STEP = 4
NA_HB = NA_HEADS_PER_STEP * HEAD_DIM

V7X_VMEM_LIMIT = 56 * 1024 * 1024

BF16 = jnp.bfloat16
F32 = jnp.float32


def _params(n_axes, vmem=V7X_VMEM_LIMIT):
    return pltpu.CompilerParams(dimension_semantics=("arbitrary",) * n_axes, vmem_limit_bytes=vmem)


def _modvec_kernel(c_ref, w_ref, b_ref, o_ref):
    s = jax.nn.silu(c_ref[...]).astype(BF16)
    o_ref[0] = jnp.dot(s, w_ref[0].astype(BF16), preferred_element_type=F32) + b_ref[0]


def modvec(cvec, w_mod, b_mod, tn=1024):
    nl, d, n6 = w_mod.shape
    return pl.pallas_call(
        _modvec_kernel,
        grid=(nl, n6 // tn),
        in_specs=[
            pl.BlockSpec((8, d), lambda l, j: (0, 0)),
            pl.BlockSpec((1, d, tn), lambda l, j: (l, 0, j)),
            pl.BlockSpec((1, 1, tn), lambda l, j: (l, 0, j)),
        ],
        out_specs=pl.BlockSpec((1, 8, tn), lambda l, j: (l, 0, j)),
        out_shape=jax.ShapeDtypeStruct((nl, 8, n6), F32),
        compiler_params=_params(2),
        name="modvec",
    )(cvec, w_mod, b_mod.reshape(nl, 1, n6))


def _rms(x, g):
    ms = jnp.mean(x * x, axis=-1, keepdims=True)
    return x * lax.rsqrt(ms + NORM_EPS) * g


def _norm_mod_kernel(x_ref, g_ref, sh_ref, sc_ref, o_ref):
    y = _rms(x_ref[0], g_ref[...])
    o_ref[0] = (y * (1 + sc_ref[0]) + sh_ref[0]).astype(o_ref.dtype)


def _norm_kernel(x_ref, g_ref, o_ref):
    o_ref[0] = _rms(x_ref[0], g_ref[...]).astype(o_ref.dtype)


def norm_mod(x, g, shift, scale, out_dtype=BF16, tm=256):
    b, n, d = x.shape
    tm = min(tm, n)
    return pl.pallas_call(
        _norm_mod_kernel,
        grid=(b, n // tm),
        in_specs=[
            pl.BlockSpec((1, tm, d), lambda i, j: (i, j, 0)),
            pl.BlockSpec((1, d), lambda i, j: (0, 0)),
            pl.BlockSpec((1, 1, d), lambda i, j: (i, 0, 0)),
            pl.BlockSpec((1, 1, d), lambda i, j: (i, 0, 0)),
        ],
        out_specs=pl.BlockSpec((1, tm, d), lambda i, j: (i, j, 0)),
        out_shape=jax.ShapeDtypeStruct((b, n, d), out_dtype),
        compiler_params=_params(2),
        name="norm_mod",
    )(x, g.reshape(1, d), shift.reshape(b, 1, d), scale.reshape(b, 1, d))


def final_norm(x, g, tm=256):
    b, n, d = x.shape
    return pl.pallas_call(
        _norm_kernel,
        grid=(b, n // tm),
        in_specs=[
            pl.BlockSpec((1, tm, d), lambda i, j: (i, j, 0)),
            pl.BlockSpec((1, d), lambda i, j: (0, 0)),
        ],
        out_specs=pl.BlockSpec((1, tm, d), lambda i, j: (i, j, 0)),
        out_shape=jax.ShapeDtypeStruct((b, n, d), F32),
        compiler_params=_params(2),
        name="final_norm",
    )(x, g.reshape(1, d))


def _mm_kernel(x_ref, w_ref, o_ref, wbf_ref):
    @pl.when(pl.program_id(1) == 0)
    def _():
        wbf_ref[...] = w_ref[...].astype(BF16)

    o_ref[...] = jnp.dot(x_ref[...], wbf_ref[...], preferred_element_type=F32).astype(o_ref.dtype)


def matmul(x, w, layer, col_off, n_cols, out_dtype=F32, tm=1024, tn=1024):
    m, k = x.shape
    tm = min(tm, m)
    off = col_off // tn
    return pl.pallas_call(
        _mm_kernel,
        grid=(n_cols // tn, m // tm),
        in_specs=[
            pl.BlockSpec((tm, k), lambda j, i: (i, 0)),
            pl.BlockSpec((None, k, tn), lambda j, i: (layer, 0, off + j)),
        ],
        out_specs=pl.BlockSpec((tm, tn), lambda j, i: (i, j)),
        out_shape=jax.ShapeDtypeStruct((m, n_cols), out_dtype),
        scratch_shapes=[pltpu.VMEM((k, tn), BF16)],
        compiler_params=_params(2),
        name="matmul",
    )(x, w)


def _qkv_kernel(x_ref, w_ref, cos_ref, sin_ref, o_ref, wbf_ref, *, rope):
    @pl.when(pl.program_id(1) == 0)
    def _():
        wbf_ref[...] = w_ref[...].astype(BF16)

    def plain():
        o_ref[...] = jnp.dot(x_ref[...], wbf_ref[...], preferred_element_type=F32).astype(BF16)

    if not rope:
        plain()
        return
    c = pl.program_id(0)

    @pl.when(c < 2)
    def _():
        cs = cos_ref[...]
        sn = sin_ref[...]
        lane = lax.broadcasted_iota(jnp.int32, cs.shape, 1)
        first_half = (lane % ROPE_AXIS_DIM) < (ROPE_AXIS_DIM // 2)
        for pair in range(N_HEADS // 2):
            cols = slice(pair * 2 * HEAD_DIM, (pair + 1) * 2 * HEAD_DIM)
            acc = jnp.dot(x_ref[...], wbf_ref[:, cols], preferred_element_type=F32)
            for h in range(2):
                t = acc[:, h * HEAD_DIM:(h + 1) * HEAD_DIM]
                partner = jnp.where(first_half,
                                    pltpu.roll(t, HEAD_DIM - ROPE_AXIS_DIM // 2, 1),
                                    pltpu.roll(t, ROPE_AXIS_DIM // 2, 1))
                lo = (pair * 2 + h) * HEAD_DIM
                o_ref[:, lo:lo + HEAD_DIM] = (t * cs + partner * sn).astype(BF16)

    @pl.when(c == 2)
    def _():
        plain()


def rope_tables(n):
    pos = jnp.arange(n)
    rc = jnp.stack([pos // GRID_W, pos % GRID_W], axis=-1).astype(F32)
    inv = ROPE_THETA ** (-jnp.arange(0, ROPE_AXIS_DIM, 2, dtype=F32) / ROPE_AXIS_DIM)
    ang = rc[:, :, None] * inv
    cos, sin = jnp.cos(ang), jnp.sin(ang)
    cos_t = jnp.concatenate([cos, cos], axis=-1).reshape(n, HEAD_DIM)
    sin_t = jnp.concatenate([-sin, sin], axis=-1).reshape(n, HEAD_DIM)
    return cos_t, sin_t


def qkv_proj(x, w, layer, n, cos_t, sin_t, rope, first_tile=0, n_tiles=3, tm=1024):
    m, k = x.shape
    tm = min(tm, n if rope else m)
    nt = n // tm if rope else 1
    tab = pl.BlockSpec((tm, HEAD_DIM), lambda c, i: (i % nt, 0))
    return pl.pallas_call(
        functools.partial(_qkv_kernel, rope=rope),
        grid=(n_tiles, m // tm),
        in_specs=[
            pl.BlockSpec((tm, k), lambda c, i: (i, 0)),
            pl.BlockSpec((None, k, D_BRANCH), lambda c, i: (layer, 0, first_tile + c)),
            tab, tab,
        ],
        out_specs=pl.BlockSpec((tm, D_BRANCH), lambda c, i: (i, c)),
        out_shape=jax.ShapeDtypeStruct((m, n_tiles * D_BRANCH), BF16),
        scratch_shapes=[pltpu.VMEM((k, D_BRANCH), BF16)],
        compiler_params=_params(2),
        name="qkv_proj",
    )(x, w, cos_t, sin_t)


def _nt_dot(a, b):
    return lax.dot_general(a, b, (((1,), (1,)), ((), ())), preferred_element_type=F32)


def _na_pattern_rows(g, rows):
    band_row = int(np.clip(g * NA_Q_ROWS - NA_KH // 2, 0, rows - NA_BAND_ROWS))
    qr = g * NA_Q_ROWS + np.arange(NA_Q_ROWS)
    rs = np.clip(qr - NA_KH // 2, 0, rows - NA_KH)
    return band_row, qr, rs


def _na_bias_kernel(rpb_ref, o_ref, *, rows):
    shape = (GRID_W, 2 * GRID_W)
    qc = lax.broadcasted_iota(jnp.int32, shape, 0)
    lane = lax.broadcasted_iota(jnp.int32, shape, 1)
    low = lane < GRID_W
    kc = lane % GRID_W
    ws = jnp.clip(qc - NA_KW // 2, 0, GRID_W - NA_KW)
    col_ok = (kc >= ws) & (kc < ws + NA_KW)
    neg = jnp.full(shape, NEG_INF, F32)

    tiles = []
    for dr in range(2 * NA_KH - 1):
        t = jnp.broadcast_to(rpb_ref[0, 0, dr:dr + 1, :], shape)
        t = pltpu.roll(t, 2 * GRID_W - (NA_KW - 1), 1)
        for bit in range(6):
            t = jnp.where(((qc >> bit) & 1) == 1, pltpu.roll(t, 1 << bit, 1), t)
        t = jnp.where(low, t, pltpu.roll(t, GRID_W, 1))
        tiles.append(jnp.where(col_ok, t * LOG2_E, NEG_INF))

    n_groups = rows // NA_Q_ROWS
    for p, g in enumerate((0, 1, n_groups - 1)):
        band_row, qr, rs = _na_pattern_rows(g, rows)
        for qi in range(NA_Q_ROWS):
            def half(krl):
                kr = band_row + krl
                if rs[qi] <= kr < rs[qi] + NA_KH:
                    return tiles[kr - qr[qi] + NA_KH - 1]
                return neg
            for pair in range(NA_BAND_ROWS // 2):
                blk = jnp.where(low, half(2 * pair), half(2 * pair + 1))
                o_ref[0, p, 0, qi * GRID_W:(qi + 1) * GRID_W, pair * 128:(pair + 1) * 128] = blk


def na_bias(na_rpb, rows):
    nl, nh, ndr, ndc = na_rpb.shape
    band1, qr1, rs1 = _na_pattern_rows(1, rows)
    for g in range(2, rows // NA_Q_ROWS - 1):
        band, qr, rs = _na_pattern_rows(g, rows)
        assert (qr - band == qr1 - band1).all() and (rs - band == rs1 - band1).all()
    rpb = jnp.pad(na_rpb, ((0, 0), (0, 0), (0, 16 - ndr), (0, 128 - ndc)))
    return pl.pallas_call(
        functools.partial(_na_bias_kernel, rows=rows),
        grid=(nl, nh),
        in_specs=[pl.BlockSpec((1, 1, 16, 128), lambda l, h: (l, h, 0, 0))],
        out_specs=pl.BlockSpec((1, 3, 1, NA_TQ, NA_TK), lambda l, h: (l, 0, h, 0, 0)),
        out_shape=jax.ShapeDtypeStruct((nl, 3, nh, NA_TQ, NA_TK), F32),
        compiler_params=_params(2),
        name="na_bias",
    )(rpb)


def _na_kernel(q_ref, k_ref, v_ref, kc_ref, vc_ref, bias_ref, o_ref):
    g = pl.program_id(2)
    n_groups = k_ref.shape[1] // NA_TQ
    band_row = jnp.clip(g * NA_Q_ROWS - NA_KH // 2, 0, n_groups * NA_Q_ROWS - NA_BAND_ROWS)
    start = pl.multiple_of(band_row * GRID_W, NA_TQ)
    pat = jnp.where(g == 0, 0, jnp.where(g == n_groups - 1, 2, 1))
    scale = HEAD_DIM ** -0.5 * LOG2_E
    for hh in range(NA_HEADS_PER_STEP):
        cols = slice(hh * HEAD_DIM, (hh + 1) * HEAD_DIM)
        q = q_ref[0, :, cols]
        kb = k_ref[0, pl.ds(start, NA_TK), cols]
        vb = v_ref[0, pl.ds(start, NA_TK), cols]
        s_loc = _nt_dot(q, kb) * scale + bias_ref[0, pat, hh]
        s_ctx = _nt_dot(q, kc_ref[0, :, cols]) * scale
        m = jnp.maximum(jnp.max(s_loc, axis=1, keepdims=True), jnp.max(s_ctx, axis=1, keepdims=True))
        e_loc = jnp.exp2(s_loc - m)
        e_ctx = jnp.exp2(s_ctx - m)
        inv = 1.0 / (jnp.sum(e_loc, axis=1, keepdims=True) + jnp.sum(e_ctx, axis=1, keepdims=True))
        o = jnp.dot(e_loc.astype(BF16), vb, preferred_element_type=F32)
        o = o + jnp.dot(e_ctx.astype(BF16), vc_ref[0, :, cols], preferred_element_type=F32)
        o_ref[0, :, cols] = (o * inv).astype(o_ref.dtype)


def na_attention(qkv, kvc, kc_col, bias, layer, b, n):
    lc = kvc.shape[0] // b
    qkv3 = qkv.reshape(b, n, QKV_COLS)
    kvc3 = kvc.reshape(b, lc, kvc.shape[1])
    hbs = D_BRANCH // NA_HB
    kcb = kc_col // NA_HB
    out = pl.pallas_call(
        _na_kernel,
        grid=(hbs, b, n // NA_TQ),
        in_specs=[
            pl.BlockSpec((1, NA_TQ, NA_HB), lambda h, i, g: (i, g, h)),
            pl.BlockSpec((1, n, NA_HB), lambda h, i, g: (i, 0, hbs + h)),
            pl.BlockSpec((1, n, NA_HB), lambda h, i, g: (i, 0, 2 * hbs + h)),
            pl.BlockSpec((1, lc, NA_HB), lambda h, i, g: (i, 0, kcb + h)),
            pl.BlockSpec((1, lc, NA_HB), lambda h, i, g: (i, 0, kcb + hbs + h)),
            pl.BlockSpec((1, 3, NA_HEADS_PER_STEP, NA_TQ, NA_TK), lambda h, i, g: (layer, 0, h, 0, 0)),
        ],
        out_specs=pl.BlockSpec((1, NA_TQ, NA_HB), lambda h, i, g: (i, g, h)),
        out_shape=jax.ShapeDtypeStruct((b, n, D_BRANCH), BF16),
        compiler_params=_params(3),
        name="na_attention",
    )(qkv3, qkv3, qkv3, kvc3, kvc3, bias)
    return out.reshape(b * n, D_BRANCH)


def _ctx_attn_kernel(q_ref, k_ref, v_ref, o_ref):
    s = _nt_dot(q_ref[0], k_ref[0]) * (HEAD_DIM ** -0.5)
    m = jnp.max(s, axis=1, keepdims=True)
    e = jnp.exp(s - m)
    p = e * (1.0 / jnp.sum(e, axis=1, keepdims=True))
    o_ref[0] = jnp.dot(p.astype(BF16), v_ref[0], preferred_element_type=F32).astype(o_ref.dtype)


def ctx_attention(qkv, b, lc):
    qkv3 = qkv.reshape(b, lc, QKV_COLS)
    out = pl.pallas_call(
        _ctx_attn_kernel,
        grid=(b, N_HEADS),
        in_specs=[
            pl.BlockSpec((1, lc, HEAD_DIM), lambda i, h: (i, 0, h)),
            pl.BlockSpec((1, lc, HEAD_DIM), lambda i, h: (i, 0, N_HEADS + h)),
            pl.BlockSpec((1, lc, HEAD_DIM), lambda i, h: (i, 0, 2 * N_HEADS + h)),
        ],
        out_specs=pl.BlockSpec((1, lc, HEAD_DIM), lambda i, h: (i, 0, h)),
        out_shape=jax.ShapeDtypeStruct((b, lc, D_BRANCH), BF16),
        compiler_params=_params(2),
        name="ctx_attention",
    )(qkv3, qkv3, qkv3)
    return out.reshape(b * lc, D_BRANCH)


def _conv_kernel(h_ref, wx_ref, wb_ref, wc_ref, w_ref, o_ref, wbf_ref):
    @pl.when(pl.program_id(1) == 0)
    def _():
        for k, ref in enumerate((wx_ref, wb_ref, wc_ref)):
            wbf_ref[k] = ref[...].astype(BF16)

    h = h_ref[0]
    xc, bg, cg = (jnp.dot(h, wbf_ref[k], preferred_element_type=F32) for k in range(3))
    z = cg * xc
    n = z.shape[0]
    row = lax.broadcasted_iota(jnp.int32, z.shape, 0)
    z_prev = jnp.where(row == 0, 0.0, pltpu.roll(z, 1, 0))
    z_next = jnp.where(row == n - 1, 0.0, pltpu.roll(z, n - 1, 0))
    y = z_prev * w_ref[0, 0:1, :] + z * w_ref[0, 1:2, :] + z_next * w_ref[0, 2:3, :]
    o_ref[0] = (bg * y).astype(o_ref.dtype)


def short_conv(h, w_in, conv_w, layer, b, n, tc=256):
    d = h.shape[1]
    wcol = lambda col: pl.BlockSpec((None, d, tc), lambda j, i: (layer, 0, col // tc + j))
    out = pl.pallas_call(
        _conv_kernel,
        grid=(D_BRANCH // tc, b),
        in_specs=[pl.BlockSpec((1, n, d), lambda j, i: (i, 0, 0)),
                  wcol(COL_XC), wcol(COL_BG), wcol(COL_CG),
                  pl.BlockSpec((1, 3, tc), lambda j, i: (layer, 0, j))],
        out_specs=pl.BlockSpec((1, n, tc), lambda j, i: (i, 0, j)),
        out_shape=jax.ShapeDtypeStruct((b, n, D_BRANCH), BF16),
        scratch_shapes=[pltpu.VMEM((3, d, tc), BF16)],
        compiler_params=_params(2),
        name="short_conv",
    )(h.reshape(b, n, d), w_in, w_in, w_in, conv_w)
    return out.reshape(b * n, D_BRANCH)


def _gmlp_kernel(u_ref, v_ref, g_ref, ws_ref, bs_ref, o_ref):
    v = jax.nn.gelu(v_ref[...])
    mu = jnp.mean(v, axis=-1, keepdims=True)
    var = jnp.mean(jnp.square(v - mu), axis=-1, keepdims=True)
    vn = ((v - mu) * lax.rsqrt(var + NORM_EPS) * g_ref[...]).astype(BF16)
    for grp in range(GMLP_GROUPS):
        cols = slice(grp * 128, (grp + 1) * 128)
        ws = ws_ref[0, grp].astype(BF16)
        for ch in range(v.shape[0] // CHUNK):
            rows = slice(ch * CHUNK, (ch + 1) * CHUNK)
            s = jnp.dot(ws, vn[rows, cols], preferred_element_type=F32) + bs_ref[:, grp:grp + 1]
            o_ref[rows, cols] = (jax.nn.gelu(u_ref[rows, cols]) * s).astype(o_ref.dtype)


def chunk_gmlp(p, ln_g, w_s, b_s, layer, tm=4 * CHUNK):
    m = p.shape[0]
    return pl.pallas_call(
        _gmlp_kernel,
        grid=(m // tm,),
        in_specs=[
            pl.BlockSpec((tm, D_BRANCH), lambda i: (i, COL_U // D_BRANCH)),
            pl.BlockSpec((tm, D_BRANCH), lambda i: (i, COL_VS // D_BRANCH)),
            pl.BlockSpec((1, D_BRANCH), lambda i: (0, 0)),
            pl.BlockSpec((1, GMLP_GROUPS, CHUNK, CHUNK), lambda i: (layer, 0, 0, 0)),
            pl.BlockSpec((CHUNK, GMLP_GROUPS), lambda i: (0, 0)),
        ],
        out_specs=pl.BlockSpec((tm, D_BRANCH), lambda i: (i, 0)),
        out_shape=jax.ShapeDtypeStruct((m, D_BRANCH), BF16),
        compiler_params=_params(1),
        name="chunk_gmlp",
    )(p, p, ln_g[layer].reshape(1, D_BRANCH), w_s, b_s[layer].T)


def _merge_kernel(a_ref, c_ref, m_ref, g0_ref, g1_ref, g2_ref, w_ref, o_ref, wbf_ref):
    @pl.when(pl.program_id(1) == 0)
    def _():
        wbf_ref[...] = w_ref[...].astype(BF16)

    acc = None
    for i, (br, gl) in enumerate(((a_ref, g0_ref), (c_ref, g1_ref), (m_ref, g2_ref))):
        proj = jnp.dot(br[...], wbf_ref[i], preferred_element_type=F32)
        term = jax.nn.sigmoid(gl[...]) * proj
        acc = term if acc is None else acc + term
    o_ref[...] = acc.astype(o_ref.dtype)


def merge_branches(attn, conv, gm, p, w_branch, layer, tm=512, tn=1024):
    m = attn.shape[0]
    br = pl.BlockSpec((tm, D_BRANCH), lambda j, i: (i, 0))
    gate = lambda k: pl.BlockSpec((tm, tn), lambda j, i: (i, (COL_GATE + k * D_MODEL) // tn + j))
    return pl.pallas_call(
        _merge_kernel,
        grid=(D_MODEL // tn, m // tm),
        in_specs=[br, br, br, gate(0), gate(1), gate(2),
                  pl.BlockSpec((None, N_BRANCHES, D_BRANCH, tn), lambda j, i: (layer, 0, 0, j))],
        out_specs=pl.BlockSpec((tm, tn), lambda j, i: (i, j)),
        out_shape=jax.ShapeDtypeStruct((m, D_MODEL), BF16),
        scratch_shapes=[pltpu.VMEM((N_BRANCHES, D_BRANCH, tn), BF16)],
        compiler_params=_params(2),
        name="merge_branches",
    )(attn, conv, gm, p, p, p, w_branch)


def _out_proj_kernel(h_ref, w_ref, x_ref, mod_ref, o_ref, wbf_ref):
    @pl.when(pl.program_id(1) == 0)
    def _():
        wbf_ref[...] = w_ref[...].astype(BF16)

    y = jnp.dot(h_ref[...], wbf_ref[...], preferred_element_type=F32)
    o_ref[...] = x_ref[...] + mod_ref[0] * y


def out_proj_residual(h, w_out, layer, x, gate_vec, n, tm=1024, tn=1024):
    m, d = x.shape
    tm = min(tm, n)
    per_b = n // tm
    return pl.pallas_call(
        _out_proj_kernel,
        grid=(d // tn, m // tm),
        in_specs=[
            pl.BlockSpec((tm, d), lambda j, i: (i, 0)),
            pl.BlockSpec((None, d, tn), lambda j, i: (layer, 0, j)),
            pl.BlockSpec((tm, tn), lambda j, i: (i, j)),
            pl.BlockSpec((1, 1, tn), lambda j, i: (i // per_b, 0, j)),
        ],
        out_specs=pl.BlockSpec((tm, tn), lambda j, i: (i, j)),
        out_shape=jax.ShapeDtypeStruct((m, d), F32),
        scratch_shapes=[pltpu.VMEM((d, tn), BF16)],
        compiler_params=_params(2),
        name="out_proj_residual",
    )(h, w_out, x, gate_vec.reshape(-1, 1, d))


def _router_kernel(x_ref, g_ref, sh_ref, sc_ref, wr_ref, br_ref, h_ref, aff_ref):
    y = _rms(x_ref[0], g_ref[...])
    h = (y * (1 + sc_ref[0]) + sh_ref[0]).astype(BF16)
    h_ref[0] = h
    logits = _nt_dot(wr_ref[...].astype(BF16), h) + br_ref[...]
    mx = jnp.max(logits, axis=0, keepdims=True)
    e = jnp.exp(logits - mx)
    aff_ref[0] = e / jnp.sum(e, axis=0, keepdims=True)


def router(x, g, shift, scale, w_router, b_router, tm=256):
    b, n, d = x.shape
    tm = min(tm, n)
    return pl.pallas_call(
        _router_kernel,
        grid=(b, n // tm),
        in_specs=[
            pl.BlockSpec((1, tm, d), lambda i, j: (i, j, 0)),
            pl.BlockSpec((1, d), lambda i, j: (0, 0)),
            pl.BlockSpec((1, 1, d), lambda i, j: (i, 0, 0)),
            pl.BlockSpec((1, 1, d), lambda i, j: (i, 0, 0)),
            pl.BlockSpec((N_EXPERTS, d), lambda i, j: (0, 0)),
            pl.BlockSpec((N_EXPERTS, 1), lambda i, j: (0, 0)),
        ],
        out_specs=[
            pl.BlockSpec((1, tm, d), lambda i, j: (i, j, 0)),
            pl.BlockSpec((1, N_EXPERTS, tm), lambda i, j: (i, 0, j)),
        ],
        out_shape=[jax.ShapeDtypeStruct((b, n, d), BF16),
                   jax.ShapeDtypeStruct((b, N_EXPERTS, n), F32)],
        compiler_params=_params(2),
        name="router",
    )(x, g.reshape(1, d), shift.reshape(b, 1, d), scale.reshape(b, 1, d),
      w_router.T, b_router.reshape(N_EXPERTS, 1))


def _select_kernel(aff_ref, tri_ref, slot_ref, *, cap):
    bits = pltpu.bitcast(aff_ref[0], jnp.int32)

    def step(i, t):
        cand = t | lax.shift_left(jnp.int32(1), 30 - i)
        cnt = jnp.sum(jnp.where(bits >= cand, 1.0, 0.0), axis=1, keepdims=True)
        return jnp.where(cnt >= cap, cand, t)

    t = lax.fori_loop(0, 31, step, jnp.zeros((bits.shape[0], 1), jnp.int32))
    gt = bits > t
    eq = bits == t
    need = cap - jnp.sum(jnp.where(gt, 1.0, 0.0), axis=1, keepdims=True)
    tri = tri_ref[...]
    eq_rank = jnp.dot(jnp.where(eq, 1.0, 0.0).astype(BF16), tri, preferred_element_type=F32)
    sel = gt | (eq & (eq_rank < need))
    slot = jnp.dot(jnp.where(sel, 1.0, 0.0).astype(BF16), tri, preferred_element_type=F32)
    slot_ref[0] = jnp.where(sel, slot.astype(jnp.int32), -1)


def select_tokens(aff, cap):
    b, e, n = aff.shape
    tri = jnp.triu(jnp.ones((n, n), BF16), k=1)
    return pl.pallas_call(
        functools.partial(_select_kernel, cap=cap),
        grid=(b,),
        in_specs=[
            pl.BlockSpec((1, e, n), lambda i: (i, 0, 0)),
            pl.BlockSpec((n, n), lambda i: (0, 0)),
        ],
        out_specs=pl.BlockSpec((1, e, n), lambda i: (i, 0, 0)),
        out_shape=jax.ShapeDtypeStruct((b, e, n), jnp.int32),
        compiler_params=_params(1),
        name="select_tokens",
    )(aff, tri)


def _onehot(slot, cap):
    hit = lax.broadcasted_iota(jnp.int32, (cap, slot.shape[1]), 0) == slot
    return hit, jnp.where(hit, 1.0, 0.0)


def _gather_kernel(h_ref, slot_ref, aff_ref, xs_ref, gate_ref, *, cap):
    hit, onehot = _onehot(slot_ref[0], cap)
    xs_ref[0] = jnp.dot(onehot.astype(BF16), h_ref[0], preferred_element_type=F32).astype(BF16)
    gate_ref[0] = jnp.sum(jnp.where(hit, aff_ref[0], 0.0), axis=1, keepdims=True)


def gather_tokens(h, slot, aff, cap):
    b, n, d = h.shape
    e = N_EXPERTS
    row = pl.BlockSpec((1, 1, n), lambda i, j: (i * e + j, 0, 0))
    return pl.pallas_call(
        functools.partial(_gather_kernel, cap=cap),
        grid=(b, e),
        in_specs=[pl.BlockSpec((1, n, d), lambda i, j: (i, 0, 0)), row, row],
        out_specs=[
            pl.BlockSpec((1, cap, d), lambda i, j: (j, i, 0)),
            pl.BlockSpec((1, cap, 1), lambda i, j: (j, i, 0)),
        ],
        out_shape=[jax.ShapeDtypeStruct((e, b * cap, d), BF16),
                   jax.ShapeDtypeStruct((e, b * cap, 1), F32)],
        compiler_params=_params(2),
        name="gather_tokens",
    )(h, slot.reshape(b * e, 1, n), aff.reshape(b * e, 1, n))


def _ffn_kernel(*refs, n_sets, n_f):
    xs_refs, gate_refs = refs[0:2 * n_sets:2], refs[1:2 * n_sets:2]
    wg_ref, wu_ref, wd_ref = refs[2 * n_sets:2 * n_sets + 3]
    y_refs = refs[2 * n_sets + 3:3 * n_sets + 3]
    act_refs = refs[3 * n_sets + 3:]
    s = pl.program_id(1)
    tf = wg_ref.shape[1]

    @pl.when(s < n_f)
    def _():
        wg = wg_ref[...].astype(BF16)
        wu = wu_ref[...].astype(BF16)
        for xs_ref, act_ref in zip(xs_refs, act_refs):
            xs = xs_ref[0]
            hg = jnp.dot(xs, wg, preferred_element_type=F32)
            hu = jnp.dot(xs, wu, preferred_element_type=F32)
            act_ref[s] = (jax.nn.silu(hg) * hu).astype(BF16)

    @pl.when(s >= n_f)
    def _():
        wd = wd_ref[...].astype(BF16)
        for gate_ref, y_ref, act_ref in zip(gate_refs, y_refs, act_refs):
            y = None
            for f in range(n_f):
                part = jnp.dot(act_ref[f], wd[f * tf:(f + 1) * tf, :], preferred_element_type=F32)
                y = part if y is None else y + part
            y_ref[0] = (y * gate_ref[0]).astype(y_ref.dtype)


def expert_ffn(sets, w_gate, w_up, w_down, layer, tf=512, tn=512):
    d, ff = w_gate.shape[2], w_gate.shape[3]
    n_f, n_d = ff // tf, d // tn
    f_idx = lambda s: jnp.minimum(s, n_f - 1)
    d_idx = lambda s: jnp.maximum(s - n_f, 0)
    in_specs, operands, out_specs, out_shape, scratch = [], [], [], [], []
    for xs, gates in sets:
        r = xs.shape[1]
        in_specs += [pl.BlockSpec((1, r, d), lambda i, s: (i, 0, 0)),
                     pl.BlockSpec((1, r, 1), lambda i, s: (i, 0, 0))]
        operands += [xs, gates]
        out_specs.append(pl.BlockSpec((1, r, tn), lambda i, s: (i, 0, d_idx(s))))
        out_shape.append(jax.ShapeDtypeStruct(xs.shape, BF16))
        scratch.append(pltpu.VMEM((n_f, r, tf), BF16))
    in_specs += [
        pl.BlockSpec((None, None, d, tf), lambda i, s: (layer, i, 0, f_idx(s))),
        pl.BlockSpec((None, None, d, tf), lambda i, s: (layer, i, 0, f_idx(s))),
        pl.BlockSpec((None, None, ff, tn), lambda i, s: (layer, i, 0, d_idx(s))),
    ]
    return pl.pallas_call(
        functools.partial(_ffn_kernel, n_sets=len(sets), n_f=n_f),
        grid=(N_EXPERTS, n_f + n_d),
        in_specs=in_specs,
        out_specs=out_specs,
        out_shape=out_shape,
        scratch_shapes=scratch,
        compiler_params=_params(2),
        name="expert_ffn",
    )(*operands, w_gate, w_up, w_down)


def _combine_kernel(y_ref, slot_ref, x_ref, mod_ref, o_ref, onehot_ref, *, cap):
    @pl.when(pl.program_id(1) == 0)
    def _():
        for e in range(N_EXPERTS):
            onehot_ref[e * cap:(e + 1) * cap, :] = _onehot(slot_ref[e], cap)[1].astype(BF16)

    y = y_ref[...].reshape(N_EXPERTS * cap, y_ref.shape[2])
    moe = lax.dot_general(onehot_ref[...], y, (((0,), (0,)), ((), ())), preferred_element_type=F32)
    o_ref[0] = x_ref[0] + mod_ref[0] * moe


def combine(y, slot, x, gate_vec, cap, tn=512):
    b, n, d = x.shape
    e = N_EXPERTS
    return pl.pallas_call(
        functools.partial(_combine_kernel, cap=cap),
        grid=(b, d // tn),
        in_specs=[
            pl.BlockSpec((e, cap, tn), lambda i, j: (0, i, j)),
            pl.BlockSpec((e, 1, n), lambda i, j: (i, 0, 0)),
            pl.BlockSpec((1, n, tn), lambda i, j: (i, 0, j)),
            pl.BlockSpec((1, 1, tn), lambda i, j: (i, 0, j)),
        ],
        out_specs=pl.BlockSpec((1, n, tn), lambda i, j: (i, 0, j)),
        out_shape=jax.ShapeDtypeStruct((b, n, d), F32),
        scratch_shapes=[pltpu.VMEM((e * cap, n), BF16)],
        compiler_params=_params(2),
        name="combine",
    )(y, slot.reshape(b * e, 1, n), x, gate_vec.reshape(b, 1, d))


def moe_route(x, g, mod, w_router, b_router):
    n = x.shape[1]
    cap = CAPACITY_FACTOR * n // N_EXPERTS
    h, aff = router(x, g, mod[:, 3], mod[:, 4], w_router, b_router)
    slot = select_tokens(aff, cap)
    xs, gates = gather_tokens(h, slot, aff, cap)
    return slot, cap, (xs, gates)


def mixer_block(x, h, attn, mod, n, layer, w_in, conv_w, ln_g, w_s, b_s, w_branch, w_out):
    b = x.shape[0] // n
    p = matmul(h, w_in, layer, P_COL0, w_in.shape[2] - P_COL0)
    conv = short_conv(h, w_in, conv_w, layer, b, n)
    gm = chunk_gmlp(p, ln_g, w_s, b_s, layer)
    merged = merge_branches(attn, conv, gm, p, w_branch, layer)
    return out_proj_residual(merged, w_out, layer, x, mod[:, 2], n)


def kernel(x, c, ctx, c_ctx, w_mod, b_mod, norm1_g, w_in, na_rpb, conv_w, gmlp_ln_g, w_spatial, b_spatial,
           w_branch, w_out, norm2_g, w_router, b_router, w_e_gate, w_e_up, w_e_down, final_g):
    b, n, d = x.shape
    lc = ctx.shape[1]
    cvec = jnp.concatenate([c, c_ctx[None], jnp.zeros((8 - b - 1, d), F32)], axis=0)
    modv = modvec(cvec, w_mod, b_mod).reshape(DEPTH, 8, 6, d)
    cos_t, sin_t = rope_tables(n)
    bias = na_bias(na_rpb, n // GRID_W)
    mix_w = (w_in, conv_w, gmlp_ln_g, w_spatial, b_spatial, w_branch, w_out)
    ctx_s = ctx
    for layer in range(DEPTH):
        last = layer == DEPTH - 1
        mod = modv[layer, :b]
        mod_c = jnp.broadcast_to(modv[layer, b][None], (b, 6, d))
        route_w = (w_router[layer], b_router[layer])

        hc = norm_mod(ctx_s, norm1_g[layer], mod_c[:, 0], mod_c[:, 1]).reshape(b * lc, d)
        if last:
            kvc = qkv_proj(hc, w_in, layer, lc, cos_t, sin_t, rope=False, first_tile=1, n_tiles=2)
            kc_col = 0
        else:
            kvc = qkv_proj(hc, w_in, layer, lc, cos_t, sin_t, rope=False)
            kc_col = D_BRANCH
            attn_c = ctx_attention(kvc, b, lc)
            ctx_mid = mixer_block(ctx_s.reshape(b * lc, d), hc, attn_c, mod_c, lc, layer, *mix_w)
            ctx_mid = ctx_mid.reshape(b, lc, d)

        h = norm_mod(x, norm1_g[layer], mod[:, 0], mod[:, 1]).reshape(b * n, d)
        qkv = qkv_proj(h, w_in, layer, n, cos_t, sin_t, rope=True)
        attn = na_attention(qkv, kvc, kc_col, bias, layer, b, n)
        x = mixer_block(x.reshape(b * n, d), h, attn, mod, n, layer, *mix_w).reshape(b, n, d)

        slot, cap, rows = moe_route(x, norm2_g[layer], mod, *route_w)
        if last:
            (y,) = expert_ffn([rows], w_e_gate, w_e_up, w_e_down, layer)
        else:
            slot_c, cap_c, rows_c = moe_route(ctx_mid, norm2_g[layer], mod_c, *route_w)
            y, y_c = expert_ffn([rows, rows_c], w_e_gate, w_e_up, w_e_down, layer)
            ctx_s = combine(y_c, slot_c, ctx_mid, mod_c[:, 5], cap_c)
        x = combine(y, slot, x, mod[:, 5], cap)
    return final_norm(x, final_g)
```

```python
import functools

import numpy as np
import jax
import jax.numpy as jnp
from jax import lax
from jax.experimental import pallas as pl
from jax.experimental.pallas import tpu as pltpu

D_MODEL = 2048
DEPTH = 2
GRID_W = 64
D_BRANCH = D_MODEL // 2
N_BRANCHES = 3
HEAD_DIM = 128
N_HEADS = D_BRANCH // HEAD_DIM
NA_KH = 8
NA_KW = 16
ROPE_THETA = 10000.0
ROPE_AXIS_DIM = HEAD_DIM // 2
CHUNK = 128
GMLP_GROUPS = D_BRANCH // 128
N_EXPERTS = 16
CAPACITY_FACTOR = 2
NORM_EPS = 1e-6
NEG_INF = -1e30
LOG2_E = float(np.log2(np.e))

QKV_COLS = 3 * D_BRANCH
COL_XC, COL_BG, COL_CG = (QKV_COLS + i * D_BRANCH for i in range(3))
P_COL0 = QKV_COLS + 3 * D_BRANCH
COL_U, COL_VS, COL_GATE = 0, D_BRANCH, 2 * D_BRANCH

NA_Q_ROWS = 4
NA_BAND_ROWS = 12
NA_TQ = NA_Q_ROWS * GRID_W
NA_TK = NA_BAND_ROWS * GRID_W
NA_HEADS_PER_STEP = 4
NA_HB = NA_HEADS_PER_STEP * HEAD_DIM
NA_SCORE_LEAD = 2

V7X_VMEM_LIMIT = 56 * 1024 * 1024

BF16 = jnp.bfloat16
F32 = jnp.float32


def _params(n_axes, vmem=V7X_VMEM_LIMIT):
    return pltpu.CompilerParams(dimension_semantics=("arbitrary",) * n_axes, vmem_limit_bytes=vmem)


def _modvec_kernel(c_ref, w_ref, b_ref, o_ref):
    s = jax.nn.silu(c_ref[...]).astype(BF16)
    o_ref[0] = jnp.dot(s, w_ref[0].astype(BF16), preferred_element_type=F32) + b_ref[0]


def modvec(cvec, w_mod, b_mod, tn=1024):
    nl, d, n6 = w_mod.shape
    return pl.pallas_call(
        _modvec_kernel,
        grid=(nl, n6 // tn),
        in_specs=[
            pl.BlockSpec((8, d), lambda l, j: (0, 0)),
            pl.BlockSpec((1, d, tn), lambda l, j: (l, 0, j)),
            pl.BlockSpec((1, 1, tn), lambda l, j: (l, 0, j)),
        ],
        out_specs=pl.BlockSpec((1, 8, tn), lambda l, j: (l, 0, j)),
        out_shape=jax.ShapeDtypeStruct((nl, 8, n6), F32),
        compiler_params=_params(2),
        name="modvec",
    )(cvec, w_mod, b_mod.reshape(nl, 1, n6))


def _rms(x, g):
    ms = jnp.mean(x * x, axis=-1, keepdims=True)
    return x * lax.rsqrt(ms + NORM_EPS) * g


def _norm_mod_kernel(x_ref, g_ref, sh_ref, sc_ref, o_ref):
    y = _rms(x_ref[0], g_ref[...])
    o_ref[0] = (y * (1 + sc_ref[0]) + sh_ref[0]).astype(o_ref.dtype)


def _norm_kernel(x_ref, g_ref, o_ref):
    o_ref[0] = _rms(x_ref[0], g_ref[...]).astype(o_ref.dtype)


def norm_mod(x, g, shift, scale, out_dtype=BF16, tm=256):
    b, n, d = x.shape
    tm = min(tm, n)
    return pl.pallas_call(
        _norm_mod_kernel,
        grid=(b, n // tm),
        in_specs=[
            pl.BlockSpec((1, tm, d), lambda i, j: (i, j, 0)),
            pl.BlockSpec((1, d), lambda i, j: (0, 0)),
            pl.BlockSpec((1, 1, d), lambda i, j: (i, 0, 0)),
            pl.BlockSpec((1, 1, d), lambda i, j: (i, 0, 0)),
        ],
        out_specs=pl.BlockSpec((1, tm, d), lambda i, j: (i, j, 0)),
        out_shape=jax.ShapeDtypeStruct((b, n, d), out_dtype),
        compiler_params=_params(2),
        name="norm_mod",
    )(x, g.reshape(1, d), shift.reshape(b, 1, d), scale.reshape(b, 1, d))


def final_norm(x, g, tm=256):
    b, n, d = x.shape
    return pl.pallas_call(
        _norm_kernel,
        grid=(b, n // tm),
        in_specs=[
            pl.BlockSpec((1, tm, d), lambda i, j: (i, j, 0)),
            pl.BlockSpec((1, d), lambda i, j: (0, 0)),
        ],
        out_specs=pl.BlockSpec((1, tm, d), lambda i, j: (i, j, 0)),
        out_shape=jax.ShapeDtypeStruct((b, n, d), F32),
        compiler_params=_params(2),
        name="final_norm",
    )(x, g.reshape(1, d))


def _mm_kernel(x_ref, w_ref, o_ref, wbf_ref):
    @pl.when(pl.program_id(1) == 0)
    def _():
        wbf_ref[...] = w_ref[...].astype(BF16)

    o_ref[...] = jnp.dot(x_ref[...], wbf_ref[...], preferred_element_type=F32).astype(o_ref.dtype)


def matmul(x, w, layer, col_off, n_cols, out_dtype=F32, tm=1024, tn=1024):
    m, k = x.shape
    tm = min(tm, m)
    off = col_off // tn
    return pl.pallas_call(
        _mm_kernel,
        grid=(n_cols // tn, m // tm),
        in_specs=[
            pl.BlockSpec((tm, k), lambda j, i: (i, 0)),
            pl.BlockSpec((None, k, tn), lambda j, i: (layer, 0, off + j)),
        ],
        out_specs=pl.BlockSpec((tm, tn), lambda j, i: (i, j)),
        out_shape=jax.ShapeDtypeStruct((m, n_cols), out_dtype),
        scratch_shapes=[pltpu.VMEM((k, tn), BF16)],
        compiler_params=_params(2),
        name="matmul",
    )(x, w)


def _qkv_kernel(x_ref, w_ref, cos_ref, sin_ref, o_ref, wbf_ref, *, rope):
    @pl.when(pl.program_id(1) == 0)
    def _():
        wbf_ref[...] = w_ref[...].astype(BF16)

    def plain():
        o_ref[...] = jnp.dot(x_ref[...], wbf_ref[...], preferred_element_type=F32).astype(BF16)

    if not rope:
        plain()
        return
    c = pl.program_id(0)

    @pl.when(c < 2)
    def _():
        cs = cos_ref[...]
        sn = sin_ref[...]
        lane = lax.broadcasted_iota(jnp.int32, cs.shape, 1)
        first_half = (lane % ROPE_AXIS_DIM) < (ROPE_AXIS_DIM // 2)
        for pair in range(N_HEADS // 2):
            cols = slice(pair * 2 * HEAD_DIM, (pair + 1) * 2 * HEAD_DIM)
            acc = jnp.dot(x_ref[...], wbf_ref[:, cols], preferred_element_type=F32)
            for h in range(2):
                t = acc[:, h * HEAD_DIM:(h + 1) * HEAD_DIM]
                partner = jnp.where(first_half,
                                    pltpu.roll(t, HEAD_DIM - ROPE_AXIS_DIM // 2, 1),
                                    pltpu.roll(t, ROPE_AXIS_DIM // 2, 1))
                lo = (pair * 2 + h) * HEAD_DIM
                o_ref[:, lo:lo + HEAD_DIM] = (t * cs + partner * sn).astype(BF16)

    @pl.when(c == 2)
    def _():
        plain()


def rope_tables(n):
    pos = jnp.arange(n)
    rc = jnp.stack([pos // GRID_W, pos % GRID_W], axis=-1).astype(F32)
    inv = ROPE_THETA ** (-jnp.arange(0, ROPE_AXIS_DIM, 2, dtype=F32) / ROPE_AXIS_DIM)
    ang = rc[:, :, None] * inv
    cos, sin = jnp.cos(ang), jnp.sin(ang)
    cos_t = jnp.concatenate([cos, cos], axis=-1).reshape(n, HEAD_DIM)
    sin_t = jnp.concatenate([-sin, sin], axis=-1).reshape(n, HEAD_DIM)
    return cos_t, sin_t


def qkv_proj(x, w, layer, n, cos_t, sin_t, rope, first_tile=0, n_tiles=3, tm=1024):
    m, k = x.shape
    tm = min(tm, n if rope else m)
    nt = n // tm if rope else 1
    tab = pl.BlockSpec((tm, HEAD_DIM), lambda c, i: (i % nt, 0))
    return pl.pallas_call(
        functools.partial(_qkv_kernel, rope=rope),
        grid=(n_tiles, m // tm),
        in_specs=[
            pl.BlockSpec((tm, k), lambda c, i: (i, 0)),
            pl.BlockSpec((None, k, D_BRANCH), lambda c, i: (layer, 0, first_tile + c)),
            tab, tab,
        ],
        out_specs=pl.BlockSpec((tm, D_BRANCH), lambda c, i: (i, c)),
        out_shape=jax.ShapeDtypeStruct((m, n_tiles * D_BRANCH), BF16),
        scratch_shapes=[pltpu.VMEM((k, D_BRANCH), BF16)],
        compiler_params=_params(2),
        name="qkv_proj",
    )(x, w, cos_t, sin_t)


def _nt_dot(a, b):
    return lax.dot_general(a, b, (((1,), (1,)), ((), ())), preferred_element_type=F32)


def _na_pattern_rows(g, rows):
    band_row = int(np.clip(g * NA_Q_ROWS - NA_KH // 2, 0, rows - NA_BAND_ROWS))
    qr = g * NA_Q_ROWS + np.arange(NA_Q_ROWS)
    rs = np.clip(qr - NA_KH // 2, 0, rows - NA_KH)
    return band_row, qr, rs


def _na_bias_kernel(rpb_ref, o_ref, *, rows):
    shape = (GRID_W, 2 * GRID_W)
    qc = lax.broadcasted_iota(jnp.int32, shape, 0)
    lane = lax.broadcasted_iota(jnp.int32, shape, 1)
    low = lane < GRID_W
    kc = lane % GRID_W
    ws = jnp.clip(qc - NA_KW // 2, 0, GRID_W - NA_KW)
    col_ok = (kc >= ws) & (kc < ws + NA_KW)
    neg = jnp.full(shape, NEG_INF, F32)

    tiles = []
    for dr in range(2 * NA_KH - 1):
        t = jnp.broadcast_to(rpb_ref[0, 0, dr:dr + 1, :], shape)
        t = pltpu.roll(t, 2 * GRID_W - (NA_KW - 1), 1)
        for bit in range(6):
            t = jnp.where(((qc >> bit) & 1) == 1, pltpu.roll(t, 1 << bit, 1), t)
        t = jnp.where(low, t, pltpu.roll(t, GRID_W, 1))
        tiles.append(jnp.where(col_ok, t * LOG2_E, NEG_INF))

    n_groups = rows // NA_Q_ROWS
    for p, g in enumerate((0, 1, n_groups - 1)):
        band_row, qr, rs = _na_pattern_rows(g, rows)
        for qi in range(NA_Q_ROWS):
            def half(krl):
                kr = band_row + krl
                if rs[qi] <= kr < rs[qi] + NA_KH:
                    return tiles[kr - qr[qi] + NA_KH - 1]
                return neg
            for pair in range(NA_BAND_ROWS // 2):
                blk = jnp.where(low, half(2 * pair), half(2 * pair + 1))
                o_ref[0, p, 0, qi * GRID_W:(qi + 1) * GRID_W, pair * 128:(pair + 1) * 128] = blk


def na_bias(na_rpb, rows):
    nl, nh, ndr, ndc = na_rpb.shape
    band1, qr1, rs1 = _na_pattern_rows(1, rows)
    for g in range(2, rows // NA_Q_ROWS - 1):
        band, qr, rs = _na_pattern_rows(g, rows)
        assert (qr - band == qr1 - band1).all() and (rs - band == rs1 - band1).all()
    rpb = jnp.pad(na_rpb, ((0, 0), (0, 0), (0, 16 - ndr), (0, 128 - ndc)))
    return pl.pallas_call(
        functools.partial(_na_bias_kernel, rows=rows),
        grid=(nl, nh),
        in_specs=[pl.BlockSpec((1, 1, 16, 128), lambda l, h: (l, h, 0, 0))],
        out_specs=pl.BlockSpec((1, 3, 1, NA_TQ, NA_TK), lambda l, h: (l, 0, h, 0, 0)),
        out_shape=jax.ShapeDtypeStruct((nl, 3, nh, NA_TQ, NA_TK), F32),
        compiler_params=_params(2),
        name="na_bias",
    )(rpb)


def _na_kernel(q_ref, k_ref, v_ref, kc_ref, vc_ref, bias_ref, o_ref):
    g = pl.program_id(2)
    n_groups = k_ref.shape[1] // NA_TQ
    band_row = jnp.clip(g * NA_Q_ROWS - NA_KH // 2, 0, n_groups * NA_Q_ROWS - NA_BAND_ROWS)
    start = pl.multiple_of(band_row * GRID_W, NA_TQ)
    pat = jnp.where(g == 0, 0, jnp.where(g == n_groups - 1, 2, 1))
    scale = HEAD_DIM ** -0.5 * LOG2_E

    def scores(hh):
        cols = slice(hh * HEAD_DIM, (hh + 1) * HEAD_DIM)
        q = q_ref[0, :, cols]
        return _nt_dot(q, k_ref[0, pl.ds(start, NA_TK), cols]), _nt_dot(q, kc_ref[0, :, cols])

    def finish(hh, qk_loc, qk_ctx):
        cols = slice(hh * HEAD_DIM, (hh + 1) * HEAD_DIM)
        s_loc = qk_loc * scale + bias_ref[0, pat, hh]
        s_ctx = qk_ctx * scale
        m = jnp.maximum(jnp.max(s_loc, axis=1, keepdims=True), jnp.max(s_ctx, axis=1, keepdims=True))
        e_loc = jnp.exp2(s_loc - m)
        e_ctx = jnp.exp2(s_ctx - m)
        inv = 1.0 / (jnp.sum(e_loc, axis=1, keepdims=True) + jnp.sum(e_ctx, axis=1, keepdims=True))
        o = jnp.dot(e_loc.astype(BF16), v_ref[0, pl.ds(start, NA_TK), cols], preferred_element_type=F32)
        o = o + jnp.dot(e_ctx.astype(BF16), vc_ref[0, :, cols], preferred_element_type=F32)
        o_ref[0, :, cols] = (o * inv).astype(o_ref.dtype)

    qk = {}
    for hh in range(NA_HEADS_PER_STEP + NA_SCORE_LEAD):
        if hh < NA_HEADS_PER_STEP:
            qk[hh] = scores(hh)
        if hh >= NA_SCORE_LEAD:
            finish(hh - NA_SCORE_LEAD, *qk.pop(hh - NA_SCORE_LEAD))


def na_attention(qkv, kvc, kc_col, bias, layer, b, n):
    lc = kvc.shape[0] // b
    qkv3 = qkv.reshape(b, n, QKV_COLS)
    kvc3 = kvc.reshape(b, lc, kvc.shape[1])
    hbs = D_BRANCH // NA_HB
    kcb = kc_col // NA_HB
    out = pl.pallas_call(
        _na_kernel,
        grid=(hbs, b, n // NA_TQ),
        in_specs=[
            pl.BlockSpec((1, NA_TQ, NA_HB), lambda h, i, g: (i, g, h)),
            pl.BlockSpec((1, n, NA_HB), lambda h, i, g: (i, 0, hbs + h)),
            pl.BlockSpec((1, n, NA_HB), lambda h, i, g: (i, 0, 2 * hbs + h)),
            pl.BlockSpec((1, lc, NA_HB), lambda h, i, g: (i, 0, kcb + h)),
            pl.BlockSpec((1, lc, NA_HB), lambda h, i, g: (i, 0, kcb + hbs + h)),
            pl.BlockSpec((1, 3, NA_HEADS_PER_STEP, NA_TQ, NA_TK), lambda h, i, g: (layer, 0, h, 0, 0)),
        ],
        out_specs=pl.BlockSpec((1, NA_TQ, NA_HB), lambda h, i, g: (i, g, h)),
        out_shape=jax.ShapeDtypeStruct((b, n, D_BRANCH), BF16),
        compiler_params=_params(3),
        name="na_attention",
    )(qkv3, qkv3, qkv3, kvc3, kvc3, bias)
    return out.reshape(b * n, D_BRANCH)


def _ctx_attn_kernel(q_ref, k_ref, v_ref, o_ref):
    s = _nt_dot(q_ref[0], k_ref[0]) * (HEAD_DIM ** -0.5)
    m = jnp.max(s, axis=1, keepdims=True)
    e = jnp.exp(s - m)
    p = e * (1.0 / jnp.sum(e, axis=1, keepdims=True))
    o_ref[0] = jnp.dot(p.astype(BF16), v_ref[0], preferred_element_type=F32).astype(o_ref.dtype)


def ctx_attention(qkv, b, lc):
    qkv3 = qkv.reshape(b, lc, QKV_COLS)
    out = pl.pallas_call(
        _ctx_attn_kernel,
        grid=(b, N_HEADS),
        in_specs=[
            pl.BlockSpec((1, lc, HEAD_DIM), lambda i, h: (i, 0, h)),
            pl.BlockSpec((1, lc, HEAD_DIM), lambda i, h: (i, 0, N_HEADS + h)),
            pl.BlockSpec((1, lc, HEAD_DIM), lambda i, h: (i, 0, 2 * N_HEADS + h)),
        ],
        out_specs=pl.BlockSpec((1, lc, HEAD_DIM), lambda i, h: (i, 0, h)),
        out_shape=jax.ShapeDtypeStruct((b, lc, D_BRANCH), BF16),
        compiler_params=_params(2),
        name="ctx_attention",
    )(qkv3, qkv3, qkv3)
    return out.reshape(b * lc, D_BRANCH)


def _conv_kernel(h_ref, wx_ref, wb_ref, wc_ref, w_ref, o_ref, wbf_ref):
    @pl.when(pl.program_id(1) == 0)
    def _():
        for k, ref in enumerate((wx_ref, wb_ref, wc_ref)):
            wbf_ref[k] = ref[...].astype(BF16)

    h = h_ref[0]
    xc, bg, cg = (jnp.dot(h, wbf_ref[k], preferred_element_type=F32) for k in range(3))
    z = cg * xc
    n = z.shape[0]
    row = lax.broadcasted_iota(jnp.int32, z.shape, 0)
    z_prev = jnp.where(row == 0, 0.0, pltpu.roll(z, 1, 0))
    z_next = jnp.where(row == n - 1, 0.0, pltpu.roll(z, n - 1, 0))
    y = z_prev * w_ref[0, 0:1, :] + z * w_ref[0, 1:2, :] + z_next * w_ref[0, 2:3, :]
    o_ref[0] = (bg * y).astype(o_ref.dtype)


def short_conv(h, w_in, conv_w, layer, b, n, tc=256):
    d = h.shape[1]
    wcol = lambda col: pl.BlockSpec((None, d, tc), lambda j, i: (layer, 0, col // tc + j))
    out = pl.pallas_call(
        _conv_kernel,
        grid=(D_BRANCH // tc, b),
        in_specs=[pl.BlockSpec((1, n, d), lambda j, i: (i, 0, 0)),
                  wcol(COL_XC), wcol(COL_BG), wcol(COL_CG),
                  pl.BlockSpec((1, 3, tc), lambda j, i: (layer, 0, j))],
        out_specs=pl.BlockSpec((1, n, tc), lambda j, i: (i, 0, j)),
        out_shape=jax.ShapeDtypeStruct((b, n, D_BRANCH), BF16),
        scratch_shapes=[pltpu.VMEM((3, d, tc), BF16)],
        compiler_params=_params(2),
        name="short_conv",
    )(h.reshape(b, n, d), w_in, w_in, w_in, conv_w)
    return out.reshape(b * n, D_BRANCH)


def _gmlp_kernel(u_ref, v_ref, g_ref, ws_ref, bs_ref, o_ref):
    v = jax.nn.gelu(v_ref[...])
    mu = jnp.mean(v, axis=-1, keepdims=True)
    var = jnp.mean(jnp.square(v - mu), axis=-1, keepdims=True)
    vn = ((v - mu) * lax.rsqrt(var + NORM_EPS) * g_ref[...]).astype(BF16)
    for grp in range(GMLP_GROUPS):
        cols = slice(grp * 128, (grp + 1) * 128)
        ws = ws_ref[0, grp].astype(BF16)
        for ch in range(v.shape[0] // CHUNK):
            rows = slice(ch * CHUNK, (ch + 1) * CHUNK)
            s = jnp.dot(ws, vn[rows, cols], preferred_element_type=F32) + bs_ref[:, grp:grp + 1]
            o_ref[rows, cols] = (jax.nn.gelu(u_ref[rows, cols]) * s).astype(o_ref.dtype)


def chunk_gmlp(p, ln_g, w_s, b_s, layer, tm=4 * CHUNK):
    m = p.shape[0]
    return pl.pallas_call(
        _gmlp_kernel,
        grid=(m // tm,),
        in_specs=[
            pl.BlockSpec((tm, D_BRANCH), lambda i: (i, COL_U // D_BRANCH)),
            pl.BlockSpec((tm, D_BRANCH), lambda i: (i, COL_VS // D_BRANCH)),
            pl.BlockSpec((1, D_BRANCH), lambda i: (0, 0)),
            pl.BlockSpec((1, GMLP_GROUPS, CHUNK, CHUNK), lambda i: (layer, 0, 0, 0)),
            pl.BlockSpec((CHUNK, GMLP_GROUPS), lambda i: (0, 0)),
        ],
        out_specs=pl.BlockSpec((tm, D_BRANCH), lambda i: (i, 0)),
        out_shape=jax.ShapeDtypeStruct((m, D_BRANCH), BF16),
        compiler_params=_params(1),
        name="chunk_gmlp",
    )(p, p, ln_g[layer].reshape(1, D_BRANCH), w_s, b_s[layer].T)


def _merge_kernel(a_ref, c_ref, m_ref, g0_ref, g1_ref, g2_ref, w_ref, o_ref, wbf_ref):
    @pl.when(pl.program_id(1) == 0)
    def _():
        wbf_ref[...] = w_ref[...].astype(BF16)

    acc = None
    for i, (br, gl) in enumerate(((a_ref, g0_ref), (c_ref, g1_ref), (m_ref, g2_ref))):
        proj = jnp.dot(br[...], wbf_ref[i], preferred_element_type=F32)
        term = jax.nn.sigmoid(gl[...]) * proj
        acc = term if acc is None else acc + term
    o_ref[...] = acc.astype(o_ref.dtype)


def merge_branches(attn, conv, gm, p, w_branch, layer, tm=512, tn=1024):
    m = attn.shape[0]
    br = pl.BlockSpec((tm, D_BRANCH), lambda j, i: (i, 0))
    gate = lambda k: pl.BlockSpec((tm, tn), lambda j, i: (i, (COL_GATE + k * D_MODEL) // tn + j))
    return pl.pallas_call(
        _merge_kernel,
        grid=(D_MODEL // tn, m // tm),
        in_specs=[br, br, br, gate(0), gate(1), gate(2),
                  pl.BlockSpec((None, N_BRANCHES, D_BRANCH, tn), lambda j, i: (layer, 0, 0, j))],
        out_specs=pl.BlockSpec((tm, tn), lambda j, i: (i, j)),
        out_shape=jax.ShapeDtypeStruct((m, D_MODEL), BF16),
        scratch_shapes=[pltpu.VMEM((N_BRANCHES, D_BRANCH, tn), BF16)],
        compiler_params=_params(2),
        name="merge_branches",
    )(attn, conv, gm, p, p, p, w_branch)


def _out_proj_kernel(h_ref, w_ref, x_ref, mod_ref, o_ref, wbf_ref):
    @pl.when(pl.program_id(1) == 0)
    def _():
        wbf_ref[...] = w_ref[...].astype(BF16)

    y = jnp.dot(h_ref[...], wbf_ref[...], preferred_element_type=F32)
    o_ref[...] = x_ref[...] + mod_ref[0] * y


def out_proj_residual(h, w_out, layer, x, gate_vec, n, tm=1024, tn=1024):
    m, d = x.shape
    tm = min(tm, n)
    per_b = n // tm
    return pl.pallas_call(
        _out_proj_kernel,
        grid=(d // tn, m // tm),
        in_specs=[
            pl.BlockSpec((tm, d), lambda j, i: (i, 0)),
            pl.BlockSpec((None, d, tn), lambda j, i: (layer, 0, j)),
            pl.BlockSpec((tm, tn), lambda j, i: (i, j)),
            pl.BlockSpec((1, 1, tn), lambda j, i: (i // per_b, 0, j)),
        ],
        out_specs=pl.BlockSpec((tm, tn), lambda j, i: (i, j)),
        out_shape=jax.ShapeDtypeStruct((m, d), F32),
        scratch_shapes=[pltpu.VMEM((d, tn), BF16)],
        compiler_params=_params(2),
        name="out_proj_residual",
    )(h, w_out, x, gate_vec.reshape(-1, 1, d))


def _router_kernel(x_ref, g_ref, sh_ref, sc_ref, wr_ref, br_ref, h_ref, aff_ref):
    y = _rms(x_ref[0], g_ref[...])
    h = (y * (1 + sc_ref[0]) + sh_ref[0]).astype(BF16)
    h_ref[0] = h
    logits = _nt_dot(wr_ref[...].astype(BF16), h) + br_ref[...]
    mx = jnp.max(logits, axis=0, keepdims=True)
    e = jnp.exp(logits - mx)
    aff_ref[0] = e / jnp.sum(e, axis=0, keepdims=True)


def router(x, g, shift, scale, w_router, b_router, tm=256):
    b, n, d = x.shape
    tm = min(tm, n)
    return pl.pallas_call(
        _router_kernel,
        grid=(b, n // tm),
        in_specs=[
            pl.BlockSpec((1, tm, d), lambda i, j: (i, j, 0)),
            pl.BlockSpec((1, d), lambda i, j: (0, 0)),
            pl.BlockSpec((1, 1, d), lambda i, j: (i, 0, 0)),
            pl.BlockSpec((1, 1, d), lambda i, j: (i, 0, 0)),
            pl.BlockSpec((N_EXPERTS, d), lambda i, j: (0, 0)),
            pl.BlockSpec((N_EXPERTS, 1), lambda i, j: (0, 0)),
        ],
        out_specs=[
            pl.BlockSpec((1, tm, d), lambda i, j: (i, j, 0)),
            pl.BlockSpec((1, N_EXPERTS, tm), lambda i, j: (i, 0, j)),
        ],
        out_shape=[jax.ShapeDtypeStruct((b, n, d), BF16),
                   jax.ShapeDtypeStruct((b, N_EXPERTS, n), F32)],
        compiler_params=_params(2),
        name="router",
    )(x, g.reshape(1, d), shift.reshape(b, 1, d), scale.reshape(b, 1, d),
      w_router.T, b_router.reshape(N_EXPERTS, 1))


def _select_kernel(aff_ref, tri_ref, slot_ref, *, cap):
    bits = pltpu.bitcast(aff_ref[0], jnp.int32)

    def step(i, t):
        cand = t | lax.shift_left(jnp.int32(1), 30 - i)
        cnt = jnp.sum(jnp.where(bits >= cand, 1.0, 0.0), axis=1, keepdims=True)
        return jnp.where(cnt >= cap, cand, t)

    t = lax.fori_loop(0, 31, step, jnp.zeros((bits.shape[0], 1), jnp.int32))
    gt = bits > t
    eq = bits == t
    need = cap - jnp.sum(jnp.where(gt, 1.0, 0.0), axis=1, keepdims=True)
    tri = tri_ref[...]
    eq_rank = jnp.dot(jnp.where(eq, 1.0, 0.0).astype(BF16), tri, preferred_element_type=F32)
    sel = gt | (eq & (eq_rank < need))
    slot = jnp.dot(jnp.where(sel, 1.0, 0.0).astype(BF16), tri, preferred_element_type=F32)
    slot_ref[0] = jnp.where(sel, slot.astype(jnp.int32), -1)


def select_tokens(aff, cap):
    b, e, n = aff.shape
    tri = jnp.triu(jnp.ones((n, n), BF16), k=1)
    return pl.pallas_call(
        functools.partial(_select_kernel, cap=cap),
        grid=(b,),
        in_specs=[
            pl.BlockSpec((1, e, n), lambda i: (i, 0, 0)),
            pl.BlockSpec((n, n), lambda i: (0, 0)),
        ],
        out_specs=pl.BlockSpec((1, e, n), lambda i: (i, 0, 0)),
        out_shape=jax.ShapeDtypeStruct((b, e, n), jnp.int32),
        compiler_params=_params(1),
        name="select_tokens",
    )(aff, tri)


def _onehot(slot, cap):
    hit = lax.broadcasted_iota(jnp.int32, (cap, slot.shape[1]), 0) == slot
    return hit, jnp.where(hit, 1.0, 0.0)


def _gather_kernel(h_ref, slot_ref, aff_ref, xs_ref, gate_ref, *, cap):
    hit, onehot = _onehot(slot_ref[0], cap)
    xs_ref[0] = jnp.dot(onehot.astype(BF16), h_ref[0], preferred_element_type=F32).astype(BF16)
    gate_ref[0] = jnp.sum(jnp.where(hit, aff_ref[0], 0.0), axis=1, keepdims=True)


def gather_tokens(h, slot, aff, cap):
    b, n, d = h.shape
    e = N_EXPERTS
    row = pl.BlockSpec((1, 1, n), lambda i, j: (i * e + j, 0, 0))
    return pl.pallas_call(
        functools.partial(_gather_kernel, cap=cap),
        grid=(b, e),
        in_specs=[pl.BlockSpec((1, n, d), lambda i, j: (i, 0, 0)), row, row],
        out_specs=[
            pl.BlockSpec((1, cap, d), lambda i, j: (j, i, 0)),
            pl.BlockSpec((1, cap, 1), lambda i, j: (j, i, 0)),
        ],
        out_shape=[jax.ShapeDtypeStruct((e, b * cap, d), BF16),
                   jax.ShapeDtypeStruct((e, b * cap, 1), F32)],
        compiler_params=_params(2),
        name="gather_tokens",
    )(h, slot.reshape(b * e, 1, n), aff.reshape(b * e, 1, n))


def _ffn_kernel(*refs, n_sets, n_f):
    xs_refs, gate_refs = refs[0:2 * n_sets:2], refs[1:2 * n_sets:2]
    wg_ref, wu_ref, wd_ref = refs[2 * n_sets:2 * n_sets + 3]
    y_refs = refs[2 * n_sets + 3:3 * n_sets + 3]
    act_refs = refs[3 * n_sets + 3:]
    s = pl.program_id(1)
    tf = wg_ref.shape[1]

    @pl.when(s < n_f)
    def _():
        wg = wg_ref[...].astype(BF16)
        wu = wu_ref[...].astype(BF16)
        for xs_ref, act_ref in zip(xs_refs, act_refs):
            xs = xs_ref[0]
            hg = jnp.dot(xs, wg, preferred_element_type=F32)
            hu = jnp.dot(xs, wu, preferred_element_type=F32)
            act_ref[s] = (jax.nn.silu(hg) * hu).astype(BF16)

    @pl.when(s >= n_f)
    def _():
        wd = wd_ref[...].astype(BF16)
        for gate_ref, y_ref, act_ref in zip(gate_refs, y_refs, act_refs):
            y = None
            for f in range(n_f):
                part = jnp.dot(act_ref[f], wd[f * tf:(f + 1) * tf, :], preferred_element_type=F32)
                y = part if y is None else y + part
            y_ref[0] = (y * gate_ref[0]).astype(y_ref.dtype)


def expert_ffn(sets, w_gate, w_up, w_down, layer, tf=512, tn=512):
    d, ff = w_gate.shape[2], w_gate.shape[3]
    n_f, n_d = ff // tf, d // tn
    d_idx = lambda s: jnp.maximum(s - n_f, 0)

    def up_tile(i, s):
        ahead = s >= n_f
        return (layer, jnp.where(ahead, jnp.minimum(i + 1, N_EXPERTS - 1), i), 0,
                jnp.where(ahead, 0, jnp.minimum(s, n_f - 1)))

    def down_tile(i, s):
        behind = s < n_f
        return (layer, jnp.where(behind, jnp.maximum(i - 1, 0), i), 0, jnp.where(behind, n_d - 1, s - n_f))
    in_specs, operands, out_specs, out_shape, scratch = [], [], [], [], []
    for xs, gates in sets:
        r = xs.shape[1]
        in_specs += [pl.BlockSpec((1, r, d), lambda i, s: (i, 0, 0)),
                     pl.BlockSpec((1, r, 1), lambda i, s: (i, 0, 0))]
        operands += [xs, gates]
        out_specs.append(pl.BlockSpec((1, r, tn), lambda i, s: (i, 0, d_idx(s))))
        out_shape.append(jax.ShapeDtypeStruct(xs.shape, BF16))
        scratch.append(pltpu.VMEM((n_f, r, tf), BF16))
    in_specs += [
        pl.BlockSpec((None, None, d, tf), up_tile),
        pl.BlockSpec((None, None, d, tf), up_tile),
        pl.BlockSpec((None, None, ff, tn), down_tile),
    ]
    return pl.pallas_call(
        functools.partial(_ffn_kernel, n_sets=len(sets), n_f=n_f),
        grid=(N_EXPERTS, n_f + n_d),
        in_specs=in_specs,
        out_specs=out_specs,
        out_shape=out_shape,
        scratch_shapes=scratch,
        compiler_params=_params(2),
        name="expert_ffn",
    )(*operands, w_gate, w_up, w_down)


def _combine_kernel(y_ref, slot_ref, x_ref, mod_ref, o_ref, onehot_ref, *, cap):
    @pl.when(pl.program_id(1) == 0)
    def _():
        for e in range(N_EXPERTS):
            onehot_ref[e * cap:(e + 1) * cap, :] = _onehot(slot_ref[e], cap)[1].astype(BF16)

    y = y_ref[...].reshape(N_EXPERTS * cap, y_ref.shape[2])
    moe = lax.dot_general(onehot_ref[...], y, (((0,), (0,)), ((), ())), preferred_element_type=F32)
    o_ref[0] = x_ref[0] + mod_ref[0] * moe


def combine(y, slot, x, gate_vec, cap, tn=512):
    b, n, d = x.shape
    e = N_EXPERTS
    return pl.pallas_call(
        functools.partial(_combine_kernel, cap=cap),
        grid=(b, d // tn),
        in_specs=[
            pl.BlockSpec((e, cap, tn), lambda i, j: (0, i, j)),
            pl.BlockSpec((e, 1, n), lambda i, j: (i, 0, 0)),
            pl.BlockSpec((1, n, tn), lambda i, j: (i, 0, j)),
            pl.BlockSpec((1, 1, tn), lambda i, j: (i, 0, j)),
        ],
        out_specs=pl.BlockSpec((1, n, tn), lambda i, j: (i, 0, j)),
        out_shape=jax.ShapeDtypeStruct((b, n, d), F32),
        scratch_shapes=[pltpu.VMEM((e * cap, n), BF16)],
        compiler_params=_params(2),
        name="combine",
    )(y, slot.reshape(b * e, 1, n), x, gate_vec.reshape(b, 1, d))


def moe_route(x, g, mod, w_router, b_router):
    n = x.shape[1]
    cap = CAPACITY_FACTOR * n // N_EXPERTS
    h, aff = router(x, g, mod[:, 3], mod[:, 4], w_router, b_router)
    slot = select_tokens(aff, cap)
    xs, gates = gather_tokens(h, slot, aff, cap)
    return slot, cap, (xs, gates)


def mixer_block(x, h, attn, mod, n, layer, w_in, conv_w, ln_g, w_s, b_s, w_branch, w_out):
    b = x.shape[0] // n
    p = matmul(h, w_in, layer, P_COL0, w_in.shape[2] - P_COL0)
    conv = short_conv(h, w_in, conv_w, layer, b, n)
    gm = chunk_gmlp(p, ln_g, w_s, b_s, layer)
    merged = merge_branches(attn, conv, gm, p, w_branch, layer)
    return out_proj_residual(merged, w_out, layer, x, mod[:, 2], n)


def kernel(x, c, ctx, c_ctx, w_mod, b_mod, norm1_g, w_in, na_rpb, conv_w, gmlp_ln_g, w_spatial, b_spatial,
           w_branch, w_out, norm2_g, w_router, b_router, w_e_gate, w_e_up, w_e_down, final_g):
    b, n, d = x.shape
    lc = ctx.shape[1]
    cvec = jnp.concatenate([c, c_ctx[None], jnp.zeros((8 - b - 1, d), F32)], axis=0)
    modv = modvec(cvec, w_mod, b_mod).reshape(DEPTH, 8, 6, d)
    cos_t, sin_t = rope_tables(n)
    bias = na_bias(na_rpb, n // GRID_W)
    mix_w = (w_in, conv_w, gmlp_ln_g, w_spatial, b_spatial, w_branch, w_out)
    ctx_s = ctx
    for layer in range(DEPTH):
        last = layer == DEPTH - 1
        mod = modv[layer, :b]
        mod_c = jnp.broadcast_to(modv[layer, b][None], (b, 6, d))
        route_w = (w_router[layer], b_router[layer])

        hc = norm_mod(ctx_s, norm1_g[layer], mod_c[:, 0], mod_c[:, 1]).reshape(b * lc, d)
        if last:
            kvc = qkv_proj(hc, w_in, layer, lc, cos_t, sin_t, rope=False, first_tile=1, n_tiles=2)
            kc_col = 0
        else:
            kvc = qkv_proj(hc, w_in, layer, lc, cos_t, sin_t, rope=False)
            kc_col = D_BRANCH
            attn_c = ctx_attention(kvc, b, lc)
            ctx_mid = mixer_block(ctx_s.reshape(b * lc, d), hc, attn_c, mod_c, lc, layer, *mix_w)
            ctx_mid = ctx_mid.reshape(b, lc, d)

        h = norm_mod(x, norm1_g[layer], mod[:, 0], mod[:, 1]).reshape(b * n, d)
        qkv = qkv_proj(h, w_in, layer, n, cos_t, sin_t, rope=True)
        attn = na_attention(qkv, kvc, kc_col, bias, layer, b, n)
        x = mixer_block(x.reshape(b * n, d), h, attn, mod, n, layer, *mix_w).reshape(b, n, d)

        slot, cap, rows = moe_route(x, norm2_g[layer], mod, *route_w)
        if last:
            (y,) = expert_ffn([rows], w_e_gate, w_e_up, w_e_down, layer)
        else:
            slot_c, cap_c, rows_c = moe_route(ctx_mid, norm2_g[layer], mod_c, *route_w)
            y, y_c = expert_ffn([rows, rows_c], w_e_gate, w_e_up, w_e_down, layer)
            ctx_s = combine(y_c, slot_c, ctx_mid, mod_c[:, 5], cap_c)
        x = combine(y, slot, x, mod[:, 5], cap)
    return final_norm(x, final_g)
```

```python
import functools

import numpy as np
import jax
import jax.numpy as jnp
from jax import lax
from jax.experimental import pallas as pl
from jax.experimental.pallas import tpu as pltpu

D_MODEL = 2048
DEPTH = 2
GRID_W = 64
D_BRANCH = D_MODEL // 2
N_BRANCHES = 3
HEAD_DIM = 128
N_HEADS = D_BRANCH // HEAD_DIM
NA_KH = 8
NA_KW = 16
ROPE_THETA = 10000.0
ROPE_AXIS_DIM = HEAD_DIM // 2
CHUNK = 128
GMLP_GROUPS = D_BRANCH // 128
N_EXPERTS = 16
CAPACITY_FACTOR = 2
NORM_EPS = 1e-6
NEG_INF = -1e30
LOG2_E = float(np.log2(np.e))

QKV_COLS = 3 * D_BRANCH
COL_XC, COL_BG, COL_CG = (QKV_COLS + i * D_BRANCH for i in range(3))
P_COL0 = QKV_COLS + 3 * D_BRANCH
COL_U, COL_VS, COL_GATE = 0, D_BRANCH, 2 * D_BRANCH

NA_Q_ROWS = 4
NA_BAND_ROWS = 12
NA_TQ = NA_Q_ROWS * GRID_W
NA_TK = NA_BAND_ROWS * GRID_W
NA_HEADS_PER_STEP = 4
NA_HB = NA_HEADS_PER_STEP * HEAD_DIM
NA_SCORE_LEAD = 2

V7X_VMEM_LIMIT = 56 * 1024 * 1024

BF16 = jnp.bfloat16
F32 = jnp.float32


def _params(n_axes, vmem=V7X_VMEM_LIMIT):
    return pltpu.CompilerParams(dimension_semantics=("arbitrary",) * n_axes, vmem_limit_bytes=vmem)


def _modvec_kernel(c_ref, w_ref, b_ref, o_ref):
    s = jax.nn.silu(c_ref[...]).astype(BF16)
    o_ref[0] = jnp.dot(s, w_ref[0].astype(BF16), preferred_element_type=F32) + b_ref[0]


def modvec(cvec, w_mod, b_mod, tn=1024):
    nl, d, n6 = w_mod.shape
    return pl.pallas_call(
        _modvec_kernel,
        grid=(nl, n6 // tn),
        in_specs=[
            pl.BlockSpec((8, d), lambda l, j: (0, 0)),
            pl.BlockSpec((1, d, tn), lambda l, j: (l, 0, j)),
            pl.BlockSpec((1, 1, tn), lambda l, j: (l, 0, j)),
        ],
        out_specs=pl.BlockSpec((1, 8, tn), lambda l, j: (l, 0, j)),
        out_shape=jax.ShapeDtypeStruct((nl, 8, n6), F32),
        compiler_params=_params(2),
        name="modvec",
    )(cvec, w_mod, b_mod.reshape(nl, 1, n6))


def _rms(x, g):
    ms = jnp.mean(x * x, axis=-1, keepdims=True)
    return x * lax.rsqrt(ms + NORM_EPS) * g


def _norm_mod_kernel(x_ref, g_ref, sh_ref, sc_ref, o_ref):
    y = _rms(x_ref[0], g_ref[...])
    o_ref[0] = (y * (1 + sc_ref[0]) + sh_ref[0]).astype(o_ref.dtype)


def _norm_kernel(x_ref, g_ref, o_ref):
    o_ref[0] = _rms(x_ref[0], g_ref[...]).astype(o_ref.dtype)


def norm_mod(x, g, shift, scale, out_dtype=BF16, tm=1024):
    b, n, d = x.shape
    tm = min(tm, n)
    return pl.pallas_call(
        _norm_mod_kernel,
        grid=(b, n // tm),
        in_specs=[
            pl.BlockSpec((1, tm, d), lambda i, j: (i, j, 0)),
            pl.BlockSpec((1, d), lambda i, j: (0, 0)),
            pl.BlockSpec((1, 1, d), lambda i, j: (i, 0, 0)),
            pl.BlockSpec((1, 1, d), lambda i, j: (i, 0, 0)),
        ],
        out_specs=pl.BlockSpec((1, tm, d), lambda i, j: (i, j, 0)),
        out_shape=jax.ShapeDtypeStruct((b, n, d), out_dtype),
        compiler_params=_params(2),
        name="norm_mod",
    )(x, g.reshape(1, d), shift.reshape(b, 1, d), scale.reshape(b, 1, d))


def final_norm(x, g, tm=1024):
    b, n, d = x.shape
    return pl.pallas_call(
        _norm_kernel,
        grid=(b, n // tm),
        in_specs=[
            pl.BlockSpec((1, tm, d), lambda i, j: (i, j, 0)),
            pl.BlockSpec((1, d), lambda i, j: (0, 0)),
        ],
        out_specs=pl.BlockSpec((1, tm, d), lambda i, j: (i, j, 0)),
        out_shape=jax.ShapeDtypeStruct((b, n, d), F32),
        compiler_params=_params(2),
        name="final_norm",
    )(x, g.reshape(1, d))


def _mm_kernel(x_ref, w_ref, o_ref, wbf_ref):
    @pl.when(pl.program_id(1) == 0)
    def _():
        wbf_ref[...] = w_ref[...].astype(BF16)

    o_ref[...] = jnp.dot(x_ref[...], wbf_ref[...], preferred_element_type=F32).astype(o_ref.dtype)


def matmul(x, w, layer, col_off, n_cols, out_dtype=F32, tm=1024, tn=1024):
    m, k = x.shape
    tm = min(tm, m)
    off = col_off // tn
    return pl.pallas_call(
        _mm_kernel,
        grid=(n_cols // tn, m // tm),
        in_specs=[
            pl.BlockSpec((tm, k), lambda j, i: (i, 0)),
            pl.BlockSpec((None, k, tn), lambda j, i: (layer, 0, off + j)),
        ],
        out_specs=pl.BlockSpec((tm, tn), lambda j, i: (i, j)),
        out_shape=jax.ShapeDtypeStruct((m, n_cols), out_dtype),
        scratch_shapes=[pltpu.VMEM((k, tn), BF16)],
        compiler_params=_params(2),
        name="matmul",
    )(x, w)


def _qkv_kernel(x_ref, w_ref, cos_ref, sin_ref, o_ref, wbf_ref, *, rope):
    @pl.when(pl.program_id(1) == 0)
    def _():
        wbf_ref[...] = w_ref[...].astype(BF16)

    def plain():
        o_ref[...] = jnp.dot(x_ref[...], wbf_ref[...], preferred_element_type=F32).astype(BF16)

    if not rope:
        plain()
        return
    c = pl.program_id(0)

    @pl.when(c < 2)
    def _():
        cs = cos_ref[...]
        sn = sin_ref[...]
        lane = lax.broadcasted_iota(jnp.int32, cs.shape, 1)
        first_half = (lane % ROPE_AXIS_DIM) < (ROPE_AXIS_DIM // 2)
        for pair in range(N_HEADS // 2):
            cols = slice(pair * 2 * HEAD_DIM, (pair + 1) * 2 * HEAD_DIM)
            acc = jnp.dot(x_ref[...], wbf_ref[:, cols], preferred_element_type=F32)
            for h in range(2):
                t = acc[:, h * HEAD_DIM:(h + 1) * HEAD_DIM]
                partner = jnp.where(first_half,
                                    pltpu.roll(t, HEAD_DIM - ROPE_AXIS_DIM // 2, 1),
                                    pltpu.roll(t, ROPE_AXIS_DIM // 2, 1))
                lo = (pair * 2 + h) * HEAD_DIM
                o_ref[:, lo:lo + HEAD_DIM] = (t * cs + partner * sn).astype(BF16)

    @pl.when(c == 2)
    def _():
        plain()


def rope_tables(n):
    pos = jnp.arange(n)
    rc = jnp.stack([pos // GRID_W, pos % GRID_W], axis=-1).astype(F32)
    inv = ROPE_THETA ** (-jnp.arange(0, ROPE_AXIS_DIM, 2, dtype=F32) / ROPE_AXIS_DIM)
    ang = rc[:, :, None] * inv
    cos, sin = jnp.cos(ang), jnp.sin(ang)
    cos_t = jnp.concatenate([cos, cos], axis=-1).reshape(n, HEAD_DIM)
    sin_t = jnp.concatenate([-sin, sin], axis=-1).reshape(n, HEAD_DIM)
    return cos_t, sin_t


def qkv_proj(x, w, layer, n, cos_t, sin_t, rope, first_tile=0, n_tiles=3, tm=1024):
    m, k = x.shape
    tm = min(tm, n if rope else m)
    nt = n // tm if rope else 1
    tab = pl.BlockSpec((tm, HEAD_DIM), lambda c, i: (i % nt, 0))
    return pl.pallas_call(
        functools.partial(_qkv_kernel, rope=rope),
        grid=(n_tiles, m // tm),
        in_specs=[
            pl.BlockSpec((tm, k), lambda c, i: (i, 0)),
            pl.BlockSpec((None, k, D_BRANCH), lambda c, i: (layer, 0, first_tile + c)),
            tab, tab,
        ],
        out_specs=pl.BlockSpec((tm, D_BRANCH), lambda c, i: (i, c)),
        out_shape=jax.ShapeDtypeStruct((m, n_tiles * D_BRANCH), BF16),
        scratch_shapes=[pltpu.VMEM((k, D_BRANCH), BF16)],
        compiler_params=_params(2),
        name="qkv_proj",
    )(x, w, cos_t, sin_t)


def _nt_dot(a, b):
    return lax.dot_general(a, b, (((1,), (1,)), ((), ())), preferred_element_type=F32)


def _na_pattern_rows(g, rows):
    band_row = int(np.clip(g * NA_Q_ROWS - NA_KH // 2, 0, rows - NA_BAND_ROWS))
    qr = g * NA_Q_ROWS + np.arange(NA_Q_ROWS)
    rs = np.clip(qr - NA_KH // 2, 0, rows - NA_KH)
    return band_row, qr, rs


def _na_bias_kernel(rpb_ref, o_ref, *, rows):
    shape = (GRID_W, 2 * GRID_W)
    qc = lax.broadcasted_iota(jnp.int32, shape, 0)
    lane = lax.broadcasted_iota(jnp.int32, shape, 1)
    low = lane < GRID_W
    kc = lane % GRID_W
    ws = jnp.clip(qc - NA_KW // 2, 0, GRID_W - NA_KW)
    col_ok = (kc >= ws) & (kc < ws + NA_KW)
    neg = jnp.full(shape, NEG_INF, F32)

    tiles = []
    for dr in range(2 * NA_KH - 1):
        t = jnp.broadcast_to(rpb_ref[0, 0, dr:dr + 1, :], shape)
        t = pltpu.roll(t, 2 * GRID_W - (NA_KW - 1), 1)
        for bit in range(6):
            t = jnp.where(((qc >> bit) & 1) == 1, pltpu.roll(t, 1 << bit, 1), t)
        t = jnp.where(low, t, pltpu.roll(t, GRID_W, 1))
        tiles.append(jnp.where(col_ok, t * LOG2_E, NEG_INF))

    n_groups = rows // NA_Q_ROWS
    for p, g in enumerate((0, 1, n_groups - 1)):
        band_row, qr, rs = _na_pattern_rows(g, rows)
        for qi in range(NA_Q_ROWS):
            def half(krl):
                kr = band_row + krl
                if rs[qi] <= kr < rs[qi] + NA_KH:
                    return tiles[kr - qr[qi] + NA_KH - 1]
                return neg
            for pair in range(NA_BAND_ROWS // 2):
                blk = jnp.where(low, half(2 * pair), half(2 * pair + 1))
                o_ref[0, p, 0, qi * GRID_W:(qi + 1) * GRID_W, pair * 128:(pair + 1) * 128] = blk


def na_bias(na_rpb, rows):
    nl, nh, ndr, ndc = na_rpb.shape
    band1, qr1, rs1 = _na_pattern_rows(1, rows)
    for g in range(2, rows // NA_Q_ROWS - 1):
        band, qr, rs = _na_pattern_rows(g, rows)
        assert (qr - band == qr1 - band1).all() and (rs - band == rs1 - band1).all()
    rpb = jnp.pad(na_rpb, ((0, 0), (0, 0), (0, 16 - ndr), (0, 128 - ndc)))
    return pl.pallas_call(
        functools.partial(_na_bias_kernel, rows=rows),
        grid=(nl, nh),
        in_specs=[pl.BlockSpec((1, 1, 16, 128), lambda l, h: (l, h, 0, 0))],
        out_specs=pl.BlockSpec((1, 3, 1, NA_TQ, NA_TK), lambda l, h: (l, 0, h, 0, 0)),
        out_shape=jax.ShapeDtypeStruct((nl, 3, nh, NA_TQ, NA_TK), F32),
        compiler_params=_params(2),
        name="na_bias",
    )(rpb)


def _na_kernel(q_ref, k_ref, v_ref, kc_ref, vc_ref, bias_ref, o_ref):
    g = pl.program_id(2)
    n_groups = k_ref.shape[1] // NA_TQ
    band_row = jnp.clip(g * NA_Q_ROWS - NA_KH // 2, 0, n_groups * NA_Q_ROWS - NA_BAND_ROWS)
    start = pl.multiple_of(band_row * GRID_W, NA_TQ)
    pat = jnp.where(g == 0, 0, jnp.where(g == n_groups - 1, 2, 1))
    scale = HEAD_DIM ** -0.5 * LOG2_E

    def scores(hh):
        cols = slice(hh * HEAD_DIM, (hh + 1) * HEAD_DIM)
        q = q_ref[0, :, cols]
        return _nt_dot(q, k_ref[0, pl.ds(start, NA_TK), cols]), _nt_dot(q, kc_ref[0, :, cols])

    def finish(hh, qk_loc, qk_ctx):
        cols = slice(hh * HEAD_DIM, (hh + 1) * HEAD_DIM)
        s_loc = qk_loc * scale + bias_ref[0, pat, hh]
        s_ctx = qk_ctx * scale
        m = jnp.maximum(jnp.max(s_loc, axis=1, keepdims=True), jnp.max(s_ctx, axis=1, keepdims=True))
        e_loc = jnp.exp2(s_loc - m)
        e_ctx = jnp.exp2(s_ctx - m)
        inv = 1.0 / (jnp.sum(e_loc, axis=1, keepdims=True) + jnp.sum(e_ctx, axis=1, keepdims=True))
        o = jnp.dot(e_loc.astype(BF16), v_ref[0, pl.ds(start, NA_TK), cols], preferred_element_type=F32)
        o = o + jnp.dot(e_ctx.astype(BF16), vc_ref[0, :, cols], preferred_element_type=F32)
        o_ref[0, :, cols] = (o * inv).astype(o_ref.dtype)

    qk = {}
    for hh in range(NA_HEADS_PER_STEP + NA_SCORE_LEAD):
        if hh < NA_HEADS_PER_STEP:
            qk[hh] = scores(hh)
        if hh >= NA_SCORE_LEAD:
            finish(hh - NA_SCORE_LEAD, *qk.pop(hh - NA_SCORE_LEAD))


def na_attention(qkv, kvc, kc_col, bias, layer, b, n):
    lc = kvc.shape[0] // b
    qkv3 = qkv.reshape(b, n, QKV_COLS)
    kvc3 = kvc.reshape(b, lc, kvc.shape[1])
    hbs = D_BRANCH // NA_HB
    kcb = kc_col // NA_HB
    out = pl.pallas_call(
        _na_kernel,
        grid=(hbs, b, n // NA_TQ),
        in_specs=[
            pl.BlockSpec((1, NA_TQ, NA_HB), lambda h, i, g: (i, g, h)),
            pl.BlockSpec((1, n, NA_HB), lambda h, i, g: (i, 0, hbs + h)),
            pl.BlockSpec((1, n, NA_HB), lambda h, i, g: (i, 0, 2 * hbs + h)),
            pl.BlockSpec((1, lc, NA_HB), lambda h, i, g: (i, 0, kcb + h)),
            pl.BlockSpec((1, lc, NA_HB), lambda h, i, g: (i, 0, kcb + hbs + h)),
            pl.BlockSpec((1, 3, NA_HEADS_PER_STEP, NA_TQ, NA_TK), lambda h, i, g: (layer, 0, h, 0, 0)),
        ],
        out_specs=pl.BlockSpec((1, NA_TQ, NA_HB), lambda h, i, g: (i, g, h)),
        out_shape=jax.ShapeDtypeStruct((b, n, D_BRANCH), BF16),
        compiler_params=_params(3),
        name="na_attention",
    )(qkv3, qkv3, qkv3, kvc3, kvc3, bias)
    return out.reshape(b * n, D_BRANCH)


def _ctx_attn_kernel(q_ref, k_ref, v_ref, o_ref):
    for h in range(N_HEADS):
        cols = slice(h * HEAD_DIM, (h + 1) * HEAD_DIM)
        s = _nt_dot(q_ref[0, :, cols], k_ref[0, :, cols]) * (HEAD_DIM ** -0.5)
        m = jnp.max(s, axis=1, keepdims=True)
        e = jnp.exp(s - m)
        p = e * (1.0 / jnp.sum(e, axis=1, keepdims=True))
        o_ref[0, :, cols] = jnp.dot(p.astype(BF16), v_ref[0, :, cols],
                                    preferred_element_type=F32).astype(o_ref.dtype)


def ctx_attention(qkv, b, lc):
    qkv3 = qkv.reshape(b, lc, QKV_COLS)
    part = lambda k: pl.BlockSpec((1, lc, D_BRANCH), lambda i: (i, 0, k))
    out = pl.pallas_call(
        _ctx_attn_kernel,
        grid=(b,),
        in_specs=[part(0), part(1), part(2)],
        out_specs=part(0),
        out_shape=jax.ShapeDtypeStruct((b, lc, D_BRANCH), BF16),
        compiler_params=_params(1),
        name="ctx_attention",
    )(qkv3, qkv3, qkv3)
    return out.reshape(b * lc, D_BRANCH)


def _conv_kernel(h_ref, wx_ref, wb_ref, wc_ref, w_ref, o_ref, wbf_ref):
    @pl.when(pl.program_id(1) == 0)
    def _():
        for k, ref in enumerate((wx_ref, wb_ref, wc_ref)):
            wbf_ref[k] = ref[...].astype(BF16)

    h = h_ref[0]
    xc, bg, cg = (jnp.dot(h, wbf_ref[k], preferred_element_type=F32) for k in range(3))
    z = cg * xc
    n = z.shape[0]
    row = lax.broadcasted_iota(jnp.int32, z.shape, 0)
    z_prev = jnp.where(row == 0, 0.0, pltpu.roll(z, 1, 0))
    z_next = jnp.where(row == n - 1, 0.0, pltpu.roll(z, n - 1, 0))
    y = z_prev * w_ref[0, 0:1, :] + z * w_ref[0, 1:2, :] + z_next * w_ref[0, 2:3, :]
    o_ref[0] = (bg * y).astype(o_ref.dtype)


def short_conv(h, w_in, conv_w, layer, b, n, tc=256):
    d = h.shape[1]
    wcol = lambda col: pl.BlockSpec((None, d, tc), lambda j, i: (layer, 0, col // tc + j))
    out = pl.pallas_call(
        _conv_kernel,
        grid=(D_BRANCH // tc, b),
        in_specs=[pl.BlockSpec((1, n, d), lambda j, i: (i, 0, 0)),
                  wcol(COL_XC), wcol(COL_BG), wcol(COL_CG),
                  pl.BlockSpec((1, 3, tc), lambda j, i: (layer, 0, j))],
        out_specs=pl.BlockSpec((1, n, tc), lambda j, i: (i, 0, j)),
        out_shape=jax.ShapeDtypeStruct((b, n, D_BRANCH), BF16),
        scratch_shapes=[pltpu.VMEM((3, d, tc), BF16)],
        compiler_params=_params(2),
        name="short_conv",
    )(h.reshape(b, n, d), w_in, w_in, w_in, conv_w)
    return out.reshape(b * n, D_BRANCH)


def _gmlp_kernel(u_ref, v_ref, g_ref, ws_ref, bs_ref, o_ref):
    v = jax.nn.gelu(v_ref[...])
    mu = jnp.mean(v, axis=-1, keepdims=True)
    var = jnp.mean(jnp.square(v - mu), axis=-1, keepdims=True)
    vn = ((v - mu) * lax.rsqrt(var + NORM_EPS) * g_ref[...]).astype(BF16)
    for grp in range(GMLP_GROUPS):
        cols = slice(grp * 128, (grp + 1) * 128)
        ws = ws_ref[0, grp].astype(BF16)
        for ch in range(v.shape[0] // CHUNK):
            rows = slice(ch * CHUNK, (ch + 1) * CHUNK)
            s = jnp.dot(ws, vn[rows, cols], preferred_element_type=F32) + bs_ref[:, grp:grp + 1]
            o_ref[rows, cols] = (jax.nn.gelu(u_ref[rows, cols]) * s).astype(o_ref.dtype)


def chunk_gmlp(p, ln_g, w_s, b_s, layer, tm=4 * CHUNK):
    m = p.shape[0]
    return pl.pallas_call(
        _gmlp_kernel,
        grid=(m // tm,),
        in_specs=[
            pl.BlockSpec((tm, D_BRANCH), lambda i: (i, COL_U // D_BRANCH)),
            pl.BlockSpec((tm, D_BRANCH), lambda i: (i, COL_VS // D_BRANCH)),
            pl.BlockSpec((1, D_BRANCH), lambda i: (0, 0)),
            pl.BlockSpec((1, GMLP_GROUPS, CHUNK, CHUNK), lambda i: (layer, 0, 0, 0)),
            pl.BlockSpec((CHUNK, GMLP_GROUPS), lambda i: (0, 0)),
        ],
        out_specs=pl.BlockSpec((tm, D_BRANCH), lambda i: (i, 0)),
        out_shape=jax.ShapeDtypeStruct((m, D_BRANCH), BF16),
        compiler_params=_params(1),
        name="chunk_gmlp",
    )(p, p, ln_g[layer].reshape(1, D_BRANCH), w_s, b_s[layer].T)


def _merge_kernel(a_ref, c_ref, m_ref, g0_ref, g1_ref, g2_ref, w_ref, o_ref, wbf_ref):
    @pl.when(pl.program_id(1) == 0)
    def _():
        wbf_ref[...] = w_ref[...].astype(BF16)

    acc = None
    for i, (br, gl) in enumerate(((a_ref, g0_ref), (c_ref, g1_ref), (m_ref, g2_ref))):
        proj = jnp.dot(br[...], wbf_ref[i], preferred_element_type=F32)
        term = jax.nn.sigmoid(gl[...]) * proj
        acc = term if acc is None else acc + term
    o_ref[...] = acc.astype(o_ref.dtype)


def merge_branches(attn, conv, gm, p, w_branch, layer, tm=512, tn=1024):
    m = attn.shape[0]
    br = pl.BlockSpec((tm, D_BRANCH), lambda j, i: (i, 0))
    gate = lambda k: pl.BlockSpec((tm, tn), lambda j, i: (i, (COL_GATE + k * D_MODEL) // tn + j))
    return pl.pallas_call(
        _merge_kernel,
        grid=(D_MODEL // tn, m // tm),
        in_specs=[br, br, br, gate(0), gate(1), gate(2),
                  pl.BlockSpec((None, N_BRANCHES, D_BRANCH, tn), lambda j, i: (layer, 0, 0, j))],
        out_specs=pl.BlockSpec((tm, tn), lambda j, i: (i, j)),
        out_shape=jax.ShapeDtypeStruct((m, D_MODEL), BF16),
        scratch_shapes=[pltpu.VMEM((N_BRANCHES, D_BRANCH, tn), BF16)],
        compiler_params=_params(2),
        name="merge_branches",
    )(attn, conv, gm, p, p, p, w_branch)


def _out_proj_kernel(h_ref, w_ref, x_ref, mod_ref, o_ref, wbf_ref):
    @pl.when(pl.program_id(1) == 0)
    def _():
        wbf_ref[...] = w_ref[...].astype(BF16)

    y = jnp.dot(h_ref[...], wbf_ref[...], preferred_element_type=F32)
    o_ref[...] = x_ref[...] + mod_ref[0] * y


def out_proj_residual(h, w_out, layer, x, gate_vec, n, tm=1024, tn=1024):
    m, d = x.shape
    tm = min(tm, n)
    per_b = n // tm
    return pl.pallas_call(
        _out_proj_kernel,
        grid=(d // tn, m // tm),
        in_specs=[
            pl.BlockSpec((tm, d), lambda j, i: (i, 0)),
            pl.BlockSpec((None, d, tn), lambda j, i: (layer, 0, j)),
            pl.BlockSpec((tm, tn), lambda j, i: (i, j)),
            pl.BlockSpec((1, 1, tn), lambda j, i: (i // per_b, 0, j)),
        ],
        out_specs=pl.BlockSpec((tm, tn), lambda j, i: (i, j)),
        out_shape=jax.ShapeDtypeStruct((m, d), F32),
        scratch_shapes=[pltpu.VMEM((d, tn), BF16)],
        compiler_params=_params(2),
        name="out_proj_residual",
    )(h, w_out, x, gate_vec.reshape(-1, 1, d))


def _router_kernel(x_ref, g_ref, sh_ref, sc_ref, wr_ref, br_ref, h_ref, aff_ref):
    y = _rms(x_ref[0], g_ref[...])
    h = (y * (1 + sc_ref[0]) + sh_ref[0]).astype(BF16)
    h_ref[0] = h
    logits = _nt_dot(wr_ref[...].astype(BF16), h) + br_ref[...]
    mx = jnp.max(logits, axis=0, keepdims=True)
    e = jnp.exp(logits - mx)
    aff_ref[0] = e / jnp.sum(e, axis=0, keepdims=True)


def router(x, g, shift, scale, w_router, b_router, tm=1024):
    b, n, d = x.shape
    tm = min(tm, n)
    return pl.pallas_call(
        _router_kernel,
        grid=(b, n // tm),
        in_specs=[
            pl.BlockSpec((1, tm, d), lambda i, j: (i, j, 0)),
            pl.BlockSpec((1, d), lambda i, j: (0, 0)),
            pl.BlockSpec((1, 1, d), lambda i, j: (i, 0, 0)),
            pl.BlockSpec((1, 1, d), lambda i, j: (i, 0, 0)),
            pl.BlockSpec((N_EXPERTS, d), lambda i, j: (0, 0)),
            pl.BlockSpec((N_EXPERTS, 1), lambda i, j: (0, 0)),
        ],
        out_specs=[
            pl.BlockSpec((1, tm, d), lambda i, j: (i, j, 0)),
            pl.BlockSpec((1, N_EXPERTS, tm), lambda i, j: (i, 0, j)),
        ],
        out_shape=[jax.ShapeDtypeStruct((b, n, d), BF16),
                   jax.ShapeDtypeStruct((b, N_EXPERTS, n), F32)],
        compiler_params=_params(2),
        name="router",
    )(x, g.reshape(1, d), shift.reshape(b, 1, d), scale.reshape(b, 1, d),
      w_router.T, b_router.reshape(N_EXPERTS, 1))


def _select_kernel(aff_ref, tri_ref, slot_ref, *, cap):
    bits = pltpu.bitcast(aff_ref[...], jnp.int32)

    def step(i, t):
        cand = t | lax.shift_left(jnp.int32(1), 30 - i)
        cnt = jnp.sum(jnp.where(bits >= cand, 1.0, 0.0), axis=1, keepdims=True)
        return jnp.where(cnt >= cap, cand, t)

    t = lax.fori_loop(0, 31, step, jnp.zeros((bits.shape[0], 1), jnp.int32))
    gt = bits > t
    eq = bits == t
    need = cap - jnp.sum(jnp.where(gt, 1.0, 0.0), axis=1, keepdims=True)
    tri = tri_ref[...]
    eq_rank = jnp.dot(jnp.where(eq, 1.0, 0.0).astype(BF16), tri, preferred_element_type=F32)
    sel = gt | (eq & (eq_rank < need))
    slot = jnp.dot(jnp.where(sel, 1.0, 0.0).astype(BF16), tri, preferred_element_type=F32)
    slot_ref[...] = jnp.where(sel, slot.astype(jnp.int32), -1)


def select_tokens(aff, cap):
    b, e, n = aff.shape
    tri = jnp.triu(jnp.ones((n, n), BF16), k=1)
    rows = pl.BlockSpec((b * e, n), lambda i: (0, 0))
    return pl.pallas_call(
        functools.partial(_select_kernel, cap=cap),
        grid=(1,),
        in_specs=[rows, pl.BlockSpec((n, n), lambda i: (0, 0))],
        out_specs=rows,
        out_shape=jax.ShapeDtypeStruct((b * e, n), jnp.int32),
        compiler_params=_params(1),
        name="select_tokens",
    )(aff.reshape(b * e, n), tri).reshape(b, e, n)


def _onehot(slot, cap):
    hit = lax.broadcasted_iota(jnp.int32, (cap, slot.shape[1]), 0) == slot
    return hit, jnp.where(hit, 1.0, 0.0)


def _gather_kernel(h_ref, slot_ref, aff_ref, xs_ref, gate_ref, *, cap):
    for k in range(slot_ref.shape[0]):
        hit, onehot = _onehot(slot_ref[k], cap)
        xs_ref[k] = jnp.dot(onehot.astype(BF16), h_ref[0], preferred_element_type=F32).astype(BF16)
        gate_ref[k] = jnp.sum(jnp.where(hit, aff_ref[k], 0.0), axis=1, keepdims=True)


def gather_tokens(h, slot, aff, cap):
    b, n, d = h.shape
    e = N_EXPERTS
    eps = max(1, min(e, 1024 // cap))
    row = pl.BlockSpec((eps, 1, n), lambda i, j: (i * (e // eps) + j, 0, 0))
    return pl.pallas_call(
        functools.partial(_gather_kernel, cap=cap),
        grid=(b, e // eps),
        in_specs=[pl.BlockSpec((1, n, d), lambda i, j: (i, 0, 0)), row, row],
        out_specs=[
            pl.BlockSpec((eps, cap, d), lambda i, j: (j, i, 0)),
            pl.BlockSpec((eps, cap, 1), lambda i, j: (j, i, 0)),
        ],
        out_shape=[jax.ShapeDtypeStruct((e, b * cap, d), BF16),
                   jax.ShapeDtypeStruct((e, b * cap, 1), F32)],
        compiler_params=_params(2),
        name="gather_tokens",
    )(h, slot.reshape(b * e, 1, n), aff.reshape(b * e, 1, n))


def _ffn_kernel(*refs, n_sets, n_f):
    xs_refs, gate_refs = refs[0:2 * n_sets:2], refs[1:2 * n_sets:2]
    wg_ref, wu_ref, wd_ref = refs[2 * n_sets:2 * n_sets + 3]
    y_refs = refs[2 * n_sets + 3:3 * n_sets + 3]
    act_refs = refs[3 * n_sets + 3:]
    s = pl.program_id(1)
    tf = wg_ref.shape[1]

    @pl.when(s < n_f)
    def _():
        wg = wg_ref[...].astype(BF16)
        wu = wu_ref[...].astype(BF16)
        for xs_ref, act_ref in zip(xs_refs, act_refs):
            xs = xs_ref[0]
            hg = jnp.dot(xs, wg, preferred_element_type=F32)
            hu = jnp.dot(xs, wu, preferred_element_type=F32)
            act_ref[s] = (jax.nn.silu(hg) * hu).astype(BF16)

    @pl.when(s >= n_f)
    def _():
        wd = wd_ref[...].astype(BF16)
        for gate_ref, y_ref, act_ref in zip(gate_refs, y_refs, act_refs):
            y = None
            for f in range(n_f):
                part = jnp.dot(act_ref[f], wd[f * tf:(f + 1) * tf, :], preferred_element_type=F32)
                y = part if y is None else y + part
            y_ref[0] = (y * gate_ref[0]).astype(y_ref.dtype)


def expert_ffn(sets, w_gate, w_up, w_down, layer, tf=512, tn=512):
    d, ff = w_gate.shape[2], w_gate.shape[3]
    n_f, n_d = ff // tf, d // tn
    d_idx = lambda s: jnp.maximum(s - n_f, 0)

    def up_tile(i, s):
        ahead = s >= n_f
        return (layer, jnp.where(ahead, jnp.minimum(i + 1, N_EXPERTS - 1), i), 0,
                jnp.where(ahead, 0, jnp.minimum(s, n_f - 1)))

    def down_tile(i, s):
        behind = s < n_f
        return (layer, jnp.where(behind, jnp.maximum(i - 1, 0), i), 0, jnp.where(behind, n_d - 1, s - n_f))
    in_specs, operands, out_specs, out_shape, scratch = [], [], [], [], []
    for xs, gates in sets:
        r = xs.shape[1]
        in_specs += [pl.BlockSpec((1, r, d), lambda i, s: (i, 0, 0)),
                     pl.BlockSpec((1, r, 1), lambda i, s: (i, 0, 0))]
        operands += [xs, gates]
        out_specs.append(pl.BlockSpec((1, r, tn), lambda i, s: (i, 0, d_idx(s))))
        out_shape.append(jax.ShapeDtypeStruct(xs.shape, BF16))
        scratch.append(pltpu.VMEM((n_f, r, tf), BF16))
    in_specs += [
        pl.BlockSpec((None, None, d, tf), up_tile),
        pl.BlockSpec((None, None, d, tf), up_tile),
        pl.BlockSpec((None, None, ff, tn), down_tile),
    ]
    return pl.pallas_call(
        functools.partial(_ffn_kernel, n_sets=len(sets), n_f=n_f),
        grid=(N_EXPERTS, n_f + n_d),
        in_specs=in_specs,
        out_specs=out_specs,
        out_shape=out_shape,
        scratch_shapes=scratch,
        compiler_params=_params(2),
        name="expert_ffn",
    )(*operands, w_gate, w_up, w_down)


def _combine_kernel(y_ref, slot_ref, x_ref, mod_ref, o_ref, onehot_ref, *, cap):
    @pl.when(pl.program_id(1) == 0)
    def _():
        for e in range(N_EXPERTS):
            onehot_ref[e * cap:(e + 1) * cap, :] = _onehot(slot_ref[e], cap)[1].astype(BF16)

    y = y_ref[...].reshape(N_EXPERTS * cap, y_ref.shape[2])
    moe = lax.dot_general(onehot_ref[...], y, (((0,), (0,)), ((), ())), preferred_element_type=F32)
    o_ref[0] = x_ref[0] + mod_ref[0] * moe


def combine(y, slot, x, gate_vec, cap, tn=512):
    b, n, d = x.shape
    e = N_EXPERTS
    return pl.pallas_call(
        functools.partial(_combine_kernel, cap=cap),
        grid=(b, d // tn),
        in_specs=[
            pl.BlockSpec((e, cap, tn), lambda i, j: (0, i, j)),
            pl.BlockSpec((e, 1, n), lambda i, j: (i, 0, 0)),
            pl.BlockSpec((1, n, tn), lambda i, j: (i, 0, j)),
            pl.BlockSpec((1, 1, tn), lambda i, j: (i, 0, j)),
        ],
        out_specs=pl.BlockSpec((1, n, tn), lambda i, j: (i, 0, j)),
        out_shape=jax.ShapeDtypeStruct((b, n, d), F32),
        scratch_shapes=[pltpu.VMEM((e * cap, n), BF16)],
        compiler_params=_params(2),
        name="combine",
    )(y, slot.reshape(b * e, 1, n), x, gate_vec.reshape(b, 1, d))


def moe_route(x, g, mod, w_router, b_router):
    n = x.shape[1]
    cap = CAPACITY_FACTOR * n // N_EXPERTS
    h, aff = router(x, g, mod[:, 3], mod[:, 4], w_router, b_router)
    slot = select_tokens(aff, cap)
    xs, gates = gather_tokens(h, slot, aff, cap)
    return slot, cap, (xs, gates)


def mixer_block(x, h, attn, mod, n, layer, w_in, conv_w, ln_g, w_s, b_s, w_branch, w_out):
    b = x.shape[0] // n
    p = matmul(h, w_in, layer, P_COL0, w_in.shape[2] - P_COL0)
    conv = short_conv(h, w_in, conv_w, layer, b, n)
    gm = chunk_gmlp(p, ln_g, w_s, b_s, layer)
    merged = merge_branches(attn, conv, gm, p, w_branch, layer)
    return out_proj_residual(merged, w_out, layer, x, mod[:, 2], n)


def kernel(x, c, ctx, c_ctx, w_mod, b_mod, norm1_g, w_in, na_rpb, conv_w, gmlp_ln_g, w_spatial, b_spatial,
           w_branch, w_out, norm2_g, w_router, b_router, w_e_gate, w_e_up, w_e_down, final_g):
    b, n, d = x.shape
    lc = ctx.shape[1]
    cvec = jnp.concatenate([c, c_ctx[None], jnp.zeros((8 - b - 1, d), F32)], axis=0)
    modv = modvec(cvec, w_mod, b_mod).reshape(DEPTH, 8, 6, d)
    cos_t, sin_t = rope_tables(n)
    bias = na_bias(na_rpb, n // GRID_W)
    mix_w = (w_in, conv_w, gmlp_ln_g, w_spatial, b_spatial, w_branch, w_out)
    ctx_s = ctx
    for layer in range(DEPTH):
        last = layer == DEPTH - 1
        mod = modv[layer, :b]
        mod_c = jnp.broadcast_to(modv[layer, b][None], (b, 6, d))
        route_w = (w_router[layer], b_router[layer])

        hc = norm_mod(ctx_s, norm1_g[layer], mod_c[:, 0], mod_c[:, 1]).reshape(b * lc, d)
        if last:
            kvc = qkv_proj(hc, w_in, layer, lc, cos_t, sin_t, rope=False, first_tile=1, n_tiles=2)
            kc_col = 0
        else:
            kvc = qkv_proj(hc, w_in, layer, lc, cos_t, sin_t, rope=False)
            kc_col = D_BRANCH
            attn_c = ctx_attention(kvc, b, lc)
            ctx_mid = mixer_block(ctx_s.reshape(b * lc, d), hc, attn_c, mod_c, lc, layer, *mix_w)
            ctx_mid = ctx_mid.reshape(b, lc, d)

        h = norm_mod(x, norm1_g[layer], mod[:, 0], mod[:, 1]).reshape(b * n, d)
        qkv = qkv_proj(h, w_in, layer, n, cos_t, sin_t, rope=True)
        attn = na_attention(qkv, kvc, kc_col, bias, layer, b, n)
        x = mixer_block(x.reshape(b * n, d), h, attn, mod, n, layer, *mix_w).reshape(b, n, d)

        slot, cap, rows = moe_route(x, norm2_g[layer], mod, *route_w)
        if last:
            (y,) = expert_ffn([rows], w_e_gate, w_e_up, w_e_down, layer)
        else:
            slot_c, cap_c, rows_c = moe_route(ctx_mid, norm2_g[layer], mod_c, *route_w)
            y, y_c = expert_ffn([rows, rows_c], w_e_gate, w_e_up, w_e_down, layer)
            ctx_s = combine(y_c, slot_c, ctx_mid, mod_c[:, 5], cap_c)
        x = combine(y, slot, x, mod[:, 5], cap)
    return final_norm(x, final_g)
```

```python
import functools

import numpy as np
import jax
import jax.numpy as jnp
from jax import lax
from jax.experimental import pallas as pl
from jax.experimental.pallas import tpu as pltpu
from jax.experimental.pallas import tpu_sc as plsc

D_MODEL = 2048
DEPTH = 2
GRID_W = 64
D_BRANCH = D_MODEL // 2
N_BRANCHES = 3
HEAD_DIM = 128
N_HEADS = D_BRANCH // HEAD_DIM
NA_KH = 8
NA_KW = 16
ROPE_THETA = 10000.0
ROPE_AXIS_DIM = HEAD_DIM // 2
CHUNK = 128
GMLP_GROUPS = D_BRANCH // 128
N_EXPERTS = 16
CAPACITY_FACTOR = 2
NORM_EPS = 1e-6
NEG_INF = -1e30
LOG2_E = float(np.log2(np.e))

QKV_COLS = 3 * D_BRANCH
COL_XC, COL_BG, COL_CG = (QKV_COLS + i * D_BRANCH for i in range(3))
P_COL0 = QKV_COLS + 3 * D_BRANCH
COL_U, COL_VS, COL_GATE = 0, D_BRANCH, 2 * D_BRANCH

NA_Q_ROWS = 4
NA_BAND_ROWS = 12
NA_TQ = NA_Q_ROWS * GRID_W
NA_TK = NA_BAND_ROWS * GRID_W
NA_HEADS_PER_STEP = 4
NA_HB = NA_HEADS_PER_STEP * HEAD_DIM
NA_SCORE_LEAD = 2

V7X_VMEM_LIMIT = 56 * 1024 * 1024
V7X_SC_CORES, V7X_SC_SUBCORES, V7X_SC_LANES = 2, 16, 16
SC_GATHER_ROWS = 64

BF16 = jnp.bfloat16
F32 = jnp.float32


def _params(n_axes, vmem=V7X_VMEM_LIMIT):
    return pltpu.CompilerParams(dimension_semantics=("arbitrary",) * n_axes, vmem_limit_bytes=vmem)


def _modvec_kernel(c_ref, w_ref, b_ref, o_ref):
    s = jax.nn.silu(c_ref[...]).astype(BF16)
    o_ref[0] = jnp.dot(s, w_ref[0].astype(BF16), preferred_element_type=F32) + b_ref[0]


def modvec(cvec, w_mod, b_mod, tn=1024):
    nl, d, n6 = w_mod.shape
    return pl.pallas_call(
        _modvec_kernel,
        grid=(nl, n6 // tn),
        in_specs=[
            pl.BlockSpec((8, d), lambda l, j: (0, 0)),
            pl.BlockSpec((1, d, tn), lambda l, j: (l, 0, j)),
            pl.BlockSpec((1, 1, tn), lambda l, j: (l, 0, j)),
        ],
        out_specs=pl.BlockSpec((1, 8, tn), lambda l, j: (l, 0, j)),
        out_shape=jax.ShapeDtypeStruct((nl, 8, n6), F32),
        compiler_params=_params(2),
        name="modvec",
    )(cvec, w_mod, b_mod.reshape(nl, 1, n6))


def _rms(x, g):
    ms = jnp.mean(x * x, axis=-1, keepdims=True)
    return x * lax.rsqrt(ms + NORM_EPS) * g


def _norm_mod_kernel(x_ref, g_ref, sh_ref, sc_ref, o_ref):
    y = _rms(x_ref[0], g_ref[...])
    o_ref[0] = (y * (1 + sc_ref[0]) + sh_ref[0]).astype(o_ref.dtype)


def _norm_kernel(x_ref, g_ref, o_ref):
    o_ref[0] = _rms(x_ref[0], g_ref[...]).astype(o_ref.dtype)


def norm_mod(x, g, shift, scale, out_dtype=BF16, tm=1024):
    b, n, d = x.shape
    tm = min(tm, n)
    return pl.pallas_call(
        _norm_mod_kernel,
        grid=(b, n // tm),
        in_specs=[
            pl.BlockSpec((1, tm, d), lambda i, j: (i, j, 0)),
            pl.BlockSpec((1, d), lambda i, j: (0, 0)),
            pl.BlockSpec((1, 1, d), lambda i, j: (i, 0, 0)),
            pl.BlockSpec((1, 1, d), lambda i, j: (i, 0, 0)),
        ],
        out_specs=pl.BlockSpec((1, tm, d), lambda i, j: (i, j, 0)),
        out_shape=jax.ShapeDtypeStruct((b, n, d), out_dtype),
        compiler_params=_params(2),
        name="norm_mod",
    )(x, g.reshape(1, d), shift.reshape(b, 1, d), scale.reshape(b, 1, d))


def final_norm(x, g, tm=1024):
    b, n, d = x.shape
    return pl.pallas_call(
        _norm_kernel,
        grid=(b, n // tm),
        in_specs=[
            pl.BlockSpec((1, tm, d), lambda i, j: (i, j, 0)),
            pl.BlockSpec((1, d), lambda i, j: (0, 0)),
        ],
        out_specs=pl.BlockSpec((1, tm, d), lambda i, j: (i, j, 0)),
        out_shape=jax.ShapeDtypeStruct((b, n, d), F32),
        compiler_params=_params(2),
        name="final_norm",
    )(x, g.reshape(1, d))


def _mm_kernel(x_ref, w_ref, o_ref, wbf_ref):
    @pl.when(pl.program_id(1) == 0)
    def _():
        wbf_ref[...] = w_ref[...].astype(BF16)

    o_ref[...] = jnp.dot(x_ref[...], wbf_ref[...], preferred_element_type=F32).astype(o_ref.dtype)


def matmul(x, w, layer, col_off, n_cols, out_dtype=F32, tm=1024, tn=1024):
    m, k = x.shape
    tm = min(tm, m)
    off = col_off // tn
    return pl.pallas_call(
        _mm_kernel,
        grid=(n_cols // tn, m // tm),
        in_specs=[
            pl.BlockSpec((tm, k), lambda j, i: (i, 0)),
            pl.BlockSpec((None, k, tn), lambda j, i: (layer, 0, off + j)),
        ],
        out_specs=pl.BlockSpec((tm, tn), lambda j, i: (i, j)),
        out_shape=jax.ShapeDtypeStruct((m, n_cols), out_dtype),
        scratch_shapes=[pltpu.VMEM((k, tn), BF16)],
        compiler_params=_params(2),
        name="matmul",
    )(x, w)


def _qkv_kernel(x_ref, w_ref, cos_ref, sin_ref, o_ref, wbf_ref, *, rope):
    @pl.when(pl.program_id(1) == 0)
    def _():
        wbf_ref[...] = w_ref[...].astype(BF16)

    def plain():
        o_ref[...] = jnp.dot(x_ref[...], wbf_ref[...], preferred_element_type=F32).astype(BF16)

    if not rope:
        plain()
        return
    c = pl.program_id(0)

    @pl.when(c < 2)
    def _():
        cs = cos_ref[...]
        sn = sin_ref[...]
        lane = lax.broadcasted_iota(jnp.int32, cs.shape, 1)
        first_half = (lane % ROPE_AXIS_DIM) < (ROPE_AXIS_DIM // 2)
        for pair in range(N_HEADS // 2):
            cols = slice(pair * 2 * HEAD_DIM, (pair + 1) * 2 * HEAD_DIM)
            acc = jnp.dot(x_ref[...], wbf_ref[:, cols], preferred_element_type=F32)
            for h in range(2):
                t = acc[:, h * HEAD_DIM:(h + 1) * HEAD_DIM]
                partner = jnp.where(first_half,
                                    pltpu.roll(t, HEAD_DIM - ROPE_AXIS_DIM // 2, 1),
                                    pltpu.roll(t, ROPE_AXIS_DIM // 2, 1))
                lo = (pair * 2 + h) * HEAD_DIM
                o_ref[:, lo:lo + HEAD_DIM] = (t * cs + partner * sn).astype(BF16)

    @pl.when(c == 2)
    def _():
        plain()


def rope_tables(n):
    pos = jnp.arange(n)
    rc = jnp.stack([pos // GRID_W, pos % GRID_W], axis=-1).astype(F32)
    inv = ROPE_THETA ** (-jnp.arange(0, ROPE_AXIS_DIM, 2, dtype=F32) / ROPE_AXIS_DIM)
    ang = rc[:, :, None] * inv
    cos, sin = jnp.cos(ang), jnp.sin(ang)
    cos_t = jnp.concatenate([cos, cos], axis=-1).reshape(n, HEAD_DIM)
    sin_t = jnp.concatenate([-sin, sin], axis=-1).reshape(n, HEAD_DIM)
    return cos_t, sin_t


def qkv_proj(x, w, layer, n, cos_t, sin_t, rope, first_tile=0, n_tiles=3, tm=1024):
    m, k = x.shape
    tm = min(tm, n if rope else m)
    nt = n // tm if rope else 1
    tab = pl.BlockSpec((tm, HEAD_DIM), lambda c, i: (i % nt, 0))
    return pl.pallas_call(
        functools.partial(_qkv_kernel, rope=rope),
        grid=(n_tiles, m // tm),
        in_specs=[
            pl.BlockSpec((tm, k), lambda c, i: (i, 0)),
            pl.BlockSpec((None, k, D_BRANCH), lambda c, i: (layer, 0, first_tile + c)),
            tab, tab,
        ],
        out_specs=pl.BlockSpec((tm, D_BRANCH), lambda c, i: (i, c)),
        out_shape=jax.ShapeDtypeStruct((m, n_tiles * D_BRANCH), BF16),
        scratch_shapes=[pltpu.VMEM((k, D_BRANCH), BF16)],
        compiler_params=_params(2),
        name="qkv_proj",
    )(x, w, cos_t, sin_t)


def _nt_dot(a, b):
    return lax.dot_general(a, b, (((1,), (1,)), ((), ())), preferred_element_type=F32)


def _na_pattern_rows(g, rows):
    band_row = int(np.clip(g * NA_Q_ROWS - NA_KH // 2, 0, rows - NA_BAND_ROWS))
    qr = g * NA_Q_ROWS + np.arange(NA_Q_ROWS)
    rs = np.clip(qr - NA_KH // 2, 0, rows - NA_KH)
    return band_row, qr, rs


def _na_bias_kernel(rpb_ref, o_ref, *, rows):
    shape = (GRID_W, 2 * GRID_W)
    qc = lax.broadcasted_iota(jnp.int32, shape, 0)
    lane = lax.broadcasted_iota(jnp.int32, shape, 1)
    low = lane < GRID_W
    kc = lane % GRID_W
    ws = jnp.clip(qc - NA_KW // 2, 0, GRID_W - NA_KW)
    col_ok = (kc >= ws) & (kc < ws + NA_KW)
    neg = jnp.full(shape, NEG_INF, F32)

    tiles = []
    for dr in range(2 * NA_KH - 1):
        t = jnp.broadcast_to(rpb_ref[0, 0, dr:dr + 1, :], shape)
        t = pltpu.roll(t, 2 * GRID_W - (NA_KW - 1), 1)
        for bit in range(6):
            t = jnp.where(((qc >> bit) & 1) == 1, pltpu.roll(t, 1 << bit, 1), t)
        t = jnp.where(low, t, pltpu.roll(t, GRID_W, 1))
        tiles.append(jnp.where(col_ok, t * LOG2_E, NEG_INF))

    n_groups = rows // NA_Q_ROWS
    for p, g in enumerate((0, 1, n_groups - 1)):
        band_row, qr, rs = _na_pattern_rows(g, rows)
        for qi in range(NA_Q_ROWS):
            def half(krl):
                kr = band_row + krl
                if rs[qi] <= kr < rs[qi] + NA_KH:
                    return tiles[kr - qr[qi] + NA_KH - 1]
                return neg
            for pair in range(NA_BAND_ROWS // 2):
                blk = jnp.where(low, half(2 * pair), half(2 * pair + 1))
                o_ref[0, p, 0, qi * GRID_W:(qi + 1) * GRID_W, pair * 128:(pair + 1) * 128] = blk


def na_bias(na_rpb, rows):
    nl, nh, ndr, ndc = na_rpb.shape
    band1, qr1, rs1 = _na_pattern_rows(1, rows)
    for g in range(2, rows // NA_Q_ROWS - 1):
        band, qr, rs = _na_pattern_rows(g, rows)
        assert (qr - band == qr1 - band1).all() and (rs - band == rs1 - band1).all()
    rpb = jnp.pad(na_rpb, ((0, 0), (0, 0), (0, 16 - ndr), (0, 128 - ndc)))
    return pl.pallas_call(
        functools.partial(_na_bias_kernel, rows=rows),
        grid=(nl, nh),
        in_specs=[pl.BlockSpec((1, 1, 16, 128), lambda l, h: (l, h, 0, 0))],
        out_specs=pl.BlockSpec((1, 3, 1, NA_TQ, NA_TK), lambda l, h: (l, 0, h, 0, 0)),
        out_shape=jax.ShapeDtypeStruct((nl, 3, nh, NA_TQ, NA_TK), F32),
        compiler_params=_params(2),
        name="na_bias",
    )(rpb)


def _na_kernel(q_ref, k_ref, v_ref, kc_ref, vc_ref, bias_ref, o_ref):
    g = pl.program_id(2)
    n_groups = k_ref.shape[1] // NA_TQ
    band_row = jnp.clip(g * NA_Q_ROWS - NA_KH // 2, 0, n_groups * NA_Q_ROWS - NA_BAND_ROWS)
    start = pl.multiple_of(band_row * GRID_W, NA_TQ)
    pat = jnp.where(g == 0, 0, jnp.where(g == n_groups - 1, 2, 1))
    scale = HEAD_DIM ** -0.5 * LOG2_E

    def scores(hh):
        cols = slice(hh * HEAD_DIM, (hh + 1) * HEAD_DIM)
        q = q_ref[0, :, cols]
        return _nt_dot(q, k_ref[0, pl.ds(start, NA_TK), cols]), _nt_dot(q, kc_ref[0, :, cols])

    def finish(hh, qk_loc, qk_ctx):
        cols = slice(hh * HEAD_DIM, (hh + 1) * HEAD_DIM)
        s_loc = qk_loc * scale + bias_ref[0, pat, hh]
        s_ctx = qk_ctx * scale
        m = jnp.maximum(jnp.max(s_loc, axis=1, keepdims=True), jnp.max(s_ctx, axis=1, keepdims=True))
        e_loc = jnp.exp2(s_loc - m)
        e_ctx = jnp.exp2(s_ctx - m)
        inv = 1.0 / (jnp.sum(e_loc, axis=1, keepdims=True) + jnp.sum(e_ctx, axis=1, keepdims=True))
        o = jnp.dot(e_loc.astype(BF16), v_ref[0, pl.ds(start, NA_TK), cols], preferred_element_type=F32)
        o = o + jnp.dot(e_ctx.astype(BF16), vc_ref[0, :, cols], preferred_element_type=F32)
        o_ref[0, :, cols] = (o * inv).astype(o_ref.dtype)

    qk = {}
    for hh in range(NA_HEADS_PER_STEP + NA_SCORE_LEAD):
        if hh < NA_HEADS_PER_STEP:
            qk[hh] = scores(hh)
        if hh >= NA_SCORE_LEAD:
            finish(hh - NA_SCORE_LEAD, *qk.pop(hh - NA_SCORE_LEAD))


def na_attention(qkv, kvc, kc_col, bias, layer, b, n):
    lc = kvc.shape[0] // b
    qkv3 = qkv.reshape(b, n, QKV_COLS)
    kvc3 = kvc.reshape(b, lc, kvc.shape[1])
    hbs = D_BRANCH // NA_HB
    kcb = kc_col // NA_HB
    out = pl.pallas_call(
        _na_kernel,
        grid=(hbs, b, n // NA_TQ),
        in_specs=[
            pl.BlockSpec((1, NA_TQ, NA_HB), lambda h, i, g: (i, g, h)),
            pl.BlockSpec((1, n, NA_HB), lambda h, i, g: (i, 0, hbs + h)),
            pl.BlockSpec((1, n, NA_HB), lambda h, i, g: (i, 0, 2 * hbs + h)),
            pl.BlockSpec((1, lc, NA_HB), lambda h, i, g: (i, 0, kcb + h)),
            pl.BlockSpec((1, lc, NA_HB), lambda h, i, g: (i, 0, kcb + hbs + h)),
            pl.BlockSpec((1, 3, NA_HEADS_PER_STEP, NA_TQ, NA_TK), lambda h, i, g: (layer, 0, h, 0, 0)),
        ],
        out_specs=pl.BlockSpec((1, NA_TQ, NA_HB), lambda h, i, g: (i, g, h)),
        out_shape=jax.ShapeDtypeStruct((b, n, D_BRANCH), BF16),
        compiler_params=_params(3),
        name="na_attention",
    )(qkv3, qkv3, qkv3, kvc3, kvc3, bias)
    return out.reshape(b * n, D_BRANCH)


def _ctx_attn_kernel(q_ref, k_ref, v_ref, o_ref):
    for h in range(N_HEADS):
        cols = slice(h * HEAD_DIM, (h + 1) * HEAD_DIM)
        s = _nt_dot(q_ref[0, :, cols], k_ref[0, :, cols]) * (HEAD_DIM ** -0.5)
        m = jnp.max(s, axis=1, keepdims=True)
        e = jnp.exp(s - m)
        p = e * (1.0 / jnp.sum(e, axis=1, keepdims=True))
        o_ref[0, :, cols] = jnp.dot(p.astype(BF16), v_ref[0, :, cols],
                                    preferred_element_type=F32).astype(o_ref.dtype)


def ctx_attention(qkv, b, lc):
    qkv3 = qkv.reshape(b, lc, QKV_COLS)
    part = lambda k: pl.BlockSpec((1, lc, D_BRANCH), lambda i: (i, 0, k))
    out = pl.pallas_call(
        _ctx_attn_kernel,
        grid=(b,),
        in_specs=[part(0), part(1), part(2)],
        out_specs=part(0),
        out_shape=jax.ShapeDtypeStruct((b, lc, D_BRANCH), BF16),
        compiler_params=_params(1),
        name="ctx_attention",
    )(qkv3, qkv3, qkv3)
    return out.reshape(b * lc, D_BRANCH)


def _conv_kernel(h_ref, wx_ref, wb_ref, wc_ref, w_ref, o_ref, wbf_ref):
    @pl.when(pl.program_id(1) == 0)
    def _():
        for k, ref in enumerate((wx_ref, wb_ref, wc_ref)):
            wbf_ref[k] = ref[...].astype(BF16)

    h = h_ref[0]
    xc, bg, cg = (jnp.dot(h, wbf_ref[k], preferred_element_type=F32) for k in range(3))
    z = cg * xc
    n = z.shape[0]
    row = lax.broadcasted_iota(jnp.int32, z.shape, 0)
    z_prev = jnp.where(row == 0, 0.0, pltpu.roll(z, 1, 0))
    z_next = jnp.where(row == n - 1, 0.0, pltpu.roll(z, n - 1, 0))
    y = z_prev * w_ref[0, 0:1, :] + z * w_ref[0, 1:2, :] + z_next * w_ref[0, 2:3, :]
    o_ref[0] = (bg * y).astype(o_ref.dtype)


def short_conv(h, w_in, conv_w, layer, b, n, tc=256):
    d = h.shape[1]
    wcol = lambda col: pl.BlockSpec((None, d, tc), lambda j, i: (layer, 0, col // tc + j))
    out = pl.pallas_call(
        _conv_kernel,
        grid=(D_BRANCH // tc, b),
        in_specs=[pl.BlockSpec((1, n, d), lambda j, i: (i, 0, 0)),
                  wcol(COL_XC), wcol(COL_BG), wcol(COL_CG),
                  pl.BlockSpec((1, 3, tc), lambda j, i: (layer, 0, j))],
        out_specs=pl.BlockSpec((1, n, tc), lambda j, i: (i, 0, j)),
        out_shape=jax.ShapeDtypeStruct((b, n, D_BRANCH), BF16),
        scratch_shapes=[pltpu.VMEM((3, d, tc), BF16)],
        compiler_params=_params(2),
        name="short_conv",
    )(h.reshape(b, n, d), w_in, w_in, w_in, conv_w)
    return out.reshape(b * n, D_BRANCH)


def _gmlp_kernel(u_ref, v_ref, g_ref, ws_ref, bs_ref, o_ref):
    v = jax.nn.gelu(v_ref[...])
    mu = jnp.mean(v, axis=-1, keepdims=True)
    var = jnp.mean(jnp.square(v - mu), axis=-1, keepdims=True)
    vn = ((v - mu) * lax.rsqrt(var + NORM_EPS) * g_ref[...]).astype(BF16)
    for grp in range(GMLP_GROUPS):
        cols = slice(grp * 128, (grp + 1) * 128)
        ws = ws_ref[0, grp].astype(BF16)
        for ch in range(v.shape[0] // CHUNK):
            rows = slice(ch * CHUNK, (ch + 1) * CHUNK)
            s = jnp.dot(ws, vn[rows, cols], preferred_element_type=F32) + bs_ref[:, grp:grp + 1]
            o_ref[rows, cols] = (jax.nn.gelu(u_ref[rows, cols]) * s).astype(o_ref.dtype)


def chunk_gmlp(p, ln_g, w_s, b_s, layer, tm=4 * CHUNK):
    m = p.shape[0]
    return pl.pallas_call(
        _gmlp_kernel,
        grid=(m // tm,),
        in_specs=[
            pl.BlockSpec((tm, D_BRANCH), lambda i: (i, COL_U // D_BRANCH)),
            pl.BlockSpec((tm, D_BRANCH), lambda i: (i, COL_VS // D_BRANCH)),
            pl.BlockSpec((1, D_BRANCH), lambda i: (0, 0)),
            pl.BlockSpec((1, GMLP_GROUPS, CHUNK, CHUNK), lambda i: (layer, 0, 0, 0)),
            pl.BlockSpec((CHUNK, GMLP_GROUPS), lambda i: (0, 0)),
        ],
        out_specs=pl.BlockSpec((tm, D_BRANCH), lambda i: (i, 0)),
        out_shape=jax.ShapeDtypeStruct((m, D_BRANCH), BF16),
        compiler_params=_params(1),
        name="chunk_gmlp",
    )(p, p, ln_g[layer].reshape(1, D_BRANCH), w_s, b_s[layer].T)


def _merge_kernel(a_ref, c_ref, m_ref, g0_ref, g1_ref, g2_ref, w_ref, o_ref, wbf_ref):
    @pl.when(pl.program_id(1) == 0)
    def _():
        wbf_ref[...] = w_ref[...].astype(BF16)

    acc = None
    for i, (br, gl) in enumerate(((a_ref, g0_ref), (c_ref, g1_ref), (m_ref, g2_ref))):
        proj = jnp.dot(br[...], wbf_ref[i], preferred_element_type=F32)
        term = jax.nn.sigmoid(gl[...]) * proj
        acc = term if acc is None else acc + term
    o_ref[...] = acc.astype(o_ref.dtype)


def merge_branches(attn, conv, gm, p, w_branch, layer, tm=512, tn=1024):
    m = attn.shape[0]
    br = pl.BlockSpec((tm, D_BRANCH), lambda j, i: (i, 0))
    gate = lambda k: pl.BlockSpec((tm, tn), lambda j, i: (i, (COL_GATE + k * D_MODEL) // tn + j))
    return pl.pallas_call(
        _merge_kernel,
        grid=(D_MODEL // tn, m // tm),
        in_specs=[br, br, br, gate(0), gate(1), gate(2),
                  pl.BlockSpec((None, N_BRANCHES, D_BRANCH, tn), lambda j, i: (layer, 0, 0, j))],
        out_specs=pl.BlockSpec((tm, tn), lambda j, i: (i, j)),
        out_shape=jax.ShapeDtypeStruct((m, D_MODEL), BF16),
        scratch_shapes=[pltpu.VMEM((N_BRANCHES, D_BRANCH, tn), BF16)],
        compiler_params=_params(2),
        name="merge_branches",
    )(attn, conv, gm, p, p, p, w_branch)


def _out_proj_kernel(h_ref, w_ref, x_ref, mod_ref, o_ref, wbf_ref):
    @pl.when(pl.program_id(1) == 0)
    def _():
        wbf_ref[...] = w_ref[...].astype(BF16)

    y = jnp.dot(h_ref[...], wbf_ref[...], preferred_element_type=F32)
    o_ref[...] = x_ref[...] + mod_ref[0] * y


def out_proj_residual(h, w_out, layer, x, gate_vec, n, tm=1024, tn=1024):
    m, d = x.shape
    tm = min(tm, n)
    per_b = n // tm
    return pl.pallas_call(
        _out_proj_kernel,
        grid=(d // tn, m // tm),
        in_specs=[
            pl.BlockSpec((tm, d), lambda j, i: (i, 0)),
            pl.BlockSpec((None, d, tn), lambda j, i: (layer, 0, j)),
            pl.BlockSpec((tm, tn), lambda j, i: (i, j)),
            pl.BlockSpec((1, 1, tn), lambda j, i: (i // per_b, 0, j)),
        ],
        out_specs=pl.BlockSpec((tm, tn), lambda j, i: (i, j)),
        out_shape=jax.ShapeDtypeStruct((m, d), F32),
        scratch_shapes=[pltpu.VMEM((d, tn), BF16)],
        compiler_params=_params(2),
        name="out_proj_residual",
    )(h, w_out, x, gate_vec.reshape(-1, 1, d))


def _router_kernel(x_ref, g_ref, sh_ref, sc_ref, wr_ref, br_ref, h_ref, aff_ref):
    y = _rms(x_ref[0], g_ref[...])
    h = (y * (1 + sc_ref[0]) + sh_ref[0]).astype(BF16)
    bits = pltpu.bitcast(h.astype(F32), jnp.uint32)
    half = bits.shape[1] // 2
    h_ref[0] = lax.shift_right_logical(bits[:, :half], jnp.uint32(16)) | bits[:, half:]
    logits = _nt_dot(wr_ref[...].astype(BF16), h) + br_ref[...]
    mx = jnp.max(logits, axis=0, keepdims=True)
    e = jnp.exp(logits - mx)
    aff_ref[0] = e / jnp.sum(e, axis=0, keepdims=True)


def router(x, g, shift, scale, w_router, b_router, tm=1024):
    b, n, d = x.shape
    tm = min(tm, n)
    return pl.pallas_call(
        _router_kernel,
        grid=(b, n // tm),
        in_specs=[
            pl.BlockSpec((1, tm, d), lambda i, j: (i, j, 0)),
            pl.BlockSpec((1, d), lambda i, j: (0, 0)),
            pl.BlockSpec((1, 1, d), lambda i, j: (i, 0, 0)),
            pl.BlockSpec((1, 1, d), lambda i, j: (i, 0, 0)),
            pl.BlockSpec((N_EXPERTS, d), lambda i, j: (0, 0)),
            pl.BlockSpec((N_EXPERTS, 1), lambda i, j: (0, 0)),
        ],
        out_specs=[
            pl.BlockSpec((1, tm, d // 2), lambda i, j: (i, j, 0)),
            pl.BlockSpec((1, N_EXPERTS, tm), lambda i, j: (i, 0, j)),
        ],
        out_shape=[jax.ShapeDtypeStruct((b, n, d // 2), jnp.uint32),
                   jax.ShapeDtypeStruct((b, N_EXPERTS, n), F32)],
        compiler_params=_params(2),
        name="router",
    )(x, g.reshape(1, d), shift.reshape(b, 1, d), scale.reshape(b, 1, d),
      w_router.T, b_router.reshape(N_EXPERTS, 1))


def _select_kernel(aff_ref, tri_ref, slot_ref, *, cap):
    bits = pltpu.bitcast(aff_ref[...], jnp.int32)

    def step(i, t):
        cand = t | lax.shift_left(jnp.int32(1), 30 - i)
        cnt = jnp.sum(jnp.where(bits >= cand, 1.0, 0.0), axis=1, keepdims=True)
        return jnp.where(cnt >= cap, cand, t)

    t = lax.fori_loop(0, 31, step, jnp.zeros((bits.shape[0], 1), jnp.int32))
    gt = bits > t
    eq = bits == t
    need = cap - jnp.sum(jnp.where(gt, 1.0, 0.0), axis=1, keepdims=True)
    tri = tri_ref[...]
    eq_rank = jnp.dot(jnp.where(eq, 1.0, 0.0).astype(BF16), tri, preferred_element_type=F32)
    sel = gt | (eq & (eq_rank < need))
    slot = jnp.dot(jnp.where(sel, 1.0, 0.0).astype(BF16), tri, preferred_element_type=F32)
    slot_ref[...] = jnp.where(sel, slot.astype(jnp.int32), -1)


def select_tokens(aff, cap):
    b, e, n = aff.shape
    tri = jnp.triu(jnp.ones((n, n), BF16), k=1)
    rows = pl.BlockSpec((b * e, n), lambda i: (0, 0))
    return pl.pallas_call(
        functools.partial(_select_kernel, cap=cap),
        grid=(1,),
        in_specs=[rows, pl.BlockSpec((n, n), lambda i: (0, 0))],
        out_specs=rows,
        out_shape=jax.ShapeDtypeStruct((b * e, n), jnp.int32),
        compiler_params=_params(1),
        name="select_tokens",
    )(aff.reshape(b * e, n), tri).reshape(b, e, n)


def _onehot(slot, cap):
    hit = lax.broadcasted_iota(jnp.int32, (cap, slot.shape[1]), 0) == slot
    return hit, jnp.where(hit, 1.0, 0.0)


def gather_tokens(h, slot, aff, cap):
    b, n, w = h.shape
    e = N_EXPERTS
    workers = V7X_SC_CORES * V7X_SC_SUBCORES
    rows_per_worker = (b * e) // workers
    chunk = min(cap, SC_GATHER_ROWS)
    assert (b * e) % workers == 0 and cap % chunk == 0 and n % V7X_SC_LANES == 0

    def body(h_hbm, slot_hbm, aff_hbm, xs_hbm, gates_hbm, slot_v, aff_v, idx_v, gate_v, rows_v, sem):
        worker = lax.axis_index("s") * V7X_SC_CORES + lax.axis_index("c")
        lane = lax.iota(jnp.int32, V7X_SC_LANES)

        @pl.loop(0, rows_per_worker)
        def _(j):
            r = worker * rows_per_worker + j
            bi = r // e
            ei = r - bi * e
            pltpu.sync_copy(slot_hbm.at[r], slot_v)
            pltpu.sync_copy(aff_hbm.at[r], aff_v)

            @pl.loop(0, n // V7X_SC_LANES)
            def _(c):
                tokens = pl.ds(c * V7X_SC_LANES, V7X_SC_LANES)
                s = slot_v[tokens]
                chosen = s >= 0
                plsc.store_scatter(idx_v, [s], lane + (c * V7X_SC_LANES + bi * n), mask=chosen)
                plsc.store_scatter(gate_v, [s], aff_v[tokens], mask=chosen)

            out_row = (ei * b + bi) * cap

            @pl.loop(0, cap // chunk)
            def _(q):
                pltpu.async_copy(h_hbm.at[idx_v.at[pl.ds(q * chunk, chunk)]], rows_v, sem).wait()
                pltpu.sync_copy(rows_v, xs_hbm.at[pl.ds(out_row + q * chunk, chunk)])

            pltpu.sync_copy(gate_v, gates_hbm.at[pl.ds(out_row, cap)])

    xs, gates = pl.kernel(
        body,
        mesh=plsc.VectorSubcoreMesh(core_axis_name="c", subcore_axis_name="s"),
        out_type=[jax.ShapeDtypeStruct((e * b * cap, w), h.dtype),
                  jax.ShapeDtypeStruct((e * b * cap,), F32)],
        scratch_types=[
            pltpu.VMEM((n,), jnp.int32), pltpu.VMEM((n,), F32),
            pltpu.VMEM((cap,), jnp.int32), pltpu.VMEM((cap,), F32),
            pltpu.VMEM((chunk, w), h.dtype),
            pltpu.SemaphoreType.DMA,
        ],
        compiler_params=pltpu.CompilerParams(needs_layout_passes=False),
        name="gather_tokens",
    )(h.reshape(b * n, w), slot.reshape(b * e, n), aff.reshape(b * e, n))
    return xs.reshape(e, b * cap, w), gates.reshape(e, b * cap, 1)


def _ffn_kernel(*refs, n_sets, n_f):
    xs_refs, gate_refs = refs[0:2 * n_sets:2], refs[1:2 * n_sets:2]
    wg_ref, wu_ref, wd_ref = refs[2 * n_sets:2 * n_sets + 3]
    y_refs = refs[2 * n_sets + 3:3 * n_sets + 3]
    act_refs = refs[3 * n_sets + 3:4 * n_sets + 3]
    row_refs = refs[4 * n_sets + 3:]
    s = pl.program_id(1)
    tf = wg_ref.shape[1]

    @pl.when(s == 0)
    def _():
        for xs_ref, row_ref in zip(xs_refs, row_refs):
            words = xs_ref[0]
            half = words.shape[1]
            row_ref[:, :half] = pltpu.bitcast(lax.shift_left(words, jnp.uint32(16)), F32).astype(BF16)
            row_ref[:, half:] = pltpu.bitcast(words & jnp.uint32(0xFFFF0000), F32).astype(BF16)

    @pl.when(s < n_f)
    def _():
        wg = wg_ref[...].astype(BF16)
        wu = wu_ref[...].astype(BF16)
        for row_ref, act_ref in zip(row_refs, act_refs):
            xs = row_ref[...]
            hg = jnp.dot(xs, wg, preferred_element_type=F32)
            hu = jnp.dot(xs, wu, preferred_element_type=F32)
            act_ref[s] = (jax.nn.silu(hg) * hu).astype(BF16)

    @pl.when(s >= n_f)
    def _():
        wd = wd_ref[...].astype(BF16)
        for gate_ref, y_ref, act_ref in zip(gate_refs, y_refs, act_refs):
            y = None
            for f in range(n_f):
                part = jnp.dot(act_ref[f], wd[f * tf:(f + 1) * tf, :], preferred_element_type=F32)
                y = part if y is None else y + part
            y_ref[0] = (y * gate_ref[0]).astype(y_ref.dtype)


def expert_ffn(sets, w_gate, w_up, w_down, layer, tf=512, tn=512):
    d, ff = w_gate.shape[2], w_gate.shape[3]
    n_f, n_d = ff // tf, d // tn
    d_idx = lambda s: jnp.maximum(s - n_f, 0)

    def up_tile(i, s):
        ahead = s >= n_f
        return (layer, jnp.where(ahead, jnp.minimum(i + 1, N_EXPERTS - 1), i), 0,
                jnp.where(ahead, 0, jnp.minimum(s, n_f - 1)))

    def down_tile(i, s):
        behind = s < n_f
        return (layer, jnp.where(behind, jnp.maximum(i - 1, 0), i), 0, jnp.where(behind, n_d - 1, s - n_f))
    in_specs, operands, out_specs, out_shape, act_scratch, row_scratch = [], [], [], [], [], []
    for xs, gates in sets:
        r = xs.shape[1]
        in_specs += [pl.BlockSpec((1, r, d // 2), lambda i, s: (i, 0, 0)),
                     pl.BlockSpec((1, r, 1), lambda i, s: (i, 0, 0))]
        operands += [xs, gates]
        out_specs.append(pl.BlockSpec((1, r, tn), lambda i, s: (i, 0, d_idx(s))))
        out_shape.append(jax.ShapeDtypeStruct((N_EXPERTS, r, d), BF16))
        act_scratch.append(pltpu.VMEM((n_f, r, tf), BF16))
        row_scratch.append(pltpu.VMEM((r, d), BF16))
    in_specs += [
        pl.BlockSpec((None, None, d, tf), up_tile),
        pl.BlockSpec((None, None, d, tf), up_tile),
        pl.BlockSpec((None, None, ff, tn), down_tile),
    ]
    return pl.pallas_call(
        functools.partial(_ffn_kernel, n_sets=len(sets), n_f=n_f),
        grid=(N_EXPERTS, n_f + n_d),
        in_specs=in_specs,
        out_specs=out_specs,
        out_shape=out_shape,
        scratch_shapes=act_scratch + row_scratch,
        compiler_params=_params(2),
        name="expert_ffn",
    )(*operands, w_gate, w_up, w_down)


def _combine_kernel(y_ref, slot_ref, x_ref, mod_ref, o_ref, onehot_ref, *, cap):
    @pl.when(pl.program_id(1) == 0)
    def _():
        for e in range(N_EXPERTS):
            onehot_ref[e * cap:(e + 1) * cap, :] = _onehot(slot_ref[e], cap)[1].astype(BF16)

    y = y_ref[...].reshape(N_EXPERTS * cap, y_ref.shape[2])
    moe = lax.dot_general(onehot_ref[...], y, (((0,), (0,)), ((), ())), preferred_element_type=F32)
    o_ref[0] = x_ref[0] + mod_ref[0] * moe


def combine(y, slot, x, gate_vec, cap, tn=512):
    b, n, d = x.shape
    e = N_EXPERTS
    return pl.pallas_call(
        functools.partial(_combine_kernel, cap=cap),
        grid=(b, d // tn),
        in_specs=[
            pl.BlockSpec((e, cap, tn), lambda i, j: (0, i, j)),
            pl.BlockSpec((e, 1, n), lambda i, j: (i, 0, 0)),
            pl.BlockSpec((1, n, tn), lambda i, j: (i, 0, j)),
            pl.BlockSpec((1, 1, tn), lambda i, j: (i, 0, j)),
        ],
        out_specs=pl.BlockSpec((1, n, tn), lambda i, j: (i, 0, j)),
        out_shape=jax.ShapeDtypeStruct((b, n, d), F32),
        scratch_shapes=[pltpu.VMEM((e * cap, n), BF16)],
        compiler_params=_params(2),
        name="combine",
    )(y, slot.reshape(b * e, 1, n), x, gate_vec.reshape(b, 1, d))


def moe_route(x, g, mod, w_router, b_router):
    n = x.shape[1]
    cap = CAPACITY_FACTOR * n // N_EXPERTS
    h, aff = router(x, g, mod[:, 3], mod[:, 4], w_router, b_router)
    slot = select_tokens(aff, cap)
    xs, gates = gather_tokens(h, slot, aff, cap)
    return slot, cap, (xs, gates)


def mixer_block(x, h, attn, mod, n, layer, w_in, conv_w, ln_g, w_s, b_s, w_branch, w_out):
    b = x.shape[0] // n
    p = matmul(h, w_in, layer, P_COL0, w_in.shape[2] - P_COL0)
    conv = short_conv(h, w_in, conv_w, layer, b, n)
    gm = chunk_gmlp(p, ln_g, w_s, b_s, layer)
    merged = merge_branches(attn, conv, gm, p, w_branch, layer)
    return out_proj_residual(merged, w_out, layer, x, mod[:, 2], n)


def kernel(x, c, ctx, c_ctx, w_mod, b_mod, norm1_g, w_in, na_rpb, conv_w, gmlp_ln_g, w_spatial, b_spatial,
           w_branch, w_out, norm2_g, w_router, b_router, w_e_gate, w_e_up, w_e_down, final_g):
    b, n, d = x.shape
    lc = ctx.shape[1]
    cvec = jnp.concatenate([c, c_ctx[None], jnp.zeros((8 - b - 1, d), F32)], axis=0)
    modv = modvec(cvec, w_mod, b_mod).reshape(DEPTH, 8, 6, d)
    cos_t, sin_t = rope_tables(n)
    bias = na_bias(na_rpb, n // GRID_W)
    mix_w = (w_in, conv_w, gmlp_ln_g, w_spatial, b_spatial, w_branch, w_out)
    ctx_s = ctx
    for layer in range(DEPTH):
        last = layer == DEPTH - 1
        mod = modv[layer, :b]
        mod_c = jnp.broadcast_to(modv[layer, b][None], (b, 6, d))
        route_w = (w_router[layer], b_router[layer])

        hc = norm_mod(ctx_s, norm1_g[layer], mod_c[:, 0], mod_c[:, 1]).reshape(b * lc, d)
        if last:
            kvc = qkv_proj(hc, w_in, layer, lc, cos_t, sin_t, rope=False, first_tile=1, n_tiles=2)
            kc_col = 0
        else:
            kvc = qkv_proj(hc, w_in, layer, lc, cos_t, sin_t, rope=False)
            kc_col = D_BRANCH
            attn_c = ctx_attention(kvc, b, lc)
            ctx_mid = mixer_block(ctx_s.reshape(b * lc, d), hc, attn_c, mod_c, lc, layer, *mix_w)
            ctx_mid = ctx_mid.reshape(b, lc, d)

        h = norm_mod(x, norm1_g[layer], mod[:, 0], mod[:, 1]).reshape(b * n, d)
        qkv = qkv_proj(h, w_in, layer, n, cos_t, sin_t, rope=True)
        attn = na_attention(qkv, kvc, kc_col, bias, layer, b, n)
        x = mixer_block(x.reshape(b * n, d), h, attn, mod, n, layer, *mix_w).reshape(b, n, d)

        slot, cap, rows = moe_route(x, norm2_g[layer], mod, *route_w)
        if last:
            (y,) = expert_ffn([rows], w_e_gate, w_e_up, w_e_down, layer)
        else:
            slot_c, cap_c, rows_c = moe_route(ctx_mid, norm2_g[layer], mod_c, *route_w)
            y, y_c = expert_ffn([rows, rows_c], w_e_gate, w_e_up, w_e_down, layer)
            ctx_s = combine(y_c, slot_c, ctx_mid, mod_c[:, 5], cap_c)
        x = combine(y, slot, x, mod[:, 5], cap)
    return final_norm(x, final_g)
```

```python
import functools

import numpy as np
import jax
import jax.numpy as jnp
from jax import lax
from jax.experimental import pallas as pl
from jax.experimental.pallas import tpu as pltpu
from jax.experimental.pallas import tpu_sc as plsc

D_MODEL = 2048
DEPTH = 2
GRID_W = 64
D_BRANCH = D_MODEL // 2
N_BRANCHES = 3
HEAD_DIM = 128
N_HEADS = D_BRANCH // HEAD_DIM
NA_KH = 8
NA_KW = 16
ROPE_THETA = 10000.0
ROPE_AXIS_DIM = HEAD_DIM // 2
CHUNK = 128
GMLP_GROUPS = D_BRANCH // 128
N_EXPERTS = 16
CAPACITY_FACTOR = 2
NORM_EPS = 1e-6
NEG_INF = -1e30
LOG2_E = float(np.log2(np.e))

QKV_COLS = 3 * D_BRANCH
COL_XC, COL_BG, COL_CG = (QKV_COLS + i * D_BRANCH for i in range(3))
P_COL0 = QKV_COLS + 3 * D_BRANCH
COL_U, COL_VS, COL_GATE = 0, D_BRANCH, 2 * D_BRANCH

NA_Q_ROWS = 4
NA_BAND_ROWS = 12
NA_TQ = NA_Q_ROWS * GRID_W
NA_TK = NA_BAND_ROWS * GRID_W
NA_HEADS_PER_STEP = 4
NA_HB = NA_HEADS_PER_STEP * HEAD_DIM
NA_SCORE_LEAD = 2

V7X_VMEM_LIMIT = 56 * 1024 * 1024
V7X_SC_CORES, V7X_SC_SUBCORES, V7X_SC_LANES = 2, 16, 16
SC_GATHER_ROWS = 64

BF16 = jnp.bfloat16
F32 = jnp.float32


def _params(n_axes, vmem=V7X_VMEM_LIMIT):
    return pltpu.CompilerParams(dimension_semantics=("arbitrary",) * n_axes, vmem_limit_bytes=vmem)


def _modvec_kernel(c_ref, w_ref, b_ref, o_ref):
    s = jax.nn.silu(c_ref[...]).astype(BF16)
    o_ref[0] = jnp.dot(s, w_ref[0].astype(BF16), preferred_element_type=F32) + b_ref[0]


def modvec(cvec, w_mod, b_mod, tn=1024):
    nl, d, n6 = w_mod.shape
    return pl.pallas_call(
        _modvec_kernel,
        grid=(nl, n6 // tn),
        in_specs=[
            pl.BlockSpec((8, d), lambda l, j: (0, 0)),
            pl.BlockSpec((1, d, tn), lambda l, j: (l, 0, j)),
            pl.BlockSpec((1, 1, tn), lambda l, j: (l, 0, j)),
        ],
        out_specs=pl.BlockSpec((1, 8, tn), lambda l, j: (l, 0, j)),
        out_shape=jax.ShapeDtypeStruct((nl, 8, n6), F32),
        compiler_params=_params(2),
        name="modvec",
    )(cvec, w_mod, b_mod.reshape(nl, 1, n6))


def _rms(x, g):
    ms = jnp.mean(x * x, axis=-1, keepdims=True)
    return x * lax.rsqrt(ms + NORM_EPS) * g


def _norm_mod_kernel(x_ref, g_ref, sh_ref, sc_ref, o_ref):
    y = _rms(x_ref[0], g_ref[...])
    o_ref[0] = (y * (1 + sc_ref[0]) + sh_ref[0]).astype(o_ref.dtype)


def _norm_kernel(x_ref, g_ref, o_ref):
    o_ref[0] = _rms(x_ref[0], g_ref[...]).astype(o_ref.dtype)


def norm_mod(x, g, shift, scale, out_dtype=BF16, tm=1024):
    b, n, d = x.shape
    tm = min(tm, n)
    return pl.pallas_call(
        _norm_mod_kernel,
        grid=(b, n // tm),
        in_specs=[
            pl.BlockSpec((1, tm, d), lambda i, j: (i, j, 0)),
            pl.BlockSpec((1, d), lambda i, j: (0, 0)),
            pl.BlockSpec((1, 1, d), lambda i, j: (i, 0, 0)),
            pl.BlockSpec((1, 1, d), lambda i, j: (i, 0, 0)),
        ],
        out_specs=pl.BlockSpec((1, tm, d), lambda i, j: (i, j, 0)),
        out_shape=jax.ShapeDtypeStruct((b, n, d), out_dtype),
        compiler_params=_params(2),
        name="norm_mod",
    )(x, g.reshape(1, d), shift.reshape(b, 1, d), scale.reshape(b, 1, d))


def final_norm(x, g, tm=1024):
    b, n, d = x.shape
    return pl.pallas_call(
        _norm_kernel,
        grid=(b, n // tm),
        in_specs=[
            pl.BlockSpec((1, tm, d), lambda i, j: (i, j, 0)),
            pl.BlockSpec((1, d), lambda i, j: (0, 0)),
        ],
        out_specs=pl.BlockSpec((1, tm, d), lambda i, j: (i, j, 0)),
        out_shape=jax.ShapeDtypeStruct((b, n, d), F32),
        compiler_params=_params(2),
        name="final_norm",
    )(x, g.reshape(1, d))


def _mm_kernel(x_ref, w_ref, o_ref, wbf_ref):
    @pl.when(pl.program_id(1) == 0)
    def _():
        wbf_ref[...] = w_ref[...].astype(BF16)

    o_ref[...] = jnp.dot(x_ref[...], wbf_ref[...], preferred_element_type=F32).astype(o_ref.dtype)


def matmul(x, w, layer, col_off, n_cols, out_dtype=F32, tm=1024, tn=1024):
    m, k = x.shape
    tm = min(tm, m)
    off = col_off // tn
    return pl.pallas_call(
        _mm_kernel,
        grid=(n_cols // tn, m // tm),
        in_specs=[
            pl.BlockSpec((tm, k), lambda j, i: (i, 0)),
            pl.BlockSpec((None, k, tn), lambda j, i: (layer, 0, off + j)),
        ],
        out_specs=pl.BlockSpec((tm, tn), lambda j, i: (i, j)),
        out_shape=jax.ShapeDtypeStruct((m, n_cols), out_dtype),
        scratch_shapes=[pltpu.VMEM((k, tn), BF16)],
        compiler_params=_params(2),
        name="matmul",
    )(x, w)


def _qkv_kernel(x_ref, w_ref, cos_ref, sin_ref, o_ref, wbf_ref, *, rope):
    @pl.when(pl.program_id(1) == 0)
    def _():
        wbf_ref[...] = w_ref[...].astype(BF16)

    def plain():
        o_ref[...] = jnp.dot(x_ref[...], wbf_ref[...], preferred_element_type=F32).astype(BF16)

    if not rope:
        plain()
        return
    c = pl.program_id(0)

    @pl.when(c < 2)
    def _():
        cs = cos_ref[...]
        sn = sin_ref[...]
        lane = lax.broadcasted_iota(jnp.int32, cs.shape, 1)
        first_half = (lane % ROPE_AXIS_DIM) < (ROPE_AXIS_DIM // 2)
        for pair in range(N_HEADS // 2):
            cols = slice(pair * 2 * HEAD_DIM, (pair + 1) * 2 * HEAD_DIM)
            acc = jnp.dot(x_ref[...], wbf_ref[:, cols], preferred_element_type=F32)
            for h in range(2):
                t = acc[:, h * HEAD_DIM:(h + 1) * HEAD_DIM]
                partner = jnp.where(first_half,
                                    pltpu.roll(t, HEAD_DIM - ROPE_AXIS_DIM // 2, 1),
                                    pltpu.roll(t, ROPE_AXIS_DIM // 2, 1))
                lo = (pair * 2 + h) * HEAD_DIM
                o_ref[:, lo:lo + HEAD_DIM] = (t * cs + partner * sn).astype(BF16)

    @pl.when(c == 2)
    def _():
        plain()


def rope_tables(n):
    pos = jnp.arange(n)
    rc = jnp.stack([pos // GRID_W, pos % GRID_W], axis=-1).astype(F32)
    inv = ROPE_THETA ** (-jnp.arange(0, ROPE_AXIS_DIM, 2, dtype=F32) / ROPE_AXIS_DIM)
    ang = rc[:, :, None] * inv
    cos, sin = jnp.cos(ang), jnp.sin(ang)
    cos_t = jnp.concatenate([cos, cos], axis=-1).reshape(n, HEAD_DIM)
    sin_t = jnp.concatenate([-sin, sin], axis=-1).reshape(n, HEAD_DIM)
    return cos_t, sin_t


def qkv_proj(x, w, layer, n, cos_t, sin_t, rope, first_tile=0, n_tiles=3, tm=1024):
    m, k = x.shape
    tm = min(tm, n if rope else m)
    nt = n // tm if rope else 1
    tab = pl.BlockSpec((tm, HEAD_DIM), lambda c, i: (i % nt, 0))
    return pl.pallas_call(
        functools.partial(_qkv_kernel, rope=rope),
        grid=(n_tiles, m // tm),
        in_specs=[
            pl.BlockSpec((tm, k), lambda c, i: (i, 0)),
            pl.BlockSpec((None, k, D_BRANCH), lambda c, i: (layer, 0, first_tile + c)),
            tab, tab,
        ],
        out_specs=pl.BlockSpec((tm, D_BRANCH), lambda c, i: (i, c)),
        out_shape=jax.ShapeDtypeStruct((m, n_tiles * D_BRANCH), BF16),
        scratch_shapes=[pltpu.VMEM((k, D_BRANCH), BF16)],
        compiler_params=_params(2),
        name="qkv_proj",
    )(x, w, cos_t, sin_t)


def _nt_dot(a, b):
    return lax.dot_general(a, b, (((1,), (1,)), ((), ())), preferred_element_type=F32)


def _na_pattern_rows(g, rows):
    band_row = int(np.clip(g * NA_Q_ROWS - NA_KH // 2, 0, rows - NA_BAND_ROWS))
    qr = g * NA_Q_ROWS + np.arange(NA_Q_ROWS)
    rs = np.clip(qr - NA_KH // 2, 0, rows - NA_KH)
    return band_row, qr, rs


def _na_bias_kernel(rpb_ref, o_ref, *, rows):
    shape = (GRID_W, 2 * GRID_W)
    qc = lax.broadcasted_iota(jnp.int32, shape, 0)
    lane = lax.broadcasted_iota(jnp.int32, shape, 1)
    low = lane < GRID_W
    kc = lane % GRID_W
    ws = jnp.clip(qc - NA_KW // 2, 0, GRID_W - NA_KW)
    col_ok = (kc >= ws) & (kc < ws + NA_KW)
    neg = jnp.full(shape, NEG_INF, F32)

    tiles = []
    for dr in range(2 * NA_KH - 1):
        t = jnp.broadcast_to(rpb_ref[0, 0, dr:dr + 1, :], shape)
        t = pltpu.roll(t, 2 * GRID_W - (NA_KW - 1), 1)
        for bit in range(6):
            t = jnp.where(((qc >> bit) & 1) == 1, pltpu.roll(t, 1 << bit, 1), t)
        t = jnp.where(low, t, pltpu.roll(t, GRID_W, 1))
        tiles.append(jnp.where(col_ok, t * LOG2_E, NEG_INF))

    n_groups = rows // NA_Q_ROWS
    for p, g in enumerate((0, 1, n_groups - 1)):
        band_row, qr, rs = _na_pattern_rows(g, rows)
        for qi in range(NA_Q_ROWS):
            def half(krl):
                kr = band_row + krl
                if rs[qi] <= kr < rs[qi] + NA_KH:
                    return tiles[kr - qr[qi] + NA_KH - 1]
                return neg
            for pair in range(NA_BAND_ROWS // 2):
                blk = jnp.where(low, half(2 * pair), half(2 * pair + 1))
                o_ref[0, p, 0, qi * GRID_W:(qi + 1) * GRID_W, pair * 128:(pair + 1) * 128] = blk


def na_bias(na_rpb, rows):
    nl, nh, ndr, ndc = na_rpb.shape
    band1, qr1, rs1 = _na_pattern_rows(1, rows)
    for g in range(2, rows // NA_Q_ROWS - 1):
        band, qr, rs = _na_pattern_rows(g, rows)
        assert (qr - band == qr1 - band1).all() and (rs - band == rs1 - band1).all()
    rpb = jnp.pad(na_rpb, ((0, 0), (0, 0), (0, 16 - ndr), (0, 128 - ndc)))
    return pl.pallas_call(
        functools.partial(_na_bias_kernel, rows=rows),
        grid=(nl, nh),
        in_specs=[pl.BlockSpec((1, 1, 16, 128), lambda l, h: (l, h, 0, 0))],
        out_specs=pl.BlockSpec((1, 3, 1, NA_TQ, NA_TK), lambda l, h: (l, 0, h, 0, 0)),
        out_shape=jax.ShapeDtypeStruct((nl, 3, nh, NA_TQ, NA_TK), F32),
        compiler_params=_params(2),
        name="na_bias",
    )(rpb)


def _na_kernel(q_ref, k_ref, v_ref, kc_ref, vc_ref, bias_ref, o_ref):
    g = pl.program_id(2)
    n_groups = k_ref.shape[1] // NA_TQ
    band_row = jnp.clip(g * NA_Q_ROWS - NA_KH // 2, 0, n_groups * NA_Q_ROWS - NA_BAND_ROWS)
    start = pl.multiple_of(band_row * GRID_W, NA_TQ)
    pat = jnp.where(g == 0, 0, jnp.where(g == n_groups - 1, 2, 1))
    scale = HEAD_DIM ** -0.5 * LOG2_E

    def scores(hh):
        cols = slice(hh * HEAD_DIM, (hh + 1) * HEAD_DIM)
        q = q_ref[0, :, cols]
        return _nt_dot(q, k_ref[0, pl.ds(start, NA_TK), cols]), _nt_dot(q, kc_ref[0, :, cols])

    def finish(hh, qk_loc, qk_ctx):
        cols = slice(hh * HEAD_DIM, (hh + 1) * HEAD_DIM)
        s_loc = qk_loc * scale + bias_ref[0, pat, hh]
        s_ctx = qk_ctx * scale
        m = jnp.maximum(jnp.max(s_loc, axis=1, keepdims=True), jnp.max(s_ctx, axis=1, keepdims=True))
        e_loc = jnp.exp2(s_loc - m)
        e_ctx = jnp.exp2(s_ctx - m)
        inv = 1.0 / (jnp.sum(e_loc, axis=1, keepdims=True) + jnp.sum(e_ctx, axis=1, keepdims=True))
        o = jnp.dot(e_loc.astype(BF16), v_ref[0, pl.ds(start, NA_TK), cols], preferred_element_type=F32)
        o = o + jnp.dot(e_ctx.astype(BF16), vc_ref[0, :, cols], preferred_element_type=F32)
        o_ref[0, :, cols] = (o * inv).astype(o_ref.dtype)

    qk = {}
    for hh in range(NA_HEADS_PER_STEP + NA_SCORE_LEAD):
        if hh < NA_HEADS_PER_STEP:
            qk[hh] = scores(hh)
        if hh >= NA_SCORE_LEAD:
            finish(hh - NA_SCORE_LEAD, *qk.pop(hh - NA_SCORE_LEAD))


def na_attention(qkv, kvc, kc_col, bias, layer, b, n):
    lc = kvc.shape[0] // b
    qkv3 = qkv.reshape(b, n, QKV_COLS)
    kvc3 = kvc.reshape(b, lc, kvc.shape[1])
    hbs = D_BRANCH // NA_HB
    kcb = kc_col // NA_HB
    out = pl.pallas_call(
        _na_kernel,
        grid=(hbs, b, n // NA_TQ),
        in_specs=[
            pl.BlockSpec((1, NA_TQ, NA_HB), lambda h, i, g: (i, g, h)),
            pl.BlockSpec((1, n, NA_HB), lambda h, i, g: (i, 0, hbs + h)),
            pl.BlockSpec((1, n, NA_HB), lambda h, i, g: (i, 0, 2 * hbs + h)),
            pl.BlockSpec((1, lc, NA_HB), lambda h, i, g: (i, 0, kcb + h)),
            pl.BlockSpec((1, lc, NA_HB), lambda h, i, g: (i, 0, kcb + hbs + h)),
            pl.BlockSpec((1, 3, NA_HEADS_PER_STEP, NA_TQ, NA_TK), lambda h, i, g: (layer, 0, h, 0, 0)),
        ],
        out_specs=pl.BlockSpec((1, NA_TQ, NA_HB), lambda h, i, g: (i, g, h)),
        out_shape=jax.ShapeDtypeStruct((b, n, D_BRANCH), BF16),
        compiler_params=_params(3),
        name="na_attention",
    )(qkv3, qkv3, qkv3, kvc3, kvc3, bias)
    return out.reshape(b * n, D_BRANCH)


def _ctx_attn_kernel(q_ref, k_ref, v_ref, o_ref):
    for h in range(N_HEADS):
        cols = slice(h * HEAD_DIM, (h + 1) * HEAD_DIM)
        s = _nt_dot(q_ref[0, :, cols], k_ref[0, :, cols]) * (HEAD_DIM ** -0.5)
        m = jnp.max(s, axis=1, keepdims=True)
        e = jnp.exp(s - m)
        p = e * (1.0 / jnp.sum(e, axis=1, keepdims=True))
        o_ref[0, :, cols] = jnp.dot(p.astype(BF16), v_ref[0, :, cols],
                                    preferred_element_type=F32).astype(o_ref.dtype)


def ctx_attention(qkv, b, lc):
    qkv3 = qkv.reshape(b, lc, QKV_COLS)
    part = lambda k: pl.BlockSpec((1, lc, D_BRANCH), lambda i: (i, 0, k))
    out = pl.pallas_call(
        _ctx_attn_kernel,
        grid=(b,),
        in_specs=[part(0), part(1), part(2)],
        out_specs=part(0),
        out_shape=jax.ShapeDtypeStruct((b, lc, D_BRANCH), BF16),
        compiler_params=_params(1),
        name="ctx_attention",
    )(qkv3, qkv3, qkv3)
    return out.reshape(b * lc, D_BRANCH)


def _conv_kernel(h_ref, wx_ref, wb_ref, wc_ref, w_ref, o_ref, wbf_ref):
    @pl.when(pl.program_id(1) == 0)
    def _():
        for k, ref in enumerate((wx_ref, wb_ref, wc_ref)):
            wbf_ref[k] = ref[...].astype(BF16)

    h = h_ref[0]
    xc, bg, cg = (jnp.dot(h, wbf_ref[k], preferred_element_type=F32) for k in range(3))
    z = cg * xc
    n = z.shape[0]
    row = lax.broadcasted_iota(jnp.int32, z.shape, 0)
    z_prev = jnp.where(row == 0, 0.0, pltpu.roll(z, 1, 0))
    z_next = jnp.where(row == n - 1, 0.0, pltpu.roll(z, n - 1, 0))
    y = z_prev * w_ref[0, 0:1, :] + z * w_ref[0, 1:2, :] + z_next * w_ref[0, 2:3, :]
    o_ref[0] = (bg * y).astype(o_ref.dtype)


def short_conv(h, w_in, conv_w, layer, b, n, tc=256):
    d = h.shape[1]
    wcol = lambda col: pl.BlockSpec((None, d, tc), lambda j, i: (layer, 0, col // tc + j))
    out = pl.pallas_call(
        _conv_kernel,
        grid=(D_BRANCH // tc, b),
        in_specs=[pl.BlockSpec((1, n, d), lambda j, i: (i, 0, 0)),
                  wcol(COL_XC), wcol(COL_BG), wcol(COL_CG),
                  pl.BlockSpec((1, 3, tc), lambda j, i: (layer, 0, j))],
        out_specs=pl.BlockSpec((1, n, tc), lambda j, i: (i, 0, j)),
        out_shape=jax.ShapeDtypeStruct((b, n, D_BRANCH), BF16),
        scratch_shapes=[pltpu.VMEM((3, d, tc), BF16)],
        compiler_params=_params(2),
        name="short_conv",
    )(h.reshape(b, n, d), w_in, w_in, w_in, conv_w)
    return out.reshape(b * n, D_BRANCH)


def _gmlp_kernel(u_ref, v_ref, g_ref, ws_ref, bs_ref, o_ref):
    v = jax.nn.gelu(v_ref[...])
    mu = jnp.mean(v, axis=-1, keepdims=True)
    var = jnp.mean(jnp.square(v - mu), axis=-1, keepdims=True)
    vn = ((v - mu) * lax.rsqrt(var + NORM_EPS) * g_ref[...]).astype(BF16)
    for grp in range(GMLP_GROUPS):
        cols = slice(grp * 128, (grp + 1) * 128)
        ws = ws_ref[0, grp].astype(BF16)
        for ch in range(v.shape[0] // CHUNK):
            rows = slice(ch * CHUNK, (ch + 1) * CHUNK)
            s = jnp.dot(ws, vn[rows, cols], preferred_element_type=F32) + bs_ref[:, grp:grp + 1]
            o_ref[rows, cols] = (jax.nn.gelu(u_ref[rows, cols]) * s).astype(o_ref.dtype)


def chunk_gmlp(p, ln_g, w_s, b_s, layer, tm=8 * CHUNK):
    m = p.shape[0]
    return pl.pallas_call(
        _gmlp_kernel,
        grid=(m // tm,),
        in_specs=[
            pl.BlockSpec((tm, D_BRANCH), lambda i: (i, COL_U // D_BRANCH)),
            pl.BlockSpec((tm, D_BRANCH), lambda i: (i, COL_VS // D_BRANCH)),
            pl.BlockSpec((1, D_BRANCH), lambda i: (0, 0)),
            pl.BlockSpec((1, GMLP_GROUPS, CHUNK, CHUNK), lambda i: (layer, 0, 0, 0)),
            pl.BlockSpec((CHUNK, GMLP_GROUPS), lambda i: (0, 0)),
        ],
        out_specs=pl.BlockSpec((tm, D_BRANCH), lambda i: (i, 0)),
        out_shape=jax.ShapeDtypeStruct((m, D_BRANCH), BF16),
        compiler_params=_params(1),
        name="chunk_gmlp",
    )(p, p, ln_g[layer].reshape(1, D_BRANCH), w_s, b_s[layer].T)


def _merge_kernel(a_ref, c_ref, m_ref, g0_ref, g1_ref, g2_ref, w_ref, o_ref, wbf_ref):
    @pl.when(pl.program_id(1) == 0)
    def _():
        wbf_ref[...] = w_ref[...].astype(BF16)

    acc = None
    for i, (br, gl) in enumerate(((a_ref, g0_ref), (c_ref, g1_ref), (m_ref, g2_ref))):
        proj = jnp.dot(br[...], wbf_ref[i], preferred_element_type=F32)
        term = jax.nn.sigmoid(gl[...]) * proj
        acc = term if acc is None else acc + term
    o_ref[...] = acc.astype(o_ref.dtype)


def merge_branches(attn, conv, gm, p, w_branch, layer, tm=512, tn=1024):
    m = attn.shape[0]
    br = pl.BlockSpec((tm, D_BRANCH), lambda j, i: (i, 0))
    gate = lambda k: pl.BlockSpec((tm, tn), lambda j, i: (i, (COL_GATE + k * D_MODEL) // tn + j))
    return pl.pallas_call(
        _merge_kernel,
        grid=(D_MODEL // tn, m // tm),
        in_specs=[br, br, br, gate(0), gate(1), gate(2),
                  pl.BlockSpec((None, N_BRANCHES, D_BRANCH, tn), lambda j, i: (layer, 0, 0, j))],
        out_specs=pl.BlockSpec((tm, tn), lambda j, i: (i, j)),
        out_shape=jax.ShapeDtypeStruct((m, D_MODEL), BF16),
        scratch_shapes=[pltpu.VMEM((N_BRANCHES, D_BRANCH, tn), BF16)],
        compiler_params=_params(2),
        name="merge_branches",
    )(attn, conv, gm, p, p, p, w_branch)


def _out_proj_router_kernel(h_ref, w_ref, x_ref, mod_ref, g_ref, sh_ref, sc_ref, wr_ref, br_ref,
                            o_ref, hw_ref, aff_ref, wbf_ref):
    @pl.when(pl.program_id(0) == 0)
    def _():
        wbf_ref[...] = w_ref[...].astype(BF16)

    x = x_ref[...] + mod_ref[0] * jnp.dot(h_ref[...], wbf_ref[...], preferred_element_type=F32)
    o_ref[...] = x
    y = _rms(x, g_ref[...])
    h = (y * (1 + sc_ref[0]) + sh_ref[0]).astype(BF16)
    bits = pltpu.bitcast(h.astype(F32), jnp.uint32)
    half = bits.shape[1] // 2
    hw_ref[...] = lax.shift_right_logical(bits[:, :half], jnp.uint32(16)) | bits[:, half:]
    logits = _nt_dot(wr_ref[...].astype(BF16), h) + br_ref[...]
    mx = jnp.max(logits, axis=0, keepdims=True)
    e = jnp.exp(logits - mx)
    aff_ref[0] = e / jnp.sum(e, axis=0, keepdims=True)


def out_proj_router(h, w_out, layer, x, mod, g, w_router, b_router, n, tm=512):
    m, d = x.shape
    b = m // n
    tm = min(tm, n)
    per_b = n // tm
    vec = lambda k: (pl.BlockSpec((1, 1, d), lambda i: (i // per_b, 0, 0)), mod[:, k].reshape(b, 1, d))
    (gate_s, gate_v), (sh_s, sh_v), (sc_s, sc_v) = vec(2), vec(3), vec(4)
    return pl.pallas_call(
        _out_proj_router_kernel,
        grid=(m // tm,),
        in_specs=[
            pl.BlockSpec((tm, d), lambda i: (i, 0)),
            pl.BlockSpec((None, d, d), lambda i: (layer, 0, 0), pipeline_mode=pl.Buffered(1)),
            pl.BlockSpec((tm, d), lambda i: (i, 0)),
            gate_s,
            pl.BlockSpec((1, d), lambda i: (0, 0)),
            sh_s, sc_s,
            pl.BlockSpec((N_EXPERTS, d), lambda i: (0, 0)),
            pl.BlockSpec((N_EXPERTS, 1), lambda i: (0, 0)),
        ],
        out_specs=[
            pl.BlockSpec((tm, d), lambda i: (i, 0)),
            pl.BlockSpec((tm, d // 2), lambda i: (i, 0)),
            pl.BlockSpec((1, N_EXPERTS, tm), lambda i: (i // per_b, 0, i % per_b)),
        ],
        out_shape=[jax.ShapeDtypeStruct((m, d), F32),
                   jax.ShapeDtypeStruct((m, d // 2), jnp.uint32),
                   jax.ShapeDtypeStruct((b, N_EXPERTS, n), F32)],
        scratch_shapes=[pltpu.VMEM((d, d), BF16)],
        compiler_params=_params(1),
        name="out_proj_router",
    )(h, w_out, x, gate_v, g.reshape(1, d), sh_v, sc_v, w_router.T, b_router.reshape(N_EXPERTS, 1))


def _select_kernel(aff_ref, tri_ref, slot_ref, *, cap):
    bits = pltpu.bitcast(aff_ref[...], jnp.int32)

    def step(i, t):
        cand = t | lax.shift_left(jnp.int32(1), 30 - i)
        cnt = jnp.sum(jnp.where(bits >= cand, 1.0, 0.0), axis=1, keepdims=True)
        return jnp.where(cnt >= cap, cand, t)

    t = lax.fori_loop(0, 31, step, jnp.zeros((bits.shape[0], 1), jnp.int32))
    gt = bits > t
    eq = bits == t
    need = cap - jnp.sum(jnp.where(gt, 1.0, 0.0), axis=1, keepdims=True)
    tri = tri_ref[...]
    eq_rank = jnp.dot(jnp.where(eq, 1.0, 0.0).astype(BF16), tri, preferred_element_type=F32)
    sel = gt | (eq & (eq_rank < need))
    slot = jnp.dot(jnp.where(sel, 1.0, 0.0).astype(BF16), tri, preferred_element_type=F32)
    slot_ref[...] = jnp.where(sel, slot.astype(jnp.int32), -1)


def select_tokens(aff, cap):
    b, e, n = aff.shape
    tri = jnp.triu(jnp.ones((n, n), BF16), k=1)
    rows = pl.BlockSpec((b * e, n), lambda i: (0, 0))
    return pl.pallas_call(
        functools.partial(_select_kernel, cap=cap),
        grid=(1,),
        in_specs=[rows, pl.BlockSpec((n, n), lambda i: (0, 0))],
        out_specs=rows,
        out_shape=jax.ShapeDtypeStruct((b * e, n), jnp.int32),
        compiler_params=_params(1),
        name="select_tokens",
    )(aff.reshape(b * e, n), tri).reshape(b, e, n)


def _onehot(slot, cap):
    hit = lax.broadcasted_iota(jnp.int32, (cap, slot.shape[1]), 0) == slot
    return hit, jnp.where(hit, 1.0, 0.0)


def gather_tokens(h, slot, aff, cap):
    b, n, w = h.shape
    e = N_EXPERTS
    workers = V7X_SC_CORES * V7X_SC_SUBCORES
    rows_per_worker = (b * e) // workers
    chunk = min(cap, SC_GATHER_ROWS)
    assert (b * e) % workers == 0 and cap % chunk == 0 and n % V7X_SC_LANES == 0

    def body(h_hbm, slot_hbm, aff_hbm, xs_hbm, gates_hbm, slot_v, aff_v, idx_v, gate_v, rows_v, sem):
        worker = lax.axis_index("s") * V7X_SC_CORES + lax.axis_index("c")
        lane = lax.iota(jnp.int32, V7X_SC_LANES)

        @pl.loop(0, rows_per_worker)
        def _(j):
            r = worker * rows_per_worker + j
            bi = r // e
            ei = r - bi * e
            pltpu.sync_copy(slot_hbm.at[r], slot_v)
            pltpu.sync_copy(aff_hbm.at[r], aff_v)

            @pl.loop(0, n // V7X_SC_LANES)
            def _(c):
                tokens = pl.ds(c * V7X_SC_LANES, V7X_SC_LANES)
                s = slot_v[tokens]
                chosen = s >= 0
                plsc.store_scatter(idx_v, [s], lane + (c * V7X_SC_LANES + bi * n), mask=chosen)
                plsc.store_scatter(gate_v, [s], aff_v[tokens], mask=chosen)

            out_row = (ei * b + bi) * cap

            @pl.loop(0, cap // chunk)
            def _(q):
                pltpu.async_copy(h_hbm.at[idx_v.at[pl.ds(q * chunk, chunk)]], rows_v, sem).wait()
                pltpu.sync_copy(rows_v, xs_hbm.at[pl.ds(out_row + q * chunk, chunk)])

            pltpu.sync_copy(gate_v, gates_hbm.at[pl.ds(out_row, cap)])

    xs, gates = pl.kernel(
        body,
        mesh=plsc.VectorSubcoreMesh(core_axis_name="c", subcore_axis_name="s"),
        out_type=[jax.ShapeDtypeStruct((e * b * cap, w), h.dtype),
                  jax.ShapeDtypeStruct((e * b * cap,), F32)],
        scratch_types=[
            pltpu.VMEM((n,), jnp.int32), pltpu.VMEM((n,), F32),
            pltpu.VMEM((cap,), jnp.int32), pltpu.VMEM((cap,), F32),
            pltpu.VMEM((chunk, w), h.dtype),
            pltpu.SemaphoreType.DMA,
        ],
        compiler_params=pltpu.CompilerParams(needs_layout_passes=False),
        name="gather_tokens",
    )(h.reshape(b * n, w), slot.reshape(b * e, n), aff.reshape(b * e, n))
    return xs.reshape(e, b * cap, w), gates.reshape(e, b * cap, 1)


def _ffn_kernel(*refs, n_sets, n_f):
    xs_refs, gate_refs = refs[0:2 * n_sets:2], refs[1:2 * n_sets:2]
    wg_ref, wu_ref, wd_ref = refs[2 * n_sets:2 * n_sets + 3]
    y_refs = refs[2 * n_sets + 3:3 * n_sets + 3]
    act_refs = refs[3 * n_sets + 3:4 * n_sets + 3]
    row_refs = refs[4 * n_sets + 3:]
    s = pl.program_id(1)
    tf = wg_ref.shape[1]

    @pl.when(s == 0)
    def _():
        for xs_ref, row_ref in zip(xs_refs, row_refs):
            words = xs_ref[0]
            half = words.shape[1]
            row_ref[:, :half] = pltpu.bitcast(lax.shift_left(words, jnp.uint32(16)), F32).astype(BF16)
            row_ref[:, half:] = pltpu.bitcast(words & jnp.uint32(0xFFFF0000), F32).astype(BF16)

    @pl.when(s < n_f)
    def _():
        wg = wg_ref[...].astype(BF16)
        wu = wu_ref[...].astype(BF16)
        for row_ref, act_ref in zip(row_refs, act_refs):
            xs = row_ref[...]
            hg = jnp.dot(xs, wg, preferred_element_type=F32)
            hu = jnp.dot(xs, wu, preferred_element_type=F32)
            act_ref[s] = (jax.nn.silu(hg) * hu).astype(BF16)

    @pl.when(s >= n_f)
    def _():
        wd = wd_ref[...].astype(BF16)
        for gate_ref, y_ref, act_ref in zip(gate_refs, y_refs, act_refs):
            y = None
            for f in range(n_f):
                part = jnp.dot(act_ref[f], wd[f * tf:(f + 1) * tf, :], preferred_element_type=F32)
                y = part if y is None else y + part
            y_ref[0] = (y * gate_ref[0]).astype(y_ref.dtype)


def expert_ffn(sets, w_gate, w_up, w_down, layer, tf=512, tn=512):
    d, ff = w_gate.shape[2], w_gate.shape[3]
    n_f, n_d = ff // tf, d // tn
    d_idx = lambda s: jnp.maximum(s - n_f, 0)

    def up_tile(i, s):
        ahead = s >= n_f
        return (layer, jnp.where(ahead, jnp.minimum(i + 1, N_EXPERTS - 1), i), 0,
                jnp.where(ahead, 0, jnp.minimum(s, n_f - 1)))

    def down_tile(i, s):
        behind = s < n_f
        return (layer, jnp.where(behind, jnp.maximum(i - 1, 0), i), 0, jnp.where(behind, n_d - 1, s - n_f))
    in_specs, operands, out_specs, out_shape, act_scratch, row_scratch = [], [], [], [], [], []
    for xs, gates in sets:
        r = xs.shape[1]
        in_specs += [pl.BlockSpec((1, r, d // 2), lambda i, s: (i, 0, 0)),
                     pl.BlockSpec((1, r, 1), lambda i, s: (i, 0, 0))]
        operands += [xs, gates]
        out_specs.append(pl.BlockSpec((1, r, tn), lambda i, s: (i, 0, d_idx(s))))
        out_shape.append(jax.ShapeDtypeStruct((N_EXPERTS, r, d), BF16))
        act_scratch.append(pltpu.VMEM((n_f, r, tf), BF16))
        row_scratch.append(pltpu.VMEM((r, d), BF16))
    in_specs += [
        pl.BlockSpec((None, None, d, tf), up_tile),
        pl.BlockSpec((None, None, d, tf), up_tile),
        pl.BlockSpec((None, None, ff, tn), down_tile),
    ]
    return pl.pallas_call(
        functools.partial(_ffn_kernel, n_sets=len(sets), n_f=n_f),
        grid=(N_EXPERTS, n_f + n_d),
        in_specs=in_specs,
        out_specs=out_specs,
        out_shape=out_shape,
        scratch_shapes=act_scratch + row_scratch,
        compiler_params=_params(2),
        name="expert_ffn",
    )(*operands, w_gate, w_up, w_down)


def _combine_kernel(y_ref, slot_ref, x_ref, mod_ref, o_ref, onehot_ref, *, cap):
    @pl.when(pl.program_id(1) == 0)
    def _():
        for e in range(N_EXPERTS):
            onehot_ref[e * cap:(e + 1) * cap, :] = _onehot(slot_ref[e], cap)[1].astype(BF16)

    y = y_ref[...].reshape(N_EXPERTS * cap, y_ref.shape[2])
    moe = lax.dot_general(onehot_ref[...], y, (((0,), (0,)), ((), ())), preferred_element_type=F32)
    o_ref[0] = x_ref[0] + mod_ref[0] * moe


def combine(y, slot, x, gate_vec, cap, tn=512):
    b, n, d = x.shape
    e = N_EXPERTS
    return pl.pallas_call(
        functools.partial(_combine_kernel, cap=cap),
        grid=(b, d // tn),
        in_specs=[
            pl.BlockSpec((e, cap, tn), lambda i, j: (0, i, j)),
            pl.BlockSpec((e, 1, n), lambda i, j: (i, 0, 0)),
            pl.BlockSpec((1, n, tn), lambda i, j: (i, 0, j)),
            pl.BlockSpec((1, 1, tn), lambda i, j: (i, 0, j)),
        ],
        out_specs=pl.BlockSpec((1, n, tn), lambda i, j: (i, 0, j)),
        out_shape=jax.ShapeDtypeStruct((b, n, d), F32),
        scratch_shapes=[pltpu.VMEM((e * cap, n), BF16)],
        compiler_params=_params(2),
        name="combine",
    )(y, slot.reshape(b * e, 1, n), x, gate_vec.reshape(b, 1, d))


def moe_route(hw, aff):
    n = hw.shape[1]
    cap = CAPACITY_FACTOR * n // N_EXPERTS
    slot = select_tokens(aff, cap)
    return slot, cap, gather_tokens(hw, slot, aff, cap)


def mixer_block(x, h, attn, mod, n, layer, w_in, conv_w, ln_g, w_s, b_s, w_branch, w_out, g2, w_router, b_router):
    b = x.shape[0] // n
    p = matmul(h, w_in, layer, P_COL0, w_in.shape[2] - P_COL0)
    conv = short_conv(h, w_in, conv_w, layer, b, n)
    gm = chunk_gmlp(p, ln_g, w_s, b_s, layer)
    merged = merge_branches(attn, conv, gm, p, w_branch, layer)
    x_new, hw, aff = out_proj_router(merged, w_out, layer, x, mod, g2, w_router, b_router, n)
    return x_new.reshape(b, n, -1), hw.reshape(b, n, -1), aff

def kernel(x, c, ctx, c_ctx, w_mod, b_mod, norm1_g, w_in, na_rpb, conv_w, gmlp_ln_g, w_spatial, b_spatial,
           w_branch, w_out, norm2_g, w_router, b_router, w_e_gate, w_e_up, w_e_down, final_g):
    b, n, d = x.shape
    lc = ctx.shape[1]
    cvec = jnp.concatenate([c, c_ctx[None], jnp.zeros((8 - b - 1, d), F32)], axis=0)
    modv = modvec(cvec, w_mod, b_mod).reshape(DEPTH, 8, 6, d)
    cos_t, sin_t = rope_tables(n)
    bias = na_bias(na_rpb, n // GRID_W)
    mix_w = (w_in, conv_w, gmlp_ln_g, w_spatial, b_spatial, w_branch, w_out)
    ctx_s = ctx
    for layer in range(DEPTH):
        last = layer == DEPTH - 1
        mod = modv[layer, :b]
        mod_c = jnp.broadcast_to(modv[layer, b][None], (b, 6, d))
        layer_w = mix_w + (norm2_g[layer], w_router[layer], b_router[layer])

        hc = norm_mod(ctx_s, norm1_g[layer], mod_c[:, 0], mod_c[:, 1]).reshape(b * lc, d)
        if last:
            kvc = qkv_proj(hc, w_in, layer, lc, cos_t, sin_t, rope=False, first_tile=1, n_tiles=2)
            kc_col = 0
        else:
            kvc = qkv_proj(hc, w_in, layer, lc, cos_t, sin_t, rope=False)
            kc_col = D_BRANCH
            attn_c = ctx_attention(kvc, b, lc)
            ctx_mid, hw_c, aff_c = mixer_block(ctx_s.reshape(b * lc, d), hc, attn_c, mod_c, lc, layer, *layer_w)

        h = norm_mod(x, norm1_g[layer], mod[:, 0], mod[:, 1]).reshape(b * n, d)
        qkv = qkv_proj(h, w_in, layer, n, cos_t, sin_t, rope=True)
        attn = na_attention(qkv, kvc, kc_col, bias, layer, b, n)
        x, hw, aff = mixer_block(x.reshape(b * n, d), h, attn, mod, n, layer, *layer_w)

        slot, cap, rows = moe_route(hw, aff)
        if last:
            (y,) = expert_ffn([rows], w_e_gate, w_e_up, w_e_down, layer)
        else:
            slot_c, cap_c, rows_c = moe_route(hw_c, aff_c)
            y, y_c = expert_ffn([rows, rows_c], w_e_gate, w_e_up, w_e_down, layer)
            ctx_s = combine(y_c, slot_c, ctx_mid, mod_c[:, 5], cap_c)
        x = combine(y, slot, x, mod[:, 5], cap)
    return final_norm(x, final_g)
```

```python
import functools

import numpy as np
import jax
import jax.numpy as jnp
from jax import lax
from jax.experimental import pallas as pl
from jax.experimental.pallas import tpu as pltpu
from jax.experimental.pallas import tpu_sc as plsc

D_MODEL = 2048
DEPTH = 2
GRID_W = 64
D_BRANCH = D_MODEL // 2
N_BRANCHES = 3
HEAD_DIM = 128
N_HEADS = D_BRANCH // HEAD_DIM
NA_KH = 8
NA_KW = 16
ROPE_THETA = 10000.0
ROPE_AXIS_DIM = HEAD_DIM // 2
CHUNK = 128
GMLP_GROUPS = D_BRANCH // 128
N_EXPERTS = 16
CAPACITY_FACTOR = 2
NORM_EPS = 1e-6
NEG_INF = -1e30
LOG2_E = float(np.log2(np.e))

QKV_COLS = 3 * D_BRANCH
COL_XC, COL_BG, COL_CG = (QKV_COLS + i * D_BRANCH for i in range(3))
P_COL0 = QKV_COLS + 3 * D_BRANCH
COL_U, COL_VS, COL_GATE = 0, D_BRANCH, 2 * D_BRANCH

NA_Q_ROWS = 4
NA_BAND_ROWS = 12
NA_TQ = NA_Q_ROWS * GRID_W
NA_TK = NA_BAND_ROWS * GRID_W
NA_HEADS_PER_STEP = N_HEADS
NA_HB = NA_HEADS_PER_STEP * HEAD_DIM
NA_SCORE_LEAD = 3

V7X_VMEM_LIMIT = 56 * 1024 * 1024
V7X_SC_CORES, V7X_SC_SUBCORES, V7X_SC_LANES = 2, 16, 16
SC_GATHER_ROWS = 64

BF16 = jnp.bfloat16
F32 = jnp.float32


def _params(n_axes, vmem=V7X_VMEM_LIMIT):
    return pltpu.CompilerParams(dimension_semantics=("arbitrary",) * n_axes, vmem_limit_bytes=vmem)


def _modvec_kernel(c_ref, w_ref, b_ref, o_ref):
    s = jax.nn.silu(c_ref[...]).astype(BF16)
    o_ref[0] = jnp.dot(s, w_ref[0].astype(BF16), preferred_element_type=F32) + b_ref[0]


def modvec(cvec, w_mod, b_mod, tn=1024):
    nl, d, n6 = w_mod.shape
    return pl.pallas_call(
        _modvec_kernel,
        grid=(nl, n6 // tn),
        in_specs=[
            pl.BlockSpec((8, d), lambda l, j: (0, 0)),
            pl.BlockSpec((1, d, tn), lambda l, j: (l, 0, j)),
            pl.BlockSpec((1, 1, tn), lambda l, j: (l, 0, j)),
        ],
        out_specs=pl.BlockSpec((1, 8, tn), lambda l, j: (l, 0, j)),
        out_shape=jax.ShapeDtypeStruct((nl, 8, n6), F32),
        compiler_params=_params(2),
        name="modvec",
    )(cvec, w_mod, b_mod.reshape(nl, 1, n6))


def _rms(x, g):
    ms = jnp.mean(x * x, axis=-1, keepdims=True)
    return x * lax.rsqrt(ms + NORM_EPS) * g


def _norm_mod_kernel(x_ref, g_ref, sh_ref, sc_ref, o_ref):
    y = _rms(x_ref[0], g_ref[...])
    o_ref[0] = (y * (1 + sc_ref[0]) + sh_ref[0]).astype(o_ref.dtype)


def _norm_kernel(x_ref, g_ref, o_ref):
    o_ref[0] = _rms(x_ref[0], g_ref[...]).astype(o_ref.dtype)


def norm_mod(x, g, shift, scale, out_dtype=BF16, tm=1024):
    b, n, d = x.shape
    tm = min(tm, n)
    return pl.pallas_call(
        _norm_mod_kernel,
        grid=(b, n // tm),
        in_specs=[
            pl.BlockSpec((1, tm, d), lambda i, j: (i, j, 0)),
            pl.BlockSpec((1, d), lambda i, j: (0, 0)),
            pl.BlockSpec((1, 1, d), lambda i, j: (i, 0, 0)),
            pl.BlockSpec((1, 1, d), lambda i, j: (i, 0, 0)),
        ],
        out_specs=pl.BlockSpec((1, tm, d), lambda i, j: (i, j, 0)),
        out_shape=jax.ShapeDtypeStruct((b, n, d), out_dtype),
        compiler_params=_params(2),
        name="norm_mod",
    )(x, g.reshape(1, d), shift.reshape(b, 1, d), scale.reshape(b, 1, d))


def final_norm(x, g, tm=1024):
    b, n, d = x.shape
    return pl.pallas_call(
        _norm_kernel,
        grid=(b, n // tm),
        in_specs=[
            pl.BlockSpec((1, tm, d), lambda i, j: (i, j, 0)),
            pl.BlockSpec((1, d), lambda i, j: (0, 0)),
        ],
        out_specs=pl.BlockSpec((1, tm, d), lambda i, j: (i, j, 0)),
        out_shape=jax.ShapeDtypeStruct((b, n, d), F32),
        compiler_params=_params(2),
        name="final_norm",
    )(x, g.reshape(1, d))


def _mm_kernel(x_ref, w_ref, o_ref, wbf_ref):
    @pl.when(pl.program_id(1) == 0)
    def _():
        wbf_ref[...] = w_ref[...].astype(BF16)

    o_ref[...] = jnp.dot(x_ref[...], wbf_ref[...], preferred_element_type=F32).astype(o_ref.dtype)


def matmul(x, w, layer, col_off, n_cols, out_dtype=F32, tm=1024, tn=1024):
    m, k = x.shape
    tm = min(tm, m)
    off = col_off // tn
    return pl.pallas_call(
        _mm_kernel,
        grid=(n_cols // tn, m // tm),
        in_specs=[
            pl.BlockSpec((tm, k), lambda j, i: (i, 0)),
            pl.BlockSpec((None, k, tn), lambda j, i: (layer, 0, off + j)),
        ],
        out_specs=pl.BlockSpec((tm, tn), lambda j, i: (i, j)),
        out_shape=jax.ShapeDtypeStruct((m, n_cols), out_dtype),
        scratch_shapes=[pltpu.VMEM((k, tn), BF16)],
        compiler_params=_params(2),
        name="matmul",
    )(x, w)


def _qkv_kernel(x_ref, w_ref, cos_ref, sin_ref, o_ref, wbf_ref, *, rope):
    @pl.when(pl.program_id(1) == 0)
    def _():
        wbf_ref[...] = w_ref[...].astype(BF16)

    def plain():
        o_ref[...] = jnp.dot(x_ref[...], wbf_ref[...], preferred_element_type=F32).astype(BF16)

    if not rope:
        plain()
        return
    c = pl.program_id(0)

    @pl.when(c < 2)
    def _():
        cs = cos_ref[...]
        sn = sin_ref[...]
        lane = lax.broadcasted_iota(jnp.int32, cs.shape, 1)
        first_half = (lane % ROPE_AXIS_DIM) < (ROPE_AXIS_DIM // 2)
        for pair in range(N_HEADS // 2):
            cols = slice(pair * 2 * HEAD_DIM, (pair + 1) * 2 * HEAD_DIM)
            acc = jnp.dot(x_ref[...], wbf_ref[:, cols], preferred_element_type=F32)
            for h in range(2):
                t = acc[:, h * HEAD_DIM:(h + 1) * HEAD_DIM]
                partner = jnp.where(first_half,
                                    pltpu.roll(t, HEAD_DIM - ROPE_AXIS_DIM // 2, 1),
                                    pltpu.roll(t, ROPE_AXIS_DIM // 2, 1))
                lo = (pair * 2 + h) * HEAD_DIM
                o_ref[:, lo:lo + HEAD_DIM] = (t * cs + partner * sn).astype(BF16)

    @pl.when(c == 2)
    def _():
        plain()


def rope_tables(n):
    pos = jnp.arange(n)
    rc = jnp.stack([pos // GRID_W, pos % GRID_W], axis=-1).astype(F32)
    inv = ROPE_THETA ** (-jnp.arange(0, ROPE_AXIS_DIM, 2, dtype=F32) / ROPE_AXIS_DIM)
    ang = rc[:, :, None] * inv
    cos, sin = jnp.cos(ang), jnp.sin(ang)
    cos_t = jnp.concatenate([cos, cos], axis=-1).reshape(n, HEAD_DIM)
    sin_t = jnp.concatenate([-sin, sin], axis=-1).reshape(n, HEAD_DIM)
    return cos_t, sin_t


def qkv_proj(x, w, layer, n, cos_t, sin_t, rope, first_tile=0, n_tiles=3, tm=1024):
    m, k = x.shape
    tm = min(tm, n if rope else m)
    nt = n // tm if rope else 1
    tab = pl.BlockSpec((tm, HEAD_DIM), lambda c, i: (i % nt, 0))
    return pl.pallas_call(
        functools.partial(_qkv_kernel, rope=rope),
        grid=(n_tiles, m // tm),
        in_specs=[
            pl.BlockSpec((tm, k), lambda c, i: (i, 0)),
            pl.BlockSpec((None, k, D_BRANCH), lambda c, i: (layer, 0, first_tile + c)),
            tab, tab,
        ],
        out_specs=pl.BlockSpec((tm, D_BRANCH), lambda c, i: (i, c)),
        out_shape=jax.ShapeDtypeStruct((m, n_tiles * D_BRANCH), BF16),
        scratch_shapes=[pltpu.VMEM((k, D_BRANCH), BF16)],
        compiler_params=_params(2),
        name="qkv_proj",
    )(x, w, cos_t, sin_t)


def _nt_dot(a, b):
    return lax.dot_general(a, b, (((1,), (1,)), ((), ())), preferred_element_type=F32)


def _na_pattern_rows(g, rows):
    band_row = int(np.clip(g * NA_Q_ROWS - NA_KH // 2, 0, rows - NA_BAND_ROWS))
    qr = g * NA_Q_ROWS + np.arange(NA_Q_ROWS)
    rs = np.clip(qr - NA_KH // 2, 0, rows - NA_KH)
    return band_row, qr, rs


def _na_bias_kernel(rpb_ref, o_ref, *, rows):
    shape = (GRID_W, 2 * GRID_W)
    qc = lax.broadcasted_iota(jnp.int32, shape, 0)
    lane = lax.broadcasted_iota(jnp.int32, shape, 1)
    low = lane < GRID_W
    kc = lane % GRID_W
    ws = jnp.clip(qc - NA_KW // 2, 0, GRID_W - NA_KW)
    col_ok = (kc >= ws) & (kc < ws + NA_KW)
    neg = jnp.full(shape, NEG_INF, F32)

    tiles = []
    for dr in range(2 * NA_KH - 1):
        t = jnp.broadcast_to(rpb_ref[0, 0, dr:dr + 1, :], shape)
        t = pltpu.roll(t, 2 * GRID_W - (NA_KW - 1), 1)
        for bit in range(6):
            t = jnp.where(((qc >> bit) & 1) == 1, pltpu.roll(t, 1 << bit, 1), t)
        t = jnp.where(low, t, pltpu.roll(t, GRID_W, 1))
        tiles.append(jnp.where(col_ok, t * LOG2_E, NEG_INF))

    n_groups = rows // NA_Q_ROWS
    for p, g in enumerate((0, 1, n_groups - 1)):
        band_row, qr, rs = _na_pattern_rows(g, rows)
        for qi in range(NA_Q_ROWS):
            def half(krl):
                kr = band_row + krl
                if rs[qi] <= kr < rs[qi] + NA_KH:
                    return tiles[kr - qr[qi] + NA_KH - 1]
                return neg
            for pair in range(NA_BAND_ROWS // 2):
                blk = jnp.where(low, half(2 * pair), half(2 * pair + 1))
                o_ref[0, p, 0, qi * GRID_W:(qi + 1) * GRID_W, pair * 128:(pair + 1) * 128] = blk


def na_bias(na_rpb, rows):
    nl, nh, ndr, ndc = na_rpb.shape
    band1, qr1, rs1 = _na_pattern_rows(1, rows)
    for g in range(2, rows // NA_Q_ROWS - 1):
        band, qr, rs = _na_pattern_rows(g, rows)
        assert (qr - band == qr1 - band1).all() and (rs - band == rs1 - band1).all()
    rpb = jnp.pad(na_rpb, ((0, 0), (0, 0), (0, 16 - ndr), (0, 128 - ndc)))
    return pl.pallas_call(
        functools.partial(_na_bias_kernel, rows=rows),
        grid=(nl, nh),
        in_specs=[pl.BlockSpec((1, 1, 16, 128), lambda l, h: (l, h, 0, 0))],
        out_specs=pl.BlockSpec((1, 3, 1, NA_TQ, NA_TK), lambda l, h: (l, 0, h, 0, 0)),
        out_shape=jax.ShapeDtypeStruct((nl, 3, nh, NA_TQ, NA_TK), F32),
        compiler_params=_params(2),
        name="na_bias",
    )(rpb)


def _na_kernel(q_ref, k_ref, v_ref, kc_ref, vc_ref, bias_ref, o_ref):
    g = pl.program_id(2)
    n_groups = k_ref.shape[1] // NA_TQ
    band_row = jnp.clip(g * NA_Q_ROWS - NA_KH // 2, 0, n_groups * NA_Q_ROWS - NA_BAND_ROWS)
    start = pl.multiple_of(band_row * GRID_W, NA_TQ)
    pat = jnp.where(g == 0, 0, jnp.where(g == n_groups - 1, 2, 1))
    scale = HEAD_DIM ** -0.5 * LOG2_E

    def scores(hh):
        cols = slice(hh * HEAD_DIM, (hh + 1) * HEAD_DIM)
        q = q_ref[0, :, cols]
        return _nt_dot(q, k_ref[0, pl.ds(start, NA_TK), cols]), _nt_dot(q, kc_ref[0, :, cols])

    def finish(hh, qk_loc, qk_ctx):
        cols = slice(hh * HEAD_DIM, (hh + 1) * HEAD_DIM)
        s_loc = qk_loc * scale + bias_ref[0, pat, hh]
        s_ctx = qk_ctx * scale
        m = jnp.maximum(jnp.max(s_loc, axis=1, keepdims=True), jnp.max(s_ctx, axis=1, keepdims=True))
        e_loc = jnp.exp2(s_loc - m)
        e_ctx = jnp.exp2(s_ctx - m)
        inv = 1.0 / (jnp.sum(e_loc, axis=1, keepdims=True) + jnp.sum(e_ctx, axis=1, keepdims=True))
        o = jnp.dot(e_loc.astype(BF16), v_ref[0, pl.ds(start, NA_TK), cols], preferred_element_type=F32)
        o = o + jnp.dot(e_ctx.astype(BF16), vc_ref[0, :, cols], preferred_element_type=F32)
        o_ref[0, :, cols] = (o * inv).astype(o_ref.dtype)

    qk = {}
    for hh in range(NA_HEADS_PER_STEP + NA_SCORE_LEAD):
        if hh < NA_HEADS_PER_STEP:
            qk[hh] = scores(hh)
        if hh >= NA_SCORE_LEAD:
            finish(hh - NA_SCORE_LEAD, *qk.pop(hh - NA_SCORE_LEAD))


def na_attention(qkv, kvc, kc_col, bias, layer, b, n):
    lc = kvc.shape[0] // b
    qkv3 = qkv.reshape(b, n, QKV_COLS)
    kvc3 = kvc.reshape(b, lc, kvc.shape[1])
    hbs = D_BRANCH // NA_HB
    kcb = kc_col // NA_HB
    out = pl.pallas_call(
        _na_kernel,
        grid=(hbs, b, n // NA_TQ),
        in_specs=[
            pl.BlockSpec((1, NA_TQ, NA_HB), lambda h, i, g: (i, g, h)),
            pl.BlockSpec((1, n, NA_HB), lambda h, i, g: (i, 0, hbs + h)),
            pl.BlockSpec((1, n, NA_HB), lambda h, i, g: (i, 0, 2 * hbs + h)),
            pl.BlockSpec((1, lc, NA_HB), lambda h, i, g: (i, 0, kcb + h)),
            pl.BlockSpec((1, lc, NA_HB), lambda h, i, g: (i, 0, kcb + hbs + h)),
            pl.BlockSpec((1, 3, NA_HEADS_PER_STEP, NA_TQ, NA_TK), lambda h, i, g: (layer, 0, h, 0, 0),
                         pipeline_mode=pl.Buffered(1)),
        ],
        out_specs=pl.BlockSpec((1, NA_TQ, NA_HB), lambda h, i, g: (i, g, h)),
        out_shape=jax.ShapeDtypeStruct((b, n, D_BRANCH), BF16),
        compiler_params=_params(3),
        name="na_attention",
    )(qkv3, qkv3, qkv3, kvc3, kvc3, bias)
    return out.reshape(b * n, D_BRANCH)


def _ctx_attn_kernel(q_ref, k_ref, v_ref, o_ref):
    for h in range(N_HEADS):
        cols = slice(h * HEAD_DIM, (h + 1) * HEAD_DIM)
        s = _nt_dot(q_ref[0, :, cols], k_ref[0, :, cols]) * (HEAD_DIM ** -0.5)
        m = jnp.max(s, axis=1, keepdims=True)
        e = jnp.exp(s - m)
        p = e * (1.0 / jnp.sum(e, axis=1, keepdims=True))
        o_ref[0, :, cols] = jnp.dot(p.astype(BF16), v_ref[0, :, cols],
                                    preferred_element_type=F32).astype(o_ref.dtype)


def ctx_attention(qkv, b, lc):
    qkv3 = qkv.reshape(b, lc, QKV_COLS)
    part = lambda k: pl.BlockSpec((1, lc, D_BRANCH), lambda i: (i, 0, k))
    out = pl.pallas_call(
        _ctx_attn_kernel,
        grid=(b,),
        in_specs=[part(0), part(1), part(2)],
        out_specs=part(0),
        out_shape=jax.ShapeDtypeStruct((b, lc, D_BRANCH), BF16),
        compiler_params=_params(1),
        name="ctx_attention",
    )(qkv3, qkv3, qkv3)
    return out.reshape(b * lc, D_BRANCH)


def _conv_kernel(h_ref, wx_ref, wb_ref, wc_ref, w_ref, o_ref, wbf_ref, *, seq):
    @pl.when(pl.program_id(1) == 0)
    def _():
        for k, ref in enumerate((wx_ref, wb_ref, wc_ref)):
            wbf_ref[k] = ref[...].astype(BF16)

    h = h_ref[0]
    xc, bg, cg = (jnp.dot(h, wbf_ref[k], preferred_element_type=F32) for k in range(3))
    z = cg * xc
    rows = z.shape[0]
    pos = lax.broadcasted_iota(jnp.int32, z.shape, 0) % seq
    z_prev = jnp.where(pos == 0, 0.0, pltpu.roll(z, 1, 0))
    z_next = jnp.where(pos == seq - 1, 0.0, pltpu.roll(z, rows - 1, 0))
    y = z_prev * w_ref[0, 0:1, :] + z * w_ref[0, 1:2, :] + z_next * w_ref[0, 2:3, :]
    o_ref[0] = (bg * y).astype(o_ref.dtype)


def short_conv(h, w_in, conv_w, layer, n, tc=256, rows=2048):
    m, d = h.shape
    rows = min(max(rows, n), m)
    assert rows % n == 0 and m % rows == 0
    wcol = lambda col: pl.BlockSpec((None, d, tc), lambda j, i: (layer, 0, col // tc + j))
    out = pl.pallas_call(
        functools.partial(_conv_kernel, seq=n),
        grid=(D_BRANCH // tc, m // rows),
        in_specs=[pl.BlockSpec((1, rows, d), lambda j, i: (i, 0, 0)),
                  wcol(COL_XC), wcol(COL_BG), wcol(COL_CG),
                  pl.BlockSpec((1, 3, tc), lambda j, i: (layer, 0, j))],
        out_specs=pl.BlockSpec((1, rows, tc), lambda j, i: (i, 0, j)),
        out_shape=jax.ShapeDtypeStruct((m // rows, rows, D_BRANCH), BF16),
        scratch_shapes=[pltpu.VMEM((3, d, tc), BF16)],
        compiler_params=_params(2),
        name="short_conv",
    )(h.reshape(m // rows, rows, d), w_in, w_in, w_in, conv_w)
    return out.reshape(m, D_BRANCH)


def _gmlp_kernel(u_ref, v_ref, g_ref, ws_ref, bs_ref, o_ref):
    v = jax.nn.gelu(v_ref[...])
    mu = jnp.mean(v, axis=-1, keepdims=True)
    var = jnp.mean(jnp.square(v - mu), axis=-1, keepdims=True)
    vn = ((v - mu) * lax.rsqrt(var + NORM_EPS) * g_ref[...]).astype(BF16)
    for grp in range(GMLP_GROUPS):
        cols = slice(grp * 128, (grp + 1) * 128)
        ws = ws_ref[0, grp].astype(BF16)
        for ch in range(v.shape[0] // CHUNK):
            rows = slice(ch * CHUNK, (ch + 1) * CHUNK)
            s = jnp.dot(ws, vn[rows, cols], preferred_element_type=F32) + bs_ref[:, grp:grp + 1]
            o_ref[rows, cols] = (jax.nn.gelu(u_ref[rows, cols]) * s).astype(o_ref.dtype)


def chunk_gmlp(p, ln_g, w_s, b_s, layer, tm=8 * CHUNK):
    m = p.shape[0]
    return pl.pallas_call(
        _gmlp_kernel,
        grid=(m // tm,),
        in_specs=[
            pl.BlockSpec((tm, D_BRANCH), lambda i: (i, COL_U // D_BRANCH)),
            pl.BlockSpec((tm, D_BRANCH), lambda i: (i, COL_VS // D_BRANCH)),
            pl.BlockSpec((1, D_BRANCH), lambda i: (0, 0)),
            pl.BlockSpec((1, GMLP_GROUPS, CHUNK, CHUNK), lambda i: (layer, 0, 0, 0)),
            pl.BlockSpec((CHUNK, GMLP_GROUPS), lambda i: (0, 0)),
        ],
        out_specs=pl.BlockSpec((tm, D_BRANCH), lambda i: (i, 0)),
        out_shape=jax.ShapeDtypeStruct((m, D_BRANCH), BF16),
        compiler_params=_params(1),
        name="chunk_gmlp",
    )(p, p, ln_g[layer].reshape(1, D_BRANCH), w_s, b_s[layer].T)


def _merge_kernel(a_ref, c_ref, m_ref, g0_ref, g1_ref, g2_ref, w_ref, o_ref, wbf_ref):
    @pl.when(pl.program_id(1) == 0)
    def _():
        wbf_ref[...] = w_ref[...].astype(BF16)

    acc = None
    for i, (br, gl) in enumerate(((a_ref, g0_ref), (c_ref, g1_ref), (m_ref, g2_ref))):
        proj = jnp.dot(br[...], wbf_ref[i], preferred_element_type=F32)
        term = jax.nn.sigmoid(gl[...]) * proj
        acc = term if acc is None else acc + term
    o_ref[...] = acc.astype(o_ref.dtype)


def merge_branches(attn, conv, gm, p, w_branch, layer, tm=512, tn=1024):
    m = attn.shape[0]
    br = pl.BlockSpec((tm, D_BRANCH), lambda j, i: (i, 0))
    gate = lambda k: pl.BlockSpec((tm, tn), lambda j, i: (i, (COL_GATE + k * D_MODEL) // tn + j))
    return pl.pallas_call(
        _merge_kernel,
        grid=(D_MODEL // tn, m // tm),
        in_specs=[br, br, br, gate(0), gate(1), gate(2),
                  pl.BlockSpec((None, N_BRANCHES, D_BRANCH, tn), lambda j, i: (layer, 0, 0, j))],
        out_specs=pl.BlockSpec((tm, tn), lambda j, i: (i, j)),
        out_shape=jax.ShapeDtypeStruct((m, D_MODEL), BF16),
        scratch_shapes=[pltpu.VMEM((N_BRANCHES, D_BRANCH, tn), BF16)],
        compiler_params=_params(2),
        name="merge_branches",
    )(attn, conv, gm, p, p, p, w_branch)


def _out_proj_router_kernel(h_ref, w_ref, x_ref, mod_ref, g_ref, sh_ref, sc_ref, wr_ref, br_ref,
                            o_ref, hw_ref, aff_ref, wbf_ref):
    @pl.when(pl.program_id(0) == 0)
    def _():
        wbf_ref[...] = w_ref[...].astype(BF16)

    x = x_ref[...] + mod_ref[0] * jnp.dot(h_ref[...], wbf_ref[...], preferred_element_type=F32)
    o_ref[...] = x
    y = _rms(x, g_ref[...])
    h = (y * (1 + sc_ref[0]) + sh_ref[0]).astype(BF16)
    bits = pltpu.bitcast(h.astype(F32), jnp.uint32)
    half = bits.shape[1] // 2
    hw_ref[...] = lax.shift_right_logical(bits[:, :half], jnp.uint32(16)) | bits[:, half:]
    logits = _nt_dot(wr_ref[...].astype(BF16), h) + br_ref[...]
    mx = jnp.max(logits, axis=0, keepdims=True)
    e = jnp.exp(logits - mx)
    aff_ref[0] = e / jnp.sum(e, axis=0, keepdims=True)


def out_proj_router(h, w_out, layer, x, mod, g, w_router, b_router, n, tm=512):
    m, d = x.shape
    b = m // n
    tm = min(tm, n)
    per_b = n // tm
    vec = lambda k: (pl.BlockSpec((1, 1, d), lambda i: (i // per_b, 0, 0)), mod[:, k].reshape(b, 1, d))
    (gate_s, gate_v), (sh_s, sh_v), (sc_s, sc_v) = vec(2), vec(3), vec(4)
    return pl.pallas_call(
        _out_proj_router_kernel,
        grid=(m // tm,),
        in_specs=[
            pl.BlockSpec((tm, d), lambda i: (i, 0)),
            pl.BlockSpec((None, d, d), lambda i: (layer, 0, 0), pipeline_mode=pl.Buffered(1)),
            pl.BlockSpec((tm, d), lambda i: (i, 0)),
            gate_s,
            pl.BlockSpec((1, d), lambda i: (0, 0)),
            sh_s, sc_s,
            pl.BlockSpec((N_EXPERTS, d), lambda i: (0, 0)),
            pl.BlockSpec((N_EXPERTS, 1), lambda i: (0, 0)),
        ],
        out_specs=[
            pl.BlockSpec((tm, d), lambda i: (i, 0)),
            pl.BlockSpec((tm, d // 2), lambda i: (i, 0)),
            pl.BlockSpec((1, N_EXPERTS, tm), lambda i: (i // per_b, 0, i % per_b)),
        ],
        out_shape=[jax.ShapeDtypeStruct((m, d), F32),
                   jax.ShapeDtypeStruct((m, d // 2), jnp.uint32),
                   jax.ShapeDtypeStruct((b, N_EXPERTS, n), F32)],
        scratch_shapes=[pltpu.VMEM((d, d), BF16)],
        compiler_params=_params(1),
        name="out_proj_router",
    )(h, w_out, x, gate_v, g.reshape(1, d), sh_v, sc_v, w_router.T, b_router.reshape(N_EXPERTS, 1))


def _select_kernel(aff_ref, tri_ref, slot_ref, *, cap):
    bits = pltpu.bitcast(aff_ref[...], jnp.int32)

    def step(i, t):
        cand = t | lax.shift_left(jnp.int32(1), 30 - i)
        cnt = jnp.sum(jnp.where(bits >= cand, 1.0, 0.0), axis=1, keepdims=True)
        return jnp.where(cnt >= cap, cand, t)

    t = lax.fori_loop(0, 31, step, jnp.zeros((bits.shape[0], 1), jnp.int32))
    gt = bits > t
    eq = bits == t
    need = cap - jnp.sum(jnp.where(gt, 1.0, 0.0), axis=1, keepdims=True)
    tri = tri_ref[...]
    eq_rank = jnp.dot(jnp.where(eq, 1.0, 0.0).astype(BF16), tri, preferred_element_type=F32)
    sel = gt | (eq & (eq_rank < need))
    slot = jnp.dot(jnp.where(sel, 1.0, 0.0).astype(BF16), tri, preferred_element_type=F32)
    slot_ref[...] = jnp.where(sel, slot.astype(jnp.int32), -1)


def select_tokens(aff, cap):
    b, e, n = aff.shape
    tri = jnp.triu(jnp.ones((n, n), BF16), k=1)
    rows = pl.BlockSpec((b * e, n), lambda i: (0, 0))
    return pl.pallas_call(
        functools.partial(_select_kernel, cap=cap),
        grid=(1,),
        in_specs=[rows, pl.BlockSpec((n, n), lambda i: (0, 0))],
        out_specs=rows,
        out_shape=jax.ShapeDtypeStruct((b * e, n), jnp.int32),
        compiler_params=_params(1),
        name="select_tokens",
    )(aff.reshape(b * e, n), tri).reshape(b, e, n)


def _onehot(slot, cap):
    hit = lax.broadcasted_iota(jnp.int32, (cap, slot.shape[1]), 0) == slot
    return hit, jnp.where(hit, 1.0, 0.0)


def gather_tokens(h, slot, aff, cap):
    b, n, w = h.shape
    e = N_EXPERTS
    workers = V7X_SC_CORES * V7X_SC_SUBCORES
    rows_per_worker = (b * e) // workers
    chunk = min(cap, SC_GATHER_ROWS)
    assert (b * e) % workers == 0 and cap % chunk == 0 and n % V7X_SC_LANES == 0

    def body(h_hbm, slot_hbm, aff_hbm, xs_hbm, gates_hbm, slot_v, aff_v, idx_v, gate_v, rows_v, sem):
        worker = lax.axis_index("s") * V7X_SC_CORES + lax.axis_index("c")
        lane = lax.iota(jnp.int32, V7X_SC_LANES)

        @pl.loop(0, rows_per_worker)
        def _(j):
            r = worker * rows_per_worker + j
            bi = r // e
            ei = r - bi * e
            pltpu.sync_copy(slot_hbm.at[r], slot_v)
            pltpu.sync_copy(aff_hbm.at[r], aff_v)

            @pl.loop(0, n // V7X_SC_LANES)
            def _(c):
                tokens = pl.ds(c * V7X_SC_LANES, V7X_SC_LANES)
                s = slot_v[tokens]
                chosen = s >= 0
                plsc.store_scatter(idx_v, [s], lane + (c * V7X_SC_LANES + bi * n), mask=chosen)
                plsc.store_scatter(gate_v, [s], aff_v[tokens], mask=chosen)

            out_row = (ei * b + bi) * cap

            @pl.loop(0, cap // chunk)
            def _(q):
                pltpu.async_copy(h_hbm.at[idx_v.at[pl.ds(q * chunk, chunk)]], rows_v, sem).wait()
                pltpu.sync_copy(rows_v, xs_hbm.at[pl.ds(out_row + q * chunk, chunk)])

            pltpu.sync_copy(gate_v, gates_hbm.at[pl.ds(out_row, cap)])

    xs, gates = pl.kernel(
        body,
        mesh=plsc.VectorSubcoreMesh(core_axis_name="c", subcore_axis_name="s"),
        out_type=[jax.ShapeDtypeStruct((e * b * cap, w), h.dtype),
                  jax.ShapeDtypeStruct((e * b * cap,), F32)],
        scratch_types=[
            pltpu.VMEM((n,), jnp.int32), pltpu.VMEM((n,), F32),
            pltpu.VMEM((cap,), jnp.int32), pltpu.VMEM((cap,), F32),
            pltpu.VMEM((chunk, w), h.dtype),
            pltpu.SemaphoreType.DMA,
        ],
        compiler_params=pltpu.CompilerParams(needs_layout_passes=False),
        name="gather_tokens",
    )(h.reshape(b * n, w), slot.reshape(b * e, n), aff.reshape(b * e, n))
    return xs.reshape(e, b * cap, w), gates.reshape(e, b * cap, 1)


def _ffn_kernel(*refs, n_sets, n_f):
    xs_refs, gate_refs = refs[0:2 * n_sets:2], refs[1:2 * n_sets:2]
    wg_ref, wu_ref, wd_ref = refs[2 * n_sets:2 * n_sets + 3]
    y_refs = refs[2 * n_sets + 3:3 * n_sets + 3]
    act_refs = refs[3 * n_sets + 3:4 * n_sets + 3]
    row_refs = refs[4 * n_sets + 3:]
    s = pl.program_id(1)
    tf = wg_ref.shape[1]

    @pl.when(s == 0)
    def _():
        for xs_ref, row_ref in zip(xs_refs, row_refs):
            words = xs_ref[0]
            half = words.shape[1]
            row_ref[:, :half] = pltpu.bitcast(lax.shift_left(words, jnp.uint32(16)), F32).astype(BF16)
            row_ref[:, half:] = pltpu.bitcast(words & jnp.uint32(0xFFFF0000), F32).astype(BF16)

    @pl.when(s < n_f)
    def _():
        wg = wg_ref[...].astype(BF16)
        wu = wu_ref[...].astype(BF16)
        for row_ref, act_ref in zip(row_refs, act_refs):
            xs = row_ref[...]
            hg = jnp.dot(xs, wg, preferred_element_type=F32)
            hu = jnp.dot(xs, wu, preferred_element_type=F32)
            act_ref[s] = (jax.nn.silu(hg) * hu).astype(BF16)

    @pl.when(s >= n_f)
    def _():
        wd = wd_ref[...].astype(BF16)
        for gate_ref, y_ref, act_ref in zip(gate_refs, y_refs, act_refs):
            y = None
            for f in range(n_f):
                part = jnp.dot(act_ref[f], wd[f * tf:(f + 1) * tf, :], preferred_element_type=F32)
                y = part if y is None else y + part
            y_ref[0] = (y * gate_ref[0]).astype(y_ref.dtype)


def expert_ffn(sets, w_gate, w_up, w_down, layer, tf=512, tn=512):
    d, ff = w_gate.shape[2], w_gate.shape[3]
    n_f, n_d = ff // tf, d // tn
    d_idx = lambda s: jnp.maximum(s - n_f, 0)

    def up_tile(i, s):
        ahead = s >= n_f
        return (layer, jnp.where(ahead, jnp.minimum(i + 1, N_EXPERTS - 1), i), 0,
                jnp.where(ahead, 0, jnp.minimum(s, n_f - 1)))

    def down_tile(i, s):
        behind = s < n_f
        return (layer, jnp.where(behind, jnp.maximum(i - 1, 0), i), 0, jnp.where(behind, n_d - 1, s - n_f))
    in_specs, operands, out_specs, out_shape, act_scratch, row_scratch = [], [], [], [], [], []
    for xs, gates in sets:
        r = xs.shape[1]
        in_specs += [pl.BlockSpec((1, r, d // 2), lambda i, s: (i, 0, 0)),
                     pl.BlockSpec((1, r, 1), lambda i, s: (i, 0, 0))]
        operands += [xs, gates]
        out_specs.append(pl.BlockSpec((1, r, tn), lambda i, s: (i, 0, d_idx(s))))
        out_shape.append(jax.ShapeDtypeStruct((N_EXPERTS, r, d), BF16))
        act_scratch.append(pltpu.VMEM((n_f, r, tf), BF16))
        row_scratch.append(pltpu.VMEM((r, d), BF16))
    in_specs += [
        pl.BlockSpec((None, None, d, tf), up_tile),
        pl.BlockSpec((None, None, d, tf), up_tile),
        pl.BlockSpec((None, None, ff, tn), down_tile),
    ]
    return pl.pallas_call(
        functools.partial(_ffn_kernel, n_sets=len(sets), n_f=n_f),
        grid=(N_EXPERTS, n_f + n_d),
        in_specs=in_specs,
        out_specs=out_specs,
        out_shape=out_shape,
        scratch_shapes=act_scratch + row_scratch,
        compiler_params=_params(2),
        name="expert_ffn",
    )(*operands, w_gate, w_up, w_down)


def _combine_kernel(y_ref, slot_ref, x_ref, mod_ref, o_ref, onehot_ref, *, cap):
    @pl.when(pl.program_id(1) == 0)
    def _():
        for e in range(N_EXPERTS):
            onehot_ref[e * cap:(e + 1) * cap, :] = _onehot(slot_ref[e], cap)[1].astype(BF16)

    y = y_ref[...].reshape(N_EXPERTS * cap, y_ref.shape[2])
    moe = lax.dot_general(onehot_ref[...], y, (((0,), (0,)), ((), ())), preferred_element_type=F32)
    o_ref[0] = x_ref[0] + mod_ref[0] * moe


def combine(y, slot, x, gate_vec, cap, tn=512):
    b, n, d = x.shape
    e = N_EXPERTS
    return pl.pallas_call(
        functools.partial(_combine_kernel, cap=cap),
        grid=(b, d // tn),
        in_specs=[
            pl.BlockSpec((e, cap, tn), lambda i, j: (0, i, j)),
            pl.BlockSpec((e, 1, n), lambda i, j: (i, 0, 0)),
            pl.BlockSpec((1, n, tn), lambda i, j: (i, 0, j)),
            pl.BlockSpec((1, 1, tn), lambda i, j: (i, 0, j)),
        ],
        out_specs=pl.BlockSpec((1, n, tn), lambda i, j: (i, 0, j)),
        out_shape=jax.ShapeDtypeStruct((b, n, d), F32),
        scratch_shapes=[pltpu.VMEM((e * cap, n), BF16)],
        compiler_params=_params(2),
        name="combine",
    )(y, slot.reshape(b * e, 1, n), x, gate_vec.reshape(b, 1, d))


def moe_route(hw, aff):
    n = hw.shape[1]
    cap = CAPACITY_FACTOR * n // N_EXPERTS
    slot = select_tokens(aff, cap)
    return slot, cap, gather_tokens(hw, slot, aff, cap)


def mixer_block(x, h, attn, mod, n, layer, w_in, conv_w, ln_g, w_s, b_s, w_branch, w_out, g2, w_router, b_router):
    b = x.shape[0] // n
    p = matmul(h, w_in, layer, P_COL0, w_in.shape[2] - P_COL0)
    conv = short_conv(h, w_in, conv_w, layer, n)
    gm = chunk_gmlp(p, ln_g, w_s, b_s, layer)
    merged = merge_branches(attn, conv, gm, p, w_branch, layer)
    x_new, hw, aff = out_proj_router(merged, w_out, layer, x, mod, g2, w_router, b_router, n)
    return x_new.reshape(b, n, -1), hw.reshape(b, n, -1), aff

def kernel(x, c, ctx, c_ctx, w_mod, b_mod, norm1_g, w_in, na_rpb, conv_w, gmlp_ln_g, w_spatial, b_spatial,
           w_branch, w_out, norm2_g, w_router, b_router, w_e_gate, w_e_up, w_e_down, final_g):
    b, n, d = x.shape
    lc = ctx.shape[1]
    cvec = jnp.concatenate([c, c_ctx[None], jnp.zeros((8 - b - 1, d), F32)], axis=0)
    modv = modvec(cvec, w_mod, b_mod).reshape(DEPTH, 8, 6, d)
    cos_t, sin_t = rope_tables(n)
    bias = na_bias(na_rpb, n // GRID_W)
    mix_w = (w_in, conv_w, gmlp_ln_g, w_spatial, b_spatial, w_branch, w_out)
    ctx_s = ctx
    for layer in range(DEPTH):
        last = layer == DEPTH - 1
        mod = modv[layer, :b]
        mod_c = jnp.broadcast_to(modv[layer, b][None], (b, 6, d))
        layer_w = mix_w + (norm2_g[layer], w_router[layer], b_router[layer])

        hc = norm_mod(ctx_s, norm1_g[layer], mod_c[:, 0], mod_c[:, 1]).reshape(b * lc, d)
        if last:
            kvc = qkv_proj(hc, w_in, layer, lc, cos_t, sin_t, rope=False, first_tile=1, n_tiles=2)
            kc_col = 0
        else:
            kvc = qkv_proj(hc, w_in, layer, lc, cos_t, sin_t, rope=False)
            kc_col = D_BRANCH
            attn_c = ctx_attention(kvc, b, lc)
            ctx_mid, hw_c, aff_c = mixer_block(ctx_s.reshape(b * lc, d), hc, attn_c, mod_c, lc, layer, *layer_w)

        h = norm_mod(x, norm1_g[layer], mod[:, 0], mod[:, 1]).reshape(b * n, d)
        qkv = qkv_proj(h, w_in, layer, n, cos_t, sin_t, rope=True)
        attn = na_attention(qkv, kvc, kc_col, bias, layer, b, n)
        x, hw, aff = mixer_block(x.reshape(b * n, d), h, attn, mod, n, layer, *layer_w)

        slot, cap, rows = moe_route(hw, aff)
        if last:
            (y,) = expert_ffn([rows], w_e_gate, w_e_up, w_e_down, layer)
        else:
            slot_c, cap_c, rows_c = moe_route(hw_c, aff_c)
            y, y_c = expert_ffn([rows, rows_c], w_e_gate, w_e_up, w_e_down, layer)
            ctx_s = combine(y_c, slot_c, ctx_mid, mod_c[:, 5], cap_c)
        x = combine(y, slot, x, mod[:, 5], cap)
    return final_norm(x, final_g)
```

```python
import functools

import numpy as np
import jax
import jax.numpy as jnp
from jax import lax
from jax.experimental import pallas as pl
from jax.experimental.pallas import tpu as pltpu
from jax.experimental.pallas import tpu_sc as plsc

D_MODEL = 2048
DEPTH = 2
GRID_W = 64
D_BRANCH = D_MODEL // 2
N_BRANCHES = 3
HEAD_DIM = 128
N_HEADS = D_BRANCH // HEAD_DIM
NA_KH = 8
NA_KW = 16
ROPE_THETA = 10000.0
ROPE_AXIS_DIM = HEAD_DIM // 2
CHUNK = 128
GMLP_GROUPS = D_BRANCH // 128
N_EXPERTS = 16
CAPACITY_FACTOR = 2
NORM_EPS = 1e-6
NEG_INF = -1e30
LOG2_E = float(np.log2(np.e))

QKV_COLS = 3 * D_BRANCH
COL_XC, COL_BG, COL_CG = (QKV_COLS + i * D_BRANCH for i in range(3))
P_COL0 = QKV_COLS + 3 * D_BRANCH
COL_U, COL_VS, COL_GATE = 0, D_BRANCH, 2 * D_BRANCH

NA_Q_ROWS = 4
NA_BAND_ROWS = 12
NA_TQ = NA_Q_ROWS * GRID_W
NA_TK = NA_BAND_ROWS * GRID_W
NA_HEADS_PER_STEP = N_HEADS
NA_HB = NA_HEADS_PER_STEP * HEAD_DIM
NA_SCORE_LEAD = 3

V7X_VMEM_LIMIT = 56 * 1024 * 1024
V7X_SC_CORES, V7X_SC_SUBCORES, V7X_SC_LANES = 2, 16, 16
V7X_SC_SUBCORE_VMEM_BYTES = 512 * 1024

BF16 = jnp.bfloat16
F32 = jnp.float32


def _params(n_axes, vmem=V7X_VMEM_LIMIT):
    return pltpu.CompilerParams(dimension_semantics=("arbitrary",) * n_axes, vmem_limit_bytes=vmem)


def _modvec_kernel(c_ref, w_ref, b_ref, o_ref):
    s = jax.nn.silu(c_ref[...]).astype(BF16)
    o_ref[0] = jnp.dot(s, w_ref[0].astype(BF16), preferred_element_type=F32) + b_ref[0]


def modvec(cvec, w_mod, b_mod, tn=1024):
    nl, d, n6 = w_mod.shape
    return pl.pallas_call(
        _modvec_kernel,
        grid=(nl, n6 // tn),
        in_specs=[
            pl.BlockSpec((8, d), lambda l, j: (0, 0)),
            pl.BlockSpec((1, d, tn), lambda l, j: (l, 0, j)),
            pl.BlockSpec((1, 1, tn), lambda l, j: (l, 0, j)),
        ],
        out_specs=pl.BlockSpec((1, 8, tn), lambda l, j: (l, 0, j)),
        out_shape=jax.ShapeDtypeStruct((nl, 8, n6), F32),
        compiler_params=_params(2),
        name="modvec",
    )(cvec, w_mod, b_mod.reshape(nl, 1, n6))


def _rms(x, g):
    ms = jnp.mean(x * x, axis=-1, keepdims=True)
    return x * lax.rsqrt(ms + NORM_EPS) * g


def _norm_mod_kernel(x_ref, g_ref, sh_ref, sc_ref, o_ref):
    y = _rms(x_ref[0], g_ref[...])
    o_ref[0] = (y * (1 + sc_ref[0]) + sh_ref[0]).astype(o_ref.dtype)


def _norm_kernel(x_ref, g_ref, o_ref):
    o_ref[0] = _rms(x_ref[0], g_ref[...]).astype(o_ref.dtype)


def norm_mod(x, g, shift, scale, out_dtype=BF16, tm=1024):
    b, n, d = x.shape
    tm = min(tm, n)
    return pl.pallas_call(
        _norm_mod_kernel,
        grid=(b, n // tm),
        in_specs=[
            pl.BlockSpec((1, tm, d), lambda i, j: (i, j, 0)),
            pl.BlockSpec((1, d), lambda i, j: (0, 0)),
            pl.BlockSpec((1, 1, d), lambda i, j: (i, 0, 0)),
            pl.BlockSpec((1, 1, d), lambda i, j: (i, 0, 0)),
        ],
        out_specs=pl.BlockSpec((1, tm, d), lambda i, j: (i, j, 0)),
        out_shape=jax.ShapeDtypeStruct((b, n, d), out_dtype),
        compiler_params=_params(2),
        name="norm_mod",
    )(x, g.reshape(1, d), shift.reshape(b, 1, d), scale.reshape(b, 1, d))


def final_norm(x, g, tm=1024):
    b, n, d = x.shape
    return pl.pallas_call(
        _norm_kernel,
        grid=(b, n // tm),
        in_specs=[
            pl.BlockSpec((1, tm, d), lambda i, j: (i, j, 0)),
            pl.BlockSpec((1, d), lambda i, j: (0, 0)),
        ],
        out_specs=pl.BlockSpec((1, tm, d), lambda i, j: (i, j, 0)),
        out_shape=jax.ShapeDtypeStruct((b, n, d), F32),
        compiler_params=_params(2),
        name="final_norm",
    )(x, g.reshape(1, d))


def _mm_kernel(x_ref, w_ref, o_ref, wbf_ref):
    @pl.when(pl.program_id(1) == 0)
    def _():
        wbf_ref[...] = w_ref[...].astype(BF16)

    o_ref[...] = jnp.dot(x_ref[...], wbf_ref[...], preferred_element_type=F32).astype(o_ref.dtype)


def matmul(x, w, layer, col_off, n_cols, out_dtype=F32, tm=1024, tn=1024):
    m, k = x.shape
    tm = min(tm, m)
    off = col_off // tn
    return pl.pallas_call(
        _mm_kernel,
        grid=(n_cols // tn, m // tm),
        in_specs=[
            pl.BlockSpec((tm, k), lambda j, i: (i, 0)),
            pl.BlockSpec((None, k, tn), lambda j, i: (layer, 0, off + j)),
        ],
        out_specs=pl.BlockSpec((tm, tn), lambda j, i: (i, j)),
        out_shape=jax.ShapeDtypeStruct((m, n_cols), out_dtype),
        scratch_shapes=[pltpu.VMEM((k, tn), BF16)],
        compiler_params=_params(2),
        name="matmul",
    )(x, w)


def _qkv_kernel(x_ref, w_ref, cos_ref, sin_ref, o_ref, wbf_ref, *, rope):
    @pl.when(pl.program_id(1) == 0)
    def _():
        wbf_ref[...] = w_ref[...].astype(BF16)

    def plain():
        o_ref[...] = jnp.dot(x_ref[...], wbf_ref[...], preferred_element_type=F32).astype(BF16)

    if not rope:
        plain()
        return
    c = pl.program_id(0)

    @pl.when(c < 2)
    def _():
        cs = cos_ref[...]
        sn = sin_ref[...]
        lane = lax.broadcasted_iota(jnp.int32, cs.shape, 1)
        first_half = (lane % ROPE_AXIS_DIM) < (ROPE_AXIS_DIM // 2)
        for pair in range(N_HEADS // 2):
            cols = slice(pair * 2 * HEAD_DIM, (pair + 1) * 2 * HEAD_DIM)
            acc = jnp.dot(x_ref[...], wbf_ref[:, cols], preferred_element_type=F32)
            for h in range(2):
                t = acc[:, h * HEAD_DIM:(h + 1) * HEAD_DIM]
                partner = jnp.where(first_half,
                                    pltpu.roll(t, HEAD_DIM - ROPE_AXIS_DIM // 2, 1),
                                    pltpu.roll(t, ROPE_AXIS_DIM // 2, 1))
                lo = (pair * 2 + h) * HEAD_DIM
                o_ref[:, lo:lo + HEAD_DIM] = (t * cs + partner * sn).astype(BF16)

    @pl.when(c == 2)
    def _():
        plain()


def rope_tables(n):
    pos = jnp.arange(n)
    rc = jnp.stack([pos // GRID_W, pos % GRID_W], axis=-1).astype(F32)
    inv = ROPE_THETA ** (-jnp.arange(0, ROPE_AXIS_DIM, 2, dtype=F32) / ROPE_AXIS_DIM)
    ang = rc[:, :, None] * inv
    cos, sin = jnp.cos(ang), jnp.sin(ang)
    cos_t = jnp.concatenate([cos, cos], axis=-1).reshape(n, HEAD_DIM)
    sin_t = jnp.concatenate([-sin, sin], axis=-1).reshape(n, HEAD_DIM)
    return cos_t, sin_t


def qkv_proj(x, w, layer, n, cos_t, sin_t, rope, first_tile=0, n_tiles=3, tm=1024):
    m, k = x.shape
    tm = min(tm, n if rope else m)
    nt = n // tm if rope else 1
    tab = pl.BlockSpec((tm, HEAD_DIM), lambda c, i: (i % nt, 0))
    return pl.pallas_call(
        functools.partial(_qkv_kernel, rope=rope),
        grid=(n_tiles, m // tm),
        in_specs=[
            pl.BlockSpec((tm, k), lambda c, i: (i, 0)),
            pl.BlockSpec((None, k, D_BRANCH), lambda c, i: (layer, 0, first_tile + c)),
            tab, tab,
        ],
        out_specs=pl.BlockSpec((tm, D_BRANCH), lambda c, i: (i, c)),
        out_shape=jax.ShapeDtypeStruct((m, n_tiles * D_BRANCH), BF16),
        scratch_shapes=[pltpu.VMEM((k, D_BRANCH), BF16)],
        compiler_params=_params(2),
        name="qkv_proj",
    )(x, w, cos_t, sin_t)


def _nt_dot(a, b):
    return lax.dot_general(a, b, (((1,), (1,)), ((), ())), preferred_element_type=F32)


def _na_pattern_rows(g, rows):
    band_row = int(np.clip(g * NA_Q_ROWS - NA_KH // 2, 0, rows - NA_BAND_ROWS))
    qr = g * NA_Q_ROWS + np.arange(NA_Q_ROWS)
    rs = np.clip(qr - NA_KH // 2, 0, rows - NA_KH)
    return band_row, qr, rs


def _na_bias_kernel(rpb_ref, o_ref, *, rows):
    shape = (GRID_W, 2 * GRID_W)
    qc = lax.broadcasted_iota(jnp.int32, shape, 0)
    lane = lax.broadcasted_iota(jnp.int32, shape, 1)
    low = lane < GRID_W
    kc = lane % GRID_W
    ws = jnp.clip(qc - NA_KW // 2, 0, GRID_W - NA_KW)
    col_ok = (kc >= ws) & (kc < ws + NA_KW)
    neg = jnp.full(shape, NEG_INF, F32)

    tiles = []
    for dr in range(2 * NA_KH - 1):
        t = jnp.broadcast_to(rpb_ref[0, 0, dr:dr + 1, :], shape)
        t = pltpu.roll(t, 2 * GRID_W - (NA_KW - 1), 1)
        for bit in range(6):
            t = jnp.where(((qc >> bit) & 1) == 1, pltpu.roll(t, 1 << bit, 1), t)
        t = jnp.where(low, t, pltpu.roll(t, GRID_W, 1))
        tiles.append(jnp.where(col_ok, t * LOG2_E, NEG_INF))

    n_groups = rows // NA_Q_ROWS
    for p, g in enumerate((0, 1, n_groups - 1)):
        band_row, qr, rs = _na_pattern_rows(g, rows)
        for qi in range(NA_Q_ROWS):
            def half(krl):
                kr = band_row + krl
                if rs[qi] <= kr < rs[qi] + NA_KH:
                    return tiles[kr - qr[qi] + NA_KH - 1]
                return neg
            for pair in range(NA_BAND_ROWS // 2):
                blk = jnp.where(low, half(2 * pair), half(2 * pair + 1))
                o_ref[0, p, 0, qi * GRID_W:(qi + 1) * GRID_W, pair * 128:(pair + 1) * 128] = blk


def na_bias(na_rpb, rows):
    nl, nh, ndr, ndc = na_rpb.shape
    band1, qr1, rs1 = _na_pattern_rows(1, rows)
    for g in range(2, rows // NA_Q_ROWS - 1):
        band, qr, rs = _na_pattern_rows(g, rows)
        assert (qr - band == qr1 - band1).all() and (rs - band == rs1 - band1).all()
    rpb = jnp.pad(na_rpb, ((0, 0), (0, 0), (0, 16 - ndr), (0, 128 - ndc)))
    return pl.pallas_call(
        functools.partial(_na_bias_kernel, rows=rows),
        grid=(nl, nh),
        in_specs=[pl.BlockSpec((1, 1, 16, 128), lambda l, h: (l, h, 0, 0))],
        out_specs=pl.BlockSpec((1, 3, 1, NA_TQ, NA_TK), lambda l, h: (l, 0, h, 0, 0)),
        out_shape=jax.ShapeDtypeStruct((nl, 3, nh, NA_TQ, NA_TK), F32),
        compiler_params=_params(2),
        name="na_bias",
    )(rpb)


def _na_kernel(q_ref, k_ref, v_ref, kc_ref, vc_ref, bias_ref, o_ref):
    g = pl.program_id(2)
    n_groups = k_ref.shape[1] // NA_TQ
    band_row = jnp.clip(g * NA_Q_ROWS - NA_KH // 2, 0, n_groups * NA_Q_ROWS - NA_BAND_ROWS)
    start = pl.multiple_of(band_row * GRID_W, NA_TQ)
    pat = jnp.where(g == 0, 0, jnp.where(g == n_groups - 1, 2, 1))
    scale = HEAD_DIM ** -0.5 * LOG2_E

    def scores(hh):
        cols = slice(hh * HEAD_DIM, (hh + 1) * HEAD_DIM)
        q = q_ref[0, :, cols]
        return _nt_dot(q, k_ref[0, pl.ds(start, NA_TK), cols]), _nt_dot(q, kc_ref[0, :, cols])

    def finish(hh, qk_loc, qk_ctx):
        cols = slice(hh * HEAD_DIM, (hh + 1) * HEAD_DIM)
        s_loc = qk_loc * scale + bias_ref[0, pat, hh]
        s_ctx = qk_ctx * scale
        m = jnp.maximum(jnp.max(s_loc, axis=1, keepdims=True), jnp.max(s_ctx, axis=1, keepdims=True))
        e_loc = jnp.exp2(s_loc - m)
        e_ctx = jnp.exp2(s_ctx - m)
        inv = 1.0 / (jnp.sum(e_loc, axis=1, keepdims=True) + jnp.sum(e_ctx, axis=1, keepdims=True))
        o = jnp.dot(e_loc.astype(BF16), v_ref[0, pl.ds(start, NA_TK), cols], preferred_element_type=F32)
        o = o + jnp.dot(e_ctx.astype(BF16), vc_ref[0, :, cols], preferred_element_type=F32)
        o_ref[0, :, cols] = (o * inv).astype(o_ref.dtype)

    qk = {}
    for hh in range(NA_HEADS_PER_STEP + NA_SCORE_LEAD):
        if hh < NA_HEADS_PER_STEP:
            qk[hh] = scores(hh)
        if hh >= NA_SCORE_LEAD:
            finish(hh - NA_SCORE_LEAD, *qk.pop(hh - NA_SCORE_LEAD))


def na_attention(qkv, kvc, kc_col, bias, layer, b, n):
    lc = kvc.shape[0] // b
    qkv3 = qkv.reshape(b, n, QKV_COLS)
    kvc3 = kvc.reshape(b, lc, kvc.shape[1])
    hbs = D_BRANCH // NA_HB
    kcb = kc_col // NA_HB
    out = pl.pallas_call(
        _na_kernel,
        grid=(hbs, b, n // NA_TQ),
        in_specs=[
            pl.BlockSpec((1, NA_TQ, NA_HB), lambda h, i, g: (i, g, h)),
            pl.BlockSpec((1, n, NA_HB), lambda h, i, g: (i, 0, hbs + h)),
            pl.BlockSpec((1, n, NA_HB), lambda h, i, g: (i, 0, 2 * hbs + h)),
            pl.BlockSpec((1, lc, NA_HB), lambda h, i, g: (i, 0, kcb + h)),
            pl.BlockSpec((1, lc, NA_HB), lambda h, i, g: (i, 0, kcb + hbs + h)),
            pl.BlockSpec((1, 3, NA_HEADS_PER_STEP, NA_TQ, NA_TK), lambda h, i, g: (layer, 0, h, 0, 0),
                         pipeline_mode=pl.Buffered(1)),
        ],
        out_specs=pl.BlockSpec((1, NA_TQ, NA_HB), lambda h, i, g: (i, g, h)),
        out_shape=jax.ShapeDtypeStruct((b, n, D_BRANCH), BF16),
        compiler_params=_params(3),
        name="na_attention",
    )(qkv3, qkv3, qkv3, kvc3, kvc3, bias)
    return out.reshape(b * n, D_BRANCH)


def _ctx_attn_kernel(q_ref, k_ref, v_ref, o_ref):
    for h in range(N_HEADS):
        cols = slice(h * HEAD_DIM, (h + 1) * HEAD_DIM)
        s = _nt_dot(q_ref[0, :, cols], k_ref[0, :, cols]) * (HEAD_DIM ** -0.5)
        m = jnp.max(s, axis=1, keepdims=True)
        e = jnp.exp(s - m)
        p = e * (1.0 / jnp.sum(e, axis=1, keepdims=True))
        o_ref[0, :, cols] = jnp.dot(p.astype(BF16), v_ref[0, :, cols],
                                    preferred_element_type=F32).astype(o_ref.dtype)


def ctx_attention(qkv, b, lc):
    qkv3 = qkv.reshape(b, lc, QKV_COLS)
    part = lambda k: pl.BlockSpec((1, lc, D_BRANCH), lambda i: (i, 0, k))
    out = pl.pallas_call(
        _ctx_attn_kernel,
        grid=(b,),
        in_specs=[part(0), part(1), part(2)],
        out_specs=part(0),
        out_shape=jax.ShapeDtypeStruct((b, lc, D_BRANCH), BF16),
        compiler_params=_params(1),
        name="ctx_attention",
    )(qkv3, qkv3, qkv3)
    return out.reshape(b * lc, D_BRANCH)


def _conv_kernel(h_ref, wx_ref, wb_ref, wc_ref, w_ref, o_ref, wbf_ref, *, seq):
    @pl.when(pl.program_id(1) == 0)
    def _():
        for k, ref in enumerate((wx_ref, wb_ref, wc_ref)):
            wbf_ref[k] = ref[...].astype(BF16)

    h = h_ref[0]
    xc, bg, cg = (jnp.dot(h, wbf_ref[k], preferred_element_type=F32) for k in range(3))
    z = cg * xc
    rows = z.shape[0]
    pos = lax.broadcasted_iota(jnp.int32, z.shape, 0) % seq
    z_prev = jnp.where(pos == 0, 0.0, pltpu.roll(z, 1, 0))
    z_next = jnp.where(pos == seq - 1, 0.0, pltpu.roll(z, rows - 1, 0))
    y = z_prev * w_ref[0, 0:1, :] + z * w_ref[0, 1:2, :] + z_next * w_ref[0, 2:3, :]
    o_ref[0] = (bg * y).astype(o_ref.dtype)


def short_conv(h, w_in, conv_w, layer, n, tc=256, rows=2048):
    m, d = h.shape
    rows = min(max(rows, n), m)
    assert rows % n == 0 and m % rows == 0
    wcol = lambda col: pl.BlockSpec((None, d, tc), lambda j, i: (layer, 0, col // tc + j))
    out = pl.pallas_call(
        functools.partial(_conv_kernel, seq=n),
        grid=(D_BRANCH // tc, m // rows),
        in_specs=[pl.BlockSpec((1, rows, d), lambda j, i: (i, 0, 0)),
                  wcol(COL_XC), wcol(COL_BG), wcol(COL_CG),
                  pl.BlockSpec((1, 3, tc), lambda j, i: (layer, 0, j))],
        out_specs=pl.BlockSpec((1, rows, tc), lambda j, i: (i, 0, j)),
        out_shape=jax.ShapeDtypeStruct((m // rows, rows, D_BRANCH), BF16),
        scratch_shapes=[pltpu.VMEM((3, d, tc), BF16)],
        compiler_params=_params(2),
        name="short_conv",
    )(h.reshape(m // rows, rows, d), w_in, w_in, w_in, conv_w)
    return out.reshape(m, D_BRANCH)


def _gmlp_kernel(u_ref, v_ref, g_ref, ws_ref, bs_ref, o_ref):
    v = jax.nn.gelu(v_ref[...])
    mu = jnp.mean(v, axis=-1, keepdims=True)
    var = jnp.mean(jnp.square(v - mu), axis=-1, keepdims=True)
    vn = ((v - mu) * lax.rsqrt(var + NORM_EPS) * g_ref[...]).astype(BF16)
    for grp in range(GMLP_GROUPS):
        cols = slice(grp * 128, (grp + 1) * 128)
        ws = ws_ref[0, grp].astype(BF16)
        for ch in range(v.shape[0] // CHUNK):
            rows = slice(ch * CHUNK, (ch + 1) * CHUNK)
            s = jnp.dot(ws, vn[rows, cols], preferred_element_type=F32) + bs_ref[:, grp:grp + 1]
            o_ref[rows, cols] = (jax.nn.gelu(u_ref[rows, cols]) * s).astype(o_ref.dtype)


def chunk_gmlp(p, ln_g, w_s, b_s, layer, tm=8 * CHUNK):
    m = p.shape[0]
    return pl.pallas_call(
        _gmlp_kernel,
        grid=(m // tm,),
        in_specs=[
            pl.BlockSpec((tm, D_BRANCH), lambda i: (i, COL_U // D_BRANCH)),
            pl.BlockSpec((tm, D_BRANCH), lambda i: (i, COL_VS // D_BRANCH)),
            pl.BlockSpec((1, D_BRANCH), lambda i: (0, 0)),
            pl.BlockSpec((1, GMLP_GROUPS, CHUNK, CHUNK), lambda i: (layer, 0, 0, 0)),
            pl.BlockSpec((CHUNK, GMLP_GROUPS), lambda i: (0, 0)),
        ],
        out_specs=pl.BlockSpec((tm, D_BRANCH), lambda i: (i, 0)),
        out_shape=jax.ShapeDtypeStruct((m, D_BRANCH), BF16),
        compiler_params=_params(1),
        name="chunk_gmlp",
    )(p, p, ln_g[layer].reshape(1, D_BRANCH), w_s, b_s[layer].T)


def _merge_kernel(a_ref, c_ref, m_ref, g0_ref, g1_ref, g2_ref, w_ref, o_ref, wbf_ref):
    @pl.when(pl.program_id(1) == 0)
    def _():
        wbf_ref[...] = w_ref[...].astype(BF16)

    acc = None
    for i, (br, gl) in enumerate(((a_ref, g0_ref), (c_ref, g1_ref), (m_ref, g2_ref))):
        proj = jnp.dot(br[...], wbf_ref[i], preferred_element_type=F32)
        term = jax.nn.sigmoid(gl[...]) * proj
        acc = term if acc is None else acc + term
    o_ref[...] = acc.astype(o_ref.dtype)


def merge_branches(attn, conv, gm, p, w_branch, layer, tm=512, tn=1024):
    m = attn.shape[0]
    br = pl.BlockSpec((tm, D_BRANCH), lambda j, i: (i, 0))
    gate = lambda k: pl.BlockSpec((tm, tn), lambda j, i: (i, (COL_GATE + k * D_MODEL) // tn + j))
    return pl.pallas_call(
        _merge_kernel,
        grid=(D_MODEL // tn, m // tm),
        in_specs=[br, br, br, gate(0), gate(1), gate(2),
                  pl.BlockSpec((None, N_BRANCHES, D_BRANCH, tn), lambda j, i: (layer, 0, 0, j))],
        out_specs=pl.BlockSpec((tm, tn), lambda j, i: (i, j)),
        out_shape=jax.ShapeDtypeStruct((m, D_MODEL), BF16),
        scratch_shapes=[pltpu.VMEM((N_BRANCHES, D_BRANCH, tn), BF16)],
        compiler_params=_params(2),
        name="merge_branches",
    )(attn, conv, gm, p, p, p, w_branch)


def _out_proj_router_kernel(h_ref, w_ref, x_ref, mod_ref, g_ref, sh_ref, sc_ref, wr_ref, br_ref,
                            o_ref, hw_ref, aff_ref, wbf_ref):
    @pl.when(pl.program_id(0) == 0)
    def _():
        wbf_ref[...] = w_ref[...].astype(BF16)

    x = x_ref[...] + mod_ref[0] * jnp.dot(h_ref[...], wbf_ref[...], preferred_element_type=F32)
    o_ref[...] = x
    y = _rms(x, g_ref[...])
    h = (y * (1 + sc_ref[0]) + sh_ref[0]).astype(BF16)
    bits = pltpu.bitcast(h.astype(F32), jnp.uint32)
    half = bits.shape[1] // 2
    hw_ref[...] = lax.shift_right_logical(bits[:, :half], jnp.uint32(16)) | bits[:, half:]
    logits = _nt_dot(wr_ref[...].astype(BF16), h) + br_ref[...]
    mx = jnp.max(logits, axis=0, keepdims=True)
    e = jnp.exp(logits - mx)
    aff_ref[0] = e / jnp.sum(e, axis=0, keepdims=True)


def out_proj_router(h, w_out, layer, x, mod, g, w_router, b_router, n, tm=512):
    m, d = x.shape
    b = m // n
    tm = min(tm, n)
    per_b = n // tm
    vec = lambda k: (pl.BlockSpec((1, 1, d), lambda i: (i // per_b, 0, 0)), mod[:, k].reshape(b, 1, d))
    (gate_s, gate_v), (sh_s, sh_v), (sc_s, sc_v) = vec(2), vec(3), vec(4)
    return pl.pallas_call(
        _out_proj_router_kernel,
        grid=(m // tm,),
        in_specs=[
            pl.BlockSpec((tm, d), lambda i: (i, 0)),
            pl.BlockSpec((None, d, d), lambda i: (layer, 0, 0), pipeline_mode=pl.Buffered(1)),
            pl.BlockSpec((tm, d), lambda i: (i, 0)),
            gate_s,
            pl.BlockSpec((1, d), lambda i: (0, 0)),
            sh_s, sc_s,
            pl.BlockSpec((N_EXPERTS, d), lambda i: (0, 0)),
            pl.BlockSpec((N_EXPERTS, 1), lambda i: (0, 0)),
        ],
        out_specs=[
            pl.BlockSpec((tm, d), lambda i: (i, 0)),
            pl.BlockSpec((tm, d // 2), lambda i: (i, 0)),
            pl.BlockSpec((1, N_EXPERTS, tm), lambda i: (i // per_b, 0, i % per_b)),
        ],
        out_shape=[jax.ShapeDtypeStruct((m, d), F32),
                   jax.ShapeDtypeStruct((m, d // 2), jnp.uint32),
                   jax.ShapeDtypeStruct((b, N_EXPERTS, n), F32)],
        scratch_shapes=[pltpu.VMEM((d, d), BF16)],
        compiler_params=_params(1),
        name="out_proj_router",
    )(h, w_out, x, gate_v, g.reshape(1, d), sh_v, sc_v, w_router.T, b_router.reshape(N_EXPERTS, 1))


def _select_kernel(aff_ref, tri_ref, slot_ref, *, cap):
    bits = pltpu.bitcast(aff_ref[...], jnp.int32)

    def step(i, t):
        cand = t | lax.shift_left(jnp.int32(1), 30 - i)
        cnt = jnp.sum(jnp.where(bits >= cand, 1.0, 0.0), axis=1, keepdims=True)
        return jnp.where(cnt >= cap, cand, t)

    t = lax.fori_loop(0, 31, step, jnp.zeros((bits.shape[0], 1), jnp.int32))
    gt = bits > t
    eq = bits == t
    need = cap - jnp.sum(jnp.where(gt, 1.0, 0.0), axis=1, keepdims=True)
    tri = tri_ref[...]
    eq_rank = jnp.dot(jnp.where(eq, 1.0, 0.0).astype(BF16), tri, preferred_element_type=F32)
    sel = gt | (eq & (eq_rank < need))
    slot = jnp.dot(jnp.where(sel, 1.0, 0.0).astype(BF16), tri, preferred_element_type=F32)
    slot_ref[...] = jnp.where(sel, slot.astype(jnp.int32), -1)


def select_tokens(aff, cap):
    b, e, n = aff.shape
    tri = jnp.triu(jnp.ones((n, n), BF16), k=1)
    rows = pl.BlockSpec((b * e, n), lambda i: (0, 0))
    return pl.pallas_call(
        functools.partial(_select_kernel, cap=cap),
        grid=(1,),
        in_specs=[rows, pl.BlockSpec((n, n), lambda i: (0, 0))],
        out_specs=rows,
        out_shape=jax.ShapeDtypeStruct((b * e, n), jnp.int32),
        compiler_params=_params(1),
        name="select_tokens",
    )(aff.reshape(b * e, n), tri).reshape(b, e, n)


def _onehot(slot, cap):
    hit = lax.broadcasted_iota(jnp.int32, (cap, slot.shape[1]), 0) == slot
    return hit, jnp.where(hit, 1.0, 0.0)


def gather_tokens(h, slot, aff, cap):
    b, n, w = h.shape
    e = N_EXPERTS
    workers = V7X_SC_CORES * V7X_SC_SUBCORES
    rows_per_worker = (b * e) // workers
    word_bytes = jnp.dtype(h.dtype).itemsize
    chunk = min(cap, V7X_SC_SUBCORE_VMEM_BYTES // 2 // (w * word_bytes))
    assert (b * e) % workers == 0 and cap % chunk == 0 and n % V7X_SC_LANES == 0 and word_bytes == 4

    def body(h_hbm, slot_hbm, aff_hbm, xs_hbm, gates_hbm, slot_v, aff_v, idx_v, gate_v, rows_v, sem):
        worker = lax.axis_index("s") * V7X_SC_CORES + lax.axis_index("c")
        lane = lax.iota(jnp.int32, V7X_SC_LANES)

        @pl.loop(0, rows_per_worker)
        def _(j):
            r = worker * rows_per_worker + j
            bi = r // e
            ei = r - bi * e
            pltpu.sync_copy(slot_hbm.at[r], slot_v)
            pltpu.sync_copy(aff_hbm.at[r], aff_v)

            @pl.loop(0, n // V7X_SC_LANES)
            def _(c):
                tokens = pl.ds(c * V7X_SC_LANES, V7X_SC_LANES)
                s = slot_v[tokens]
                chosen = s >= 0
                plsc.store_scatter(idx_v, [s], lane + (c * V7X_SC_LANES + bi * n), mask=chosen)
                plsc.store_scatter(gate_v, [s], aff_v[tokens], mask=chosen)

            out_row = (ei * b + bi) * cap

            @pl.loop(0, cap // chunk)
            def _(q):
                pltpu.async_copy(h_hbm.at[idx_v.at[pl.ds(q * chunk, chunk)]], rows_v, sem).wait()
                pltpu.sync_copy(rows_v, xs_hbm.at[pl.ds(out_row + q * chunk, chunk)])

            pltpu.sync_copy(gate_v, gates_hbm.at[pl.ds(out_row, cap)])

    xs, gates = pl.kernel(
        body,
        mesh=plsc.VectorSubcoreMesh(core_axis_name="c", subcore_axis_name="s"),
        out_type=[jax.ShapeDtypeStruct((e * b * cap, w), h.dtype),
                  jax.ShapeDtypeStruct((e * b * cap,), F32)],
        scratch_types=[
            pltpu.VMEM((n,), jnp.int32), pltpu.VMEM((n,), F32),
            pltpu.VMEM((cap,), jnp.int32), pltpu.VMEM((cap,), F32),
            pltpu.VMEM((chunk, w), h.dtype),
            pltpu.SemaphoreType.DMA,
        ],
        compiler_params=pltpu.CompilerParams(needs_layout_passes=False),
        name="gather_tokens",
    )(h.reshape(b * n, w), slot.reshape(b * e, n), aff.reshape(b * e, n))
    return xs.reshape(e, b * cap, w), gates.reshape(e, b * cap, 1)


def _ffn_kernel(*refs, n_sets, n_f):
    xs_refs, gate_refs = refs[0:2 * n_sets:2], refs[1:2 * n_sets:2]
    wg_ref, wu_ref, wd_ref = refs[2 * n_sets:2 * n_sets + 3]
    y_refs = refs[2 * n_sets + 3:3 * n_sets + 3]
    act_refs = refs[3 * n_sets + 3:4 * n_sets + 3]
    row_refs = refs[4 * n_sets + 3:]
    s = pl.program_id(1)
    tf = wg_ref.shape[1]

    @pl.when(s == 0)
    def _():
        for xs_ref, row_ref in zip(xs_refs, row_refs):
            words = xs_ref[0]
            half = words.shape[1]
            row_ref[:, :half] = pltpu.bitcast(lax.shift_left(words, jnp.uint32(16)), F32).astype(BF16)
            row_ref[:, half:] = pltpu.bitcast(words & jnp.uint32(0xFFFF0000), F32).astype(BF16)

    @pl.when(s < n_f)
    def _():
        wg = wg_ref[...].astype(BF16)
        wu = wu_ref[...].astype(BF16)
        for row_ref, act_ref in zip(row_refs, act_refs):
            xs = row_ref[...]
            hg = jnp.dot(xs, wg, preferred_element_type=F32)
            hu = jnp.dot(xs, wu, preferred_element_type=F32)
            act_ref[s] = (jax.nn.silu(hg) * hu).astype(BF16)

    @pl.when(s >= n_f)
    def _():
        wd = wd_ref[...].astype(BF16)
        for gate_ref, y_ref, act_ref in zip(gate_refs, y_refs, act_refs):
            y = None
            for f in range(n_f):
                part = jnp.dot(act_ref[f], wd[f * tf:(f + 1) * tf, :], preferred_element_type=F32)
                y = part if y is None else y + part
            y_ref[0] = (y * gate_ref[0]).astype(y_ref.dtype)


def expert_ffn(sets, w_gate, w_up, w_down, layer, tf=512, tn=512):
    d, ff = w_gate.shape[2], w_gate.shape[3]
    n_f, n_d = ff // tf, d // tn
    d_idx = lambda s: jnp.maximum(s - n_f, 0)

    def up_tile(i, s):
        ahead = s >= n_f
        return (layer, jnp.where(ahead, jnp.minimum(i + 1, N_EXPERTS - 1), i), 0,
                jnp.where(ahead, 0, jnp.minimum(s, n_f - 1)))

    def down_tile(i, s):
        behind = s < n_f
        return (layer, jnp.where(behind, jnp.maximum(i - 1, 0), i), 0, jnp.where(behind, n_d - 1, s - n_f))
    in_specs, operands, out_specs, out_shape, act_scratch, row_scratch = [], [], [], [], [], []
    for xs, gates in sets:
        r = xs.shape[1]
        in_specs += [pl.BlockSpec((1, r, d // 2), lambda i, s: (i, 0, 0)),
                     pl.BlockSpec((1, r, 1), lambda i, s: (i, 0, 0))]
        operands += [xs, gates]
        out_specs.append(pl.BlockSpec((1, r, tn), lambda i, s: (i, 0, d_idx(s))))
        out_shape.append(jax.ShapeDtypeStruct((N_EXPERTS, r, d), BF16))
        act_scratch.append(pltpu.VMEM((n_f, r, tf), BF16))
        row_scratch.append(pltpu.VMEM((r, d), BF16))
    in_specs += [
        pl.BlockSpec((None, None, d, tf), up_tile),
        pl.BlockSpec((None, None, d, tf), up_tile),
        pl.BlockSpec((None, None, ff, tn), down_tile),
    ]
    return pl.pallas_call(
        functools.partial(_ffn_kernel, n_sets=len(sets), n_f=n_f),
        grid=(N_EXPERTS, n_f + n_d),
        in_specs=in_specs,
        out_specs=out_specs,
        out_shape=out_shape,
        scratch_shapes=act_scratch + row_scratch,
        compiler_params=_params(2),
        name="expert_ffn",
    )(*operands, w_gate, w_up, w_down)


def _combine_kernel(y_ref, slot_ref, x_ref, mod_ref, o_ref, onehot_ref, *, cap):
    @pl.when(pl.program_id(1) == 0)
    def _():
        for e in range(N_EXPERTS):
            onehot_ref[e * cap:(e + 1) * cap, :] = _onehot(slot_ref[e], cap)[1].astype(BF16)

    y = y_ref[...].reshape(N_EXPERTS * cap, y_ref.shape[2])
    moe = lax.dot_general(onehot_ref[...], y, (((0,), (0,)), ((), ())), preferred_element_type=F32)
    o_ref[0] = x_ref[0] + mod_ref[0] * moe


def combine(y, slot, x, gate_vec, cap, tn=512):
    b, n, d = x.shape
    e = N_EXPERTS
    return pl.pallas_call(
        functools.partial(_combine_kernel, cap=cap),
        grid=(b, d // tn),
        in_specs=[
            pl.BlockSpec((e, cap, tn), lambda i, j: (0, i, j)),
            pl.BlockSpec((e, 1, n), lambda i, j: (i, 0, 0)),
            pl.BlockSpec((1, n, tn), lambda i, j: (i, 0, j)),
            pl.BlockSpec((1, 1, tn), lambda i, j: (i, 0, j)),
        ],
        out_specs=pl.BlockSpec((1, n, tn), lambda i, j: (i, 0, j)),
        out_shape=jax.ShapeDtypeStruct((b, n, d), F32),
        scratch_shapes=[pltpu.VMEM((e * cap, n), BF16)],
        compiler_params=_params(2),
        name="combine",
    )(y, slot.reshape(b * e, 1, n), x, gate_vec.reshape(b, 1, d))


def moe_route(hw, aff):
    n = hw.shape[1]
    cap = CAPACITY_FACTOR * n // N_EXPERTS
    slot = select_tokens(aff, cap)
    return slot, cap, gather_tokens(hw, slot, aff, cap)


def mixer_block(x, h, attn, mod, n, layer, w_in, conv_w, ln_g, w_s, b_s, w_branch, w_out,
                g2, w_router, b_router):
    b = x.shape[0] // n
    p = matmul(h, w_in, layer, P_COL0, w_in.shape[2] - P_COL0)
    conv = short_conv(h, w_in, conv_w, layer, n)
    gm = chunk_gmlp(p, ln_g, w_s, b_s, layer)
    merged = merge_branches(attn, conv, gm, p, w_branch, layer)
    x_new, hw, aff = out_proj_router(merged, w_out, layer, x, mod, g2, w_router, b_router, n)
    return x_new.reshape(b, n, -1), hw.reshape(b, n, -1), aff


def kernel(x, c, ctx, c_ctx, w_mod, b_mod, norm1_g, w_in, na_rpb, conv_w, gmlp_ln_g, w_spatial, b_spatial,
           w_branch, w_out, norm2_g, w_router, b_router, w_e_gate, w_e_up, w_e_down, final_g):
    b, n, d = x.shape
    lc = ctx.shape[1]
    cvec = jnp.concatenate([c, c_ctx[None], jnp.zeros((8 - b - 1, d), F32)], axis=0)
    modv = modvec(cvec, w_mod, b_mod).reshape(DEPTH, 8, 6, d)
    cos_t, sin_t = rope_tables(n)
    bias = na_bias(na_rpb, n // GRID_W)
    mix_w = (w_in, conv_w, gmlp_ln_g, w_spatial, b_spatial, w_branch, w_out)
    ctx_s = ctx
    for layer in range(DEPTH):
        last = layer == DEPTH - 1
        mod = modv[layer, :b]
        mod_c = jnp.broadcast_to(modv[layer, b][None], (b, 6, d))
        layer_w = mix_w + (norm2_g[layer], w_router[layer], b_router[layer])

        hc = norm_mod(ctx_s, norm1_g[layer], mod_c[:, 0], mod_c[:, 1]).reshape(b * lc, d)
        if last:
            kvc = qkv_proj(hc, w_in, layer, lc, cos_t, sin_t, rope=False, first_tile=1, n_tiles=2)
            kc_col = 0
        else:
            kvc = qkv_proj(hc, w_in, layer, lc, cos_t, sin_t, rope=False)
            kc_col = D_BRANCH
            attn_c = ctx_attention(kvc, b, lc)
            ctx_mid, hw_c, aff_c = mixer_block(ctx_s.reshape(b * lc, d), hc, attn_c, mod_c, lc, layer, *layer_w)

        h = norm_mod(x, norm1_g[layer], mod[:, 0], mod[:, 1]).reshape(b * n, d)
        qkv = qkv_proj(h, w_in, layer, n, cos_t, sin_t, rope=True)
        attn = na_attention(qkv, kvc, kc_col, bias, layer, b, n)
        x, hw, aff = mixer_block(x.reshape(b * n, d), h, attn, mod, n, layer, *layer_w)

        slot, cap, rows = moe_route(hw, aff)
        if last:
            (y,) = expert_ffn([rows], w_e_gate, w_e_up, w_e_down, layer)
        else:
            slot_c, cap_c, rows_c = moe_route(hw_c, aff_c)
            y, y_c = expert_ffn([rows, rows_c], w_e_gate, w_e_up, w_e_down, layer)
            ctx_s = combine(y_c, slot_c, ctx_mid, mod_c[:, 5], cap_c)
        x = combine(y, slot, x, mod[:, 5], cap)
    return final_norm(x, final_g)
```

```python
import functools

import numpy as np
import jax
import jax.numpy as jnp
from jax import lax
from jax.experimental import pallas as pl
from jax.experimental.pallas import tpu as pltpu
from jax.experimental.pallas import tpu_sc as plsc

D_MODEL = 2048
DEPTH = 2
GRID_W = 64
D_BRANCH = D_MODEL // 2
N_BRANCHES = 3
HEAD_DIM = 128
N_HEADS = D_BRANCH // HEAD_DIM
NA_KH = 8
NA_KW = 16
ROPE_THETA = 10000.0
ROPE_AXIS_DIM = HEAD_DIM // 2
CHUNK = 128
GMLP_GROUPS = D_BRANCH // 128
N_EXPERTS = 16
CAPACITY_FACTOR = 2
NORM_EPS = 1e-6
NEG_INF = -1e30
LOG2_E = float(np.log2(np.e))

QKV_COLS = 3 * D_BRANCH
COL_XC, COL_BG, COL_CG = (QKV_COLS + i * D_BRANCH for i in range(3))
P_COL0 = QKV_COLS + 3 * D_BRANCH
COL_U, COL_VS, COL_GATE = 0, D_BRANCH, 2 * D_BRANCH

NA_Q_ROWS = 4
NA_BAND_ROWS = 12
NA_TQ = NA_Q_ROWS * GRID_W
NA_TK = NA_BAND_ROWS * GRID_W
NA_HEADS_PER_STEP = N_HEADS
NA_HB = NA_HEADS_PER_STEP * HEAD_DIM
NA_SCORE_LEAD = 3

V7X_VMEM_LIMIT = 56 * 1024 * 1024
V7X_SC_CORES, V7X_SC_SUBCORES, V7X_SC_LANES = 2, 16, 16
V7X_SC_SUBCORE_VMEM_BYTES = 512 * 1024

BF16 = jnp.bfloat16
F32 = jnp.float32


def _params(n_axes, vmem=V7X_VMEM_LIMIT):
    return pltpu.CompilerParams(dimension_semantics=("arbitrary",) * n_axes, vmem_limit_bytes=vmem)


def _modvec_kernel(c_ref, w_ref, b_ref, o_ref):
    s = jax.nn.silu(c_ref[...]).astype(BF16)
    o_ref[0] = jnp.dot(s, w_ref[0].astype(BF16), preferred_element_type=F32) + b_ref[0]


def modvec(cvec, w_mod, b_mod, tn=1024):
    nl, d, n6 = w_mod.shape
    return pl.pallas_call(
        _modvec_kernel,
        grid=(nl, n6 // tn),
        in_specs=[
            pl.BlockSpec((8, d), lambda l, j: (0, 0)),
            pl.BlockSpec((1, d, tn), lambda l, j: (l, 0, j)),
            pl.BlockSpec((1, 1, tn), lambda l, j: (l, 0, j)),
        ],
        out_specs=pl.BlockSpec((1, 8, tn), lambda l, j: (l, 0, j)),
        out_shape=jax.ShapeDtypeStruct((nl, 8, n6), F32),
        compiler_params=_params(2),
        name="modvec",
    )(cvec, w_mod, b_mod.reshape(nl, 1, n6))


def _rms(x, g):
    ms = jnp.mean(x * x, axis=-1, keepdims=True)
    return x * lax.rsqrt(ms + NORM_EPS) * g


def _norm_mod_kernel(x_ref, g_ref, sh_ref, sc_ref, o_ref):
    y = _rms(x_ref[0], g_ref[...])
    o_ref[0] = (y * (1 + sc_ref[0]) + sh_ref[0]).astype(o_ref.dtype)


def _norm_kernel(x_ref, g_ref, o_ref):
    o_ref[0] = _rms(x_ref[0], g_ref[...]).astype(o_ref.dtype)


def norm_mod(x, g, shift, scale, out_dtype=BF16, tm=1024):
    b, n, d = x.shape
    tm = min(tm, n)
    return pl.pallas_call(
        _norm_mod_kernel,
        grid=(b, n // tm),
        in_specs=[
            pl.BlockSpec((1, tm, d), lambda i, j: (i, j, 0)),
            pl.BlockSpec((1, d), lambda i, j: (0, 0)),
            pl.BlockSpec((1, 1, d), lambda i, j: (i, 0, 0)),
            pl.BlockSpec((1, 1, d), lambda i, j: (i, 0, 0)),
        ],
        out_specs=pl.BlockSpec((1, tm, d), lambda i, j: (i, j, 0)),
        out_shape=jax.ShapeDtypeStruct((b, n, d), out_dtype),
        compiler_params=_params(2),
        name="norm_mod",
    )(x, g.reshape(1, d), shift.reshape(b, 1, d), scale.reshape(b, 1, d))


def final_norm(x, g, tm=1024):
    b, n, d = x.shape
    return pl.pallas_call(
        _norm_kernel,
        grid=(b, n // tm),
        in_specs=[
            pl.BlockSpec((1, tm, d), lambda i, j: (i, j, 0)),
            pl.BlockSpec((1, d), lambda i, j: (0, 0)),
        ],
        out_specs=pl.BlockSpec((1, tm, d), lambda i, j: (i, j, 0)),
        out_shape=jax.ShapeDtypeStruct((b, n, d), F32),
        compiler_params=_params(2),
        name="final_norm",
    )(x, g.reshape(1, d))


def _mm_kernel(x_ref, w_ref, o_ref, wbf_ref):
    @pl.when(pl.program_id(1) == 0)
    def _():
        wbf_ref[...] = w_ref[...].astype(BF16)

    o_ref[...] = jnp.dot(x_ref[...], wbf_ref[...], preferred_element_type=F32).astype(o_ref.dtype)


def matmul(x, w, layer, col_off, n_cols, out_dtype=F32, tm=1024, tn=1024):
    m, k = x.shape
    tm = min(tm, m)
    off = col_off // tn
    return pl.pallas_call(
        _mm_kernel,
        grid=(n_cols // tn, m // tm),
        in_specs=[
            pl.BlockSpec((tm, k), lambda j, i: (i, 0)),
            pl.BlockSpec((None, k, tn), lambda j, i: (layer, 0, off + j)),
        ],
        out_specs=pl.BlockSpec((tm, tn), lambda j, i: (i, j)),
        out_shape=jax.ShapeDtypeStruct((m, n_cols), out_dtype),
        scratch_shapes=[pltpu.VMEM((k, tn), BF16)],
        compiler_params=_params(2),
        name="matmul",
    )(x, w)


def _qkv_kernel(x_ref, w_ref, cos_ref, sin_ref, o_ref, wbf_ref, *, rope):
    @pl.when(pl.program_id(1) == 0)
    def _():
        wbf_ref[...] = w_ref[...].astype(BF16)

    def plain():
        o_ref[...] = jnp.dot(x_ref[...], wbf_ref[...], preferred_element_type=F32).astype(BF16)

    if not rope:
        plain()
        return
    c = pl.program_id(0)

    @pl.when(c < 2)
    def _():
        cs = cos_ref[...]
        sn = sin_ref[...]
        lane = lax.broadcasted_iota(jnp.int32, cs.shape, 1)
        first_half = (lane % ROPE_AXIS_DIM) < (ROPE_AXIS_DIM // 2)
        for pair in range(N_HEADS // 2):
            cols = slice(pair * 2 * HEAD_DIM, (pair + 1) * 2 * HEAD_DIM)
            acc = jnp.dot(x_ref[...], wbf_ref[:, cols], preferred_element_type=F32)
            for h in range(2):
                t = acc[:, h * HEAD_DIM:(h + 1) * HEAD_DIM]
                partner = jnp.where(first_half,
                                    pltpu.roll(t, HEAD_DIM - ROPE_AXIS_DIM // 2, 1),
                                    pltpu.roll(t, ROPE_AXIS_DIM // 2, 1))
                lo = (pair * 2 + h) * HEAD_DIM
                o_ref[:, lo:lo + HEAD_DIM] = (t * cs + partner * sn).astype(BF16)

    @pl.when(c == 2)
    def _():
        plain()


def rope_tables(n):
    pos = jnp.arange(n)
    rc = jnp.stack([pos // GRID_W, pos % GRID_W], axis=-1).astype(F32)
    inv = ROPE_THETA ** (-jnp.arange(0, ROPE_AXIS_DIM, 2, dtype=F32) / ROPE_AXIS_DIM)
    ang = rc[:, :, None] * inv
    cos, sin = jnp.cos(ang), jnp.sin(ang)
    cos_t = jnp.concatenate([cos, cos], axis=-1).reshape(n, HEAD_DIM)
    sin_t = jnp.concatenate([-sin, sin], axis=-1).reshape(n, HEAD_DIM)
    return cos_t, sin_t


def qkv_proj(x, w, layer, n, cos_t, sin_t, rope, first_tile=0, n_tiles=3, tm=1024):
    m, k = x.shape
    tm = min(tm, n if rope else m)
    nt = n // tm if rope else 1
    tab = pl.BlockSpec((tm, HEAD_DIM), lambda c, i: (i % nt, 0))
    return pl.pallas_call(
        functools.partial(_qkv_kernel, rope=rope),
        grid=(n_tiles, m // tm),
        in_specs=[
            pl.BlockSpec((tm, k), lambda c, i: (i, 0)),
            pl.BlockSpec((None, k, D_BRANCH), lambda c, i: (layer, 0, first_tile + c)),
            tab, tab,
        ],
        out_specs=pl.BlockSpec((tm, D_BRANCH), lambda c, i: (i, c)),
        out_shape=jax.ShapeDtypeStruct((m, n_tiles * D_BRANCH), BF16),
        scratch_shapes=[pltpu.VMEM((k, D_BRANCH), BF16)],
        compiler_params=_params(2),
        name="qkv_proj",
    )(x, w, cos_t, sin_t)


def _nt_dot(a, b):
    return lax.dot_general(a, b, (((1,), (1,)), ((), ())), preferred_element_type=F32)


def _na_pattern_rows(g, rows):
    band_row = int(np.clip(g * NA_Q_ROWS - NA_KH // 2, 0, rows - NA_BAND_ROWS))
    qr = g * NA_Q_ROWS + np.arange(NA_Q_ROWS)
    rs = np.clip(qr - NA_KH // 2, 0, rows - NA_KH)
    return band_row, qr, rs


def _na_bias_index(rows):
    n_dc = 2 * NA_KW - 1
    masked = (2 * NA_KH - 1) * n_dc
    pats = []
    for g in (0, 1, rows // NA_Q_ROWS - 1):
        band_row, _, _ = _na_pattern_rows(g, rows)
        qi, kj = np.arange(NA_TQ), np.arange(NA_TK)
        qr, qc = (g * NA_Q_ROWS + qi // GRID_W)[:, None], (qi % GRID_W)[:, None]
        kr, kc = (band_row + kj // GRID_W)[None], (kj % GRID_W)[None]
        rs = np.clip(qr - NA_KH // 2, 0, rows - NA_KH)
        ws = np.clip(qc - NA_KW // 2, 0, GRID_W - NA_KW)
        ok = (kr >= rs) & (kr < rs + NA_KH) & (kc >= ws) & (kc < ws + NA_KW)
        pats.append(np.where(ok, (kr - qr + NA_KH - 1) * n_dc + (kc - qc + NA_KW - 1), masked))
    return np.concatenate(pats, axis=0).astype(np.int32), masked


def na_bias(na_rpb, rows, after):
    nl, nh, ndr, ndc = na_rpb.shape
    band1, qr1, rs1 = _na_pattern_rows(1, rows)
    for g in range(2, rows // NA_Q_ROWS - 1):
        band, qr, rs = _na_pattern_rows(g, rows)
        assert (qr - band == qr1 - band1).all() and (rs - band == rs1 - band1).all()
    index, masked = _na_bias_index(rows)
    lanes = V7X_SC_LANES
    tab = -(-(masked + 1) // lanes) * lanes
    workers = V7X_SC_CORES * V7X_SC_SUBCORES
    per_head = workers // (nl * nh)
    rows_w = index.shape[0] // per_head
    chunk = NA_TQ // 8
    assert workers % (nl * nh) == 0 and index.shape[0] % per_head == 0 and rows_w % chunk == 0

    def body(tab_hbm, idx_hbm, after_hbm, out_hbm, tab_v, idx_v, out_v):
        del after_hbm
        worker = lax.axis_index("s") * V7X_SC_CORES + lax.axis_index("c")
        lh = worker // per_head
        part = worker - lh * per_head
        layer = lh // nh
        head = lh - layer * nh
        pltpu.sync_copy(tab_hbm.at[lh], tab_v)
        lane = lax.iota(jnp.int32, lanes)
        for t in range(tab // lanes):
            seg = pl.ds(t * lanes, lanes)
            tab_v[seg] = jnp.where(lane + t * lanes == masked, NEG_INF, tab_v[seg] * LOG2_E)

        @pl.loop(0, rows_w // chunk)
        def _(c):
            r0 = part * rows_w + c * chunk
            pat = r0 // NA_TQ
            pltpu.sync_copy(idx_hbm.at[pl.ds(r0, chunk)], idx_v)

            @pl.loop(0, chunk)
            def _(r):
                for v in range(NA_TK // lanes):
                    cols = pl.ds(v * lanes, lanes)
                    out_v[r, cols] = plsc.load_gather(tab_v, [idx_v[r, cols]])

            dst = ((layer * 3 + pat) * nh + head) * NA_TQ + (r0 - pat * NA_TQ)
            pltpu.sync_copy(out_v, out_hbm.at[pl.ds(dst, chunk)])

    table = jnp.pad(na_rpb.reshape(nl * nh, ndr * ndc), ((0, 0), (0, tab - ndr * ndc)))
    out = pl.kernel(
        body,
        mesh=plsc.VectorSubcoreMesh(core_axis_name="c", subcore_axis_name="s"),
        out_type=jax.ShapeDtypeStruct((nl * 3 * nh * NA_TQ, NA_TK), F32),
        scratch_types=[pltpu.VMEM((tab,), F32), pltpu.VMEM((chunk, NA_TK), jnp.int32),
                       pltpu.VMEM((chunk, NA_TK), F32)],
        compiler_params=pltpu.CompilerParams(needs_layout_passes=False),
        name="na_bias",
    )(table, jnp.asarray(index), after)
    return out.reshape(nl, 3, nh, NA_TQ, NA_TK)


def _na_kernel(q_ref, k_ref, v_ref, kc_ref, vc_ref, bias_ref, o_ref):
    g = pl.program_id(2)
    n_groups = k_ref.shape[1] // NA_TQ
    band_row = jnp.clip(g * NA_Q_ROWS - NA_KH // 2, 0, n_groups * NA_Q_ROWS - NA_BAND_ROWS)
    start = pl.multiple_of(band_row * GRID_W, NA_TQ)
    pat = jnp.where(g == 0, 0, jnp.where(g == n_groups - 1, 2, 1))
    scale = HEAD_DIM ** -0.5 * LOG2_E

    def scores(hh):
        cols = slice(hh * HEAD_DIM, (hh + 1) * HEAD_DIM)
        q = q_ref[0, :, cols]
        return _nt_dot(q, k_ref[0, pl.ds(start, NA_TK), cols]), _nt_dot(q, kc_ref[0, :, cols])

    def finish(hh, qk_loc, qk_ctx):
        cols = slice(hh * HEAD_DIM, (hh + 1) * HEAD_DIM)
        s_loc = qk_loc * scale + bias_ref[0, pat, hh]
        s_ctx = qk_ctx * scale
        m = jnp.maximum(jnp.max(s_loc, axis=1, keepdims=True), jnp.max(s_ctx, axis=1, keepdims=True))
        e_loc = jnp.exp2(s_loc - m)
        e_ctx = jnp.exp2(s_ctx - m)
        inv = 1.0 / (jnp.sum(e_loc, axis=1, keepdims=True) + jnp.sum(e_ctx, axis=1, keepdims=True))
        o = jnp.dot(e_loc.astype(BF16), v_ref[0, pl.ds(start, NA_TK), cols], preferred_element_type=F32)
        o = o + jnp.dot(e_ctx.astype(BF16), vc_ref[0, :, cols], preferred_element_type=F32)
        o_ref[0, :, cols] = (o * inv).astype(o_ref.dtype)

    qk = {}
    for hh in range(NA_HEADS_PER_STEP + NA_SCORE_LEAD):
        if hh < NA_HEADS_PER_STEP:
            qk[hh] = scores(hh)
        if hh >= NA_SCORE_LEAD:
            finish(hh - NA_SCORE_LEAD, *qk.pop(hh - NA_SCORE_LEAD))


def na_attention(qkv, kvc, kc_col, bias, layer, b, n):
    lc = kvc.shape[0] // b
    qkv3 = qkv.reshape(b, n, QKV_COLS)
    kvc3 = kvc.reshape(b, lc, kvc.shape[1])
    hbs = D_BRANCH // NA_HB
    kcb = kc_col // NA_HB
    out = pl.pallas_call(
        _na_kernel,
        grid=(hbs, b, n // NA_TQ),
        in_specs=[
            pl.BlockSpec((1, NA_TQ, NA_HB), lambda h, i, g: (i, g, h)),
            pl.BlockSpec((1, n, NA_HB), lambda h, i, g: (i, 0, hbs + h)),
            pl.BlockSpec((1, n, NA_HB), lambda h, i, g: (i, 0, 2 * hbs + h)),
            pl.BlockSpec((1, lc, NA_HB), lambda h, i, g: (i, 0, kcb + h)),
            pl.BlockSpec((1, lc, NA_HB), lambda h, i, g: (i, 0, kcb + hbs + h)),
            pl.BlockSpec((1, 3, NA_HEADS_PER_STEP, NA_TQ, NA_TK), lambda h, i, g: (layer, 0, h, 0, 0),
                         pipeline_mode=pl.Buffered(1)),
        ],
        out_specs=pl.BlockSpec((1, NA_TQ, NA_HB), lambda h, i, g: (i, g, h)),
        out_shape=jax.ShapeDtypeStruct((b, n, D_BRANCH), BF16),
        compiler_params=_params(3),
        name="na_attention",
    )(qkv3, qkv3, qkv3, kvc3, kvc3, bias)
    return out.reshape(b * n, D_BRANCH)


def _ctx_attn_kernel(q_ref, k_ref, v_ref, o_ref):
    for h in range(N_HEADS):
        cols = slice(h * HEAD_DIM, (h + 1) * HEAD_DIM)
        s = _nt_dot(q_ref[0, :, cols], k_ref[0, :, cols]) * (HEAD_DIM ** -0.5)
        m = jnp.max(s, axis=1, keepdims=True)
        e = jnp.exp(s - m)
        p = e * (1.0 / jnp.sum(e, axis=1, keepdims=True))
        o_ref[0, :, cols] = jnp.dot(p.astype(BF16), v_ref[0, :, cols],
                                    preferred_element_type=F32).astype(o_ref.dtype)


def ctx_attention(qkv, b, lc):
    qkv3 = qkv.reshape(b, lc, QKV_COLS)
    part = lambda k: pl.BlockSpec((1, lc, D_BRANCH), lambda i: (i, 0, k))
    out = pl.pallas_call(
        _ctx_attn_kernel,
        grid=(b,),
        in_specs=[part(0), part(1), part(2)],
        out_specs=part(0),
        out_shape=jax.ShapeDtypeStruct((b, lc, D_BRANCH), BF16),
        compiler_params=_params(1),
        name="ctx_attention",
    )(qkv3, qkv3, qkv3)
    return out.reshape(b * lc, D_BRANCH)


def _conv_kernel(h_ref, wx_ref, wb_ref, wc_ref, w_ref, o_ref, wbf_ref, *, seq):
    @pl.when(pl.program_id(1) == 0)
    def _():
        for k, ref in enumerate((wx_ref, wb_ref, wc_ref)):
            wbf_ref[k] = ref[...].astype(BF16)

    h = h_ref[0]
    xc, bg, cg = (jnp.dot(h, wbf_ref[k], preferred_element_type=F32) for k in range(3))
    z = cg * xc
    rows = z.shape[0]
    pos = lax.broadcasted_iota(jnp.int32, z.shape, 0) % seq
    z_prev = jnp.where(pos == 0, 0.0, pltpu.roll(z, 1, 0))
    z_next = jnp.where(pos == seq - 1, 0.0, pltpu.roll(z, rows - 1, 0))
    y = z_prev * w_ref[0, 0:1, :] + z * w_ref[0, 1:2, :] + z_next * w_ref[0, 2:3, :]
    o_ref[0] = (bg * y).astype(o_ref.dtype)


def short_conv(h, w_in, conv_w, layer, n, tc=256, rows=2048):
    m, d = h.shape
    rows = min(max(rows, n), m)
    assert rows % n == 0 and m % rows == 0
    wcol = lambda col: pl.BlockSpec((None, d, tc), lambda j, i: (layer, 0, col // tc + j))
    out = pl.pallas_call(
        functools.partial(_conv_kernel, seq=n),
        grid=(D_BRANCH // tc, m // rows),
        in_specs=[pl.BlockSpec((1, rows, d), lambda j, i: (i, 0, 0)),
                  wcol(COL_XC), wcol(COL_BG), wcol(COL_CG),
                  pl.BlockSpec((1, 3, tc), lambda j, i: (layer, 0, j))],
        out_specs=pl.BlockSpec((1, rows, tc), lambda j, i: (i, 0, j)),
        out_shape=jax.ShapeDtypeStruct((m // rows, rows, D_BRANCH), BF16),
        scratch_shapes=[pltpu.VMEM((3, d, tc), BF16)],
        compiler_params=_params(2),
        name="short_conv",
    )(h.reshape(m // rows, rows, d), w_in, w_in, w_in, conv_w)
    return out.reshape(m, D_BRANCH)


def _gmlp_kernel(u_ref, v_ref, g_ref, ws_ref, bs_ref, o_ref):
    v = jax.nn.gelu(v_ref[...])
    mu = jnp.mean(v, axis=-1, keepdims=True)
    var = jnp.mean(jnp.square(v - mu), axis=-1, keepdims=True)
    vn = ((v - mu) * lax.rsqrt(var + NORM_EPS) * g_ref[...]).astype(BF16)
    for grp in range(GMLP_GROUPS):
        cols = slice(grp * 128, (grp + 1) * 128)
        ws = ws_ref[0, grp].astype(BF16)
        for ch in range(v.shape[0] // CHUNK):
            rows = slice(ch * CHUNK, (ch + 1) * CHUNK)
            s = jnp.dot(ws, vn[rows, cols], preferred_element_type=F32) + bs_ref[:, grp:grp + 1]
            o_ref[rows, cols] = (jax.nn.gelu(u_ref[rows, cols]) * s).astype(o_ref.dtype)


def chunk_gmlp(p, ln_g, w_s, b_s, layer, tm=8 * CHUNK):
    m = p.shape[0]
    return pl.pallas_call(
        _gmlp_kernel,
        grid=(m // tm,),
        in_specs=[
            pl.BlockSpec((tm, D_BRANCH), lambda i: (i, COL_U // D_BRANCH)),
            pl.BlockSpec((tm, D_BRANCH), lambda i: (i, COL_VS // D_BRANCH)),
            pl.BlockSpec((1, D_BRANCH), lambda i: (0, 0)),
            pl.BlockSpec((1, GMLP_GROUPS, CHUNK, CHUNK), lambda i: (layer, 0, 0, 0)),
            pl.BlockSpec((CHUNK, GMLP_GROUPS), lambda i: (0, 0)),
        ],
        out_specs=pl.BlockSpec((tm, D_BRANCH), lambda i: (i, 0)),
        out_shape=jax.ShapeDtypeStruct((m, D_BRANCH), BF16),
        compiler_params=_params(1),
        name="chunk_gmlp",
    )(p, p, ln_g[layer].reshape(1, D_BRANCH), w_s, b_s[layer].T)


def _merge_kernel(a_ref, c_ref, m_ref, g0_ref, g1_ref, g2_ref, w_ref, *rest):
    o_ref, wbf_ref = rest[-2:]

    @pl.when(pl.program_id(1) == 0)
    def _():
        wbf_ref[...] = w_ref[...].astype(BF16)

    acc = None
    for i, (br, gl) in enumerate(((a_ref, g0_ref), (c_ref, g1_ref), (m_ref, g2_ref))):
        proj = jnp.dot(br[...], wbf_ref[i], preferred_element_type=F32)
        term = jax.nn.sigmoid(gl[...]) * proj
        acc = term if acc is None else acc + term
    o_ref[...] = acc.astype(o_ref.dtype)


def merge_branches(attn, conv, gm, p, w_branch, layer, after=None, tm=512, tn=1024):
    m = attn.shape[0]
    order = [] if after is None else [after]
    br = pl.BlockSpec((tm, D_BRANCH), lambda j, i: (i, 0))
    gate = lambda k: pl.BlockSpec((tm, tn), lambda j, i: (i, (COL_GATE + k * D_MODEL) // tn + j))
    return pl.pallas_call(
        _merge_kernel,
        grid=(D_MODEL // tn, m // tm),
        in_specs=[br, br, br, gate(0), gate(1), gate(2),
                  pl.BlockSpec((None, N_BRANCHES, D_BRANCH, tn), lambda j, i: (layer, 0, 0, j))]
                 + [pl.BlockSpec(memory_space=pl.ANY)] * len(order),
        out_specs=pl.BlockSpec((tm, tn), lambda j, i: (i, j)),
        out_shape=jax.ShapeDtypeStruct((m, D_MODEL), BF16),
        scratch_shapes=[pltpu.VMEM((N_BRANCHES, D_BRANCH, tn), BF16)],
        compiler_params=_params(2),
        name="merge_branches",
    )(attn, conv, gm, p, p, p, w_branch, *order)


def _out_proj_router_kernel(h_ref, w_ref, x_ref, mod_ref, g_ref, sh_ref, sc_ref, wr_ref, br_ref,
                            o_ref, hw_ref, aff_ref, wbf_ref):
    @pl.when(pl.program_id(0) == 0)
    def _():
        wbf_ref[...] = w_ref[...].astype(BF16)

    x = x_ref[...] + mod_ref[0] * jnp.dot(h_ref[...], wbf_ref[...], preferred_element_type=F32)
    o_ref[...] = x
    y = _rms(x, g_ref[...])
    h = (y * (1 + sc_ref[0]) + sh_ref[0]).astype(BF16)
    bits = pltpu.bitcast(h.astype(F32), jnp.uint32)
    half = bits.shape[1] // 2
    hw_ref[...] = lax.shift_right_logical(bits[:, :half], jnp.uint32(16)) | bits[:, half:]
    logits = _nt_dot(wr_ref[...].astype(BF16), h) + br_ref[...]
    mx = jnp.max(logits, axis=0, keepdims=True)
    e = jnp.exp(logits - mx)
    aff_ref[0] = e / jnp.sum(e, axis=0, keepdims=True)


def out_proj_router(h, w_out, layer, x, mod, g, w_router, b_router, n, tm=512):
    m, d = x.shape
    b = m // n
    tm = min(tm, n)
    per_b = n // tm
    vec = lambda k: (pl.BlockSpec((1, 1, d), lambda i: (i // per_b, 0, 0)), mod[:, k].reshape(b, 1, d))
    (gate_s, gate_v), (sh_s, sh_v), (sc_s, sc_v) = vec(2), vec(3), vec(4)
    return pl.pallas_call(
        _out_proj_router_kernel,
        grid=(m // tm,),
        in_specs=[
            pl.BlockSpec((tm, d), lambda i: (i, 0)),
            pl.BlockSpec((None, d, d), lambda i: (layer, 0, 0), pipeline_mode=pl.Buffered(1)),
            pl.BlockSpec((tm, d), lambda i: (i, 0)),
            gate_s,
            pl.BlockSpec((1, d), lambda i: (0, 0)),
            sh_s, sc_s,
            pl.BlockSpec((N_EXPERTS, d), lambda i: (0, 0)),
            pl.BlockSpec((N_EXPERTS, 1), lambda i: (0, 0)),
        ],
        out_specs=[
            pl.BlockSpec((tm, d), lambda i: (i, 0)),
            pl.BlockSpec((tm, d // 2), lambda i: (i, 0)),
            pl.BlockSpec((1, N_EXPERTS, tm), lambda i: (i // per_b, 0, i % per_b)),
        ],
        out_shape=[jax.ShapeDtypeStruct((m, d), F32),
                   jax.ShapeDtypeStruct((m, d // 2), jnp.uint32),
                   jax.ShapeDtypeStruct((b, N_EXPERTS, n), F32)],
        scratch_shapes=[pltpu.VMEM((d, d), BF16)],
        compiler_params=_params(1),
        name="out_proj_router",
    )(h, w_out, x, gate_v, g.reshape(1, d), sh_v, sc_v, w_router.T, b_router.reshape(N_EXPERTS, 1))


def _select_kernel(aff_ref, tri_ref, slot_ref, *, cap):
    bits = pltpu.bitcast(aff_ref[...], jnp.int32)

    def step(i, t):
        cand = t | lax.shift_left(jnp.int32(1), 30 - i)
        cnt = jnp.sum(jnp.where(bits >= cand, 1.0, 0.0), axis=1, keepdims=True)
        return jnp.where(cnt >= cap, cand, t)

    t = lax.fori_loop(0, 31, step, jnp.zeros((bits.shape[0], 1), jnp.int32))
    gt = bits > t
    eq = bits == t
    need = cap - jnp.sum(jnp.where(gt, 1.0, 0.0), axis=1, keepdims=True)
    tri = tri_ref[...]
    eq_rank = jnp.dot(jnp.where(eq, 1.0, 0.0).astype(BF16), tri, preferred_element_type=F32)
    sel = gt | (eq & (eq_rank < need))
    slot = jnp.dot(jnp.where(sel, 1.0, 0.0).astype(BF16), tri, preferred_element_type=F32)
    slot_ref[...] = jnp.where(sel, slot.astype(jnp.int32), -1)


def select_tokens(aff, cap):
    b, e, n = aff.shape
    tri = jnp.triu(jnp.ones((n, n), BF16), k=1)
    rows = pl.BlockSpec((b * e, n), lambda i: (0, 0))
    return pl.pallas_call(
        functools.partial(_select_kernel, cap=cap),
        grid=(1,),
        in_specs=[rows, pl.BlockSpec((n, n), lambda i: (0, 0))],
        out_specs=rows,
        out_shape=jax.ShapeDtypeStruct((b * e, n), jnp.int32),
        compiler_params=_params(1),
        name="select_tokens",
    )(aff.reshape(b * e, n), tri).reshape(b, e, n)


def _onehot(slot, cap):
    hit = lax.broadcasted_iota(jnp.int32, (cap, slot.shape[1]), 0) == slot
    return hit, jnp.where(hit, 1.0, 0.0)


def gather_tokens(h, slot, aff, cap):
    b, n, w = h.shape
    e = N_EXPERTS
    workers = V7X_SC_CORES * V7X_SC_SUBCORES
    rows_per_worker = (b * e) // workers
    word_bytes = jnp.dtype(h.dtype).itemsize
    chunk = min(cap, V7X_SC_SUBCORE_VMEM_BYTES // 2 // (w * word_bytes))
    assert (b * e) % workers == 0 and cap % chunk == 0 and n % V7X_SC_LANES == 0 and word_bytes == 4

    def body(h_hbm, slot_hbm, aff_hbm, xs_hbm, gates_hbm, slot_v, aff_v, idx_v, gate_v, rows_v, sem):
        worker = lax.axis_index("s") * V7X_SC_CORES + lax.axis_index("c")
        lane = lax.iota(jnp.int32, V7X_SC_LANES)

        @pl.loop(0, rows_per_worker)
        def _(j):
            r = worker * rows_per_worker + j
            bi = r // e
            ei = r - bi * e
            pltpu.sync_copy(slot_hbm.at[r], slot_v)
            pltpu.sync_copy(aff_hbm.at[r], aff_v)

            @pl.loop(0, n // V7X_SC_LANES)
            def _(c):
                tokens = pl.ds(c * V7X_SC_LANES, V7X_SC_LANES)
                s = slot_v[tokens]
                chosen = s >= 0
                plsc.store_scatter(idx_v, [s], lane + (c * V7X_SC_LANES + bi * n), mask=chosen)
                plsc.store_scatter(gate_v, [s], aff_v[tokens], mask=chosen)

            out_row = (ei * b + bi) * cap

            @pl.loop(0, cap // chunk)
            def _(q):
                pltpu.async_copy(h_hbm.at[idx_v.at[pl.ds(q * chunk, chunk)]], rows_v, sem).wait()
                pltpu.sync_copy(rows_v, xs_hbm.at[pl.ds(out_row + q * chunk, chunk)])

            pltpu.sync_copy(gate_v, gates_hbm.at[pl.ds(out_row, cap)])

    xs, gates = pl.kernel(
        body,
        mesh=plsc.VectorSubcoreMesh(core_axis_name="c", subcore_axis_name="s"),
        out_type=[jax.ShapeDtypeStruct((e * b * cap, w), h.dtype),
                  jax.ShapeDtypeStruct((e * b * cap,), F32)],
        scratch_types=[
            pltpu.VMEM((n,), jnp.int32), pltpu.VMEM((n,), F32),
            pltpu.VMEM((cap,), jnp.int32), pltpu.VMEM((cap,), F32),
            pltpu.VMEM((chunk, w), h.dtype),
            pltpu.SemaphoreType.DMA,
        ],
        compiler_params=pltpu.CompilerParams(needs_layout_passes=False),
        name="gather_tokens",
    )(h.reshape(b * n, w), slot.reshape(b * e, n), aff.reshape(b * e, n))
    return xs.reshape(e, b * cap, w), gates.reshape(e, b * cap, 1)


def _ffn_kernel(*refs, n_sets, n_f):
    xs_refs, gate_refs = refs[0:2 * n_sets:2], refs[1:2 * n_sets:2]
    wg_ref, wu_ref, wd_ref = refs[2 * n_sets:2 * n_sets + 3]
    y_refs = refs[2 * n_sets + 3:3 * n_sets + 3]
    act_refs = refs[3 * n_sets + 3:4 * n_sets + 3]
    row_refs = refs[4 * n_sets + 3:]
    s = pl.program_id(1)
    tf = wg_ref.shape[1]

    @pl.when(s == 0)
    def _():
        for xs_ref, row_ref in zip(xs_refs, row_refs):
            words = xs_ref[0]
            half = words.shape[1]
            row_ref[:, :half] = pltpu.bitcast(lax.shift_left(words, jnp.uint32(16)), F32).astype(BF16)
            row_ref[:, half:] = pltpu.bitcast(words & jnp.uint32(0xFFFF0000), F32).astype(BF16)

    @pl.when(s < n_f)
    def _():
        wg = wg_ref[...].astype(BF16)
        wu = wu_ref[...].astype(BF16)
        for row_ref, act_ref in zip(row_refs, act_refs):
            xs = row_ref[...]
            hg = jnp.dot(xs, wg, preferred_element_type=F32)
            hu = jnp.dot(xs, wu, preferred_element_type=F32)
            act_ref[s] = (jax.nn.silu(hg) * hu).astype(BF16)

    @pl.when(s >= n_f)
    def _():
        wd = wd_ref[...].astype(BF16)
        for gate_ref, y_ref, act_ref in zip(gate_refs, y_refs, act_refs):
            y = None
            for f in range(n_f):
                part = jnp.dot(act_ref[f], wd[f * tf:(f + 1) * tf, :], preferred_element_type=F32)
                y = part if y is None else y + part
            y_ref[0] = (y * gate_ref[0]).astype(y_ref.dtype)


def expert_ffn(sets, w_gate, w_up, w_down, layer, tf=512, tn=512):
    d, ff = w_gate.shape[2], w_gate.shape[3]
    n_f, n_d = ff // tf, d // tn
    d_idx = lambda s: jnp.maximum(s - n_f, 0)

    def up_tile(i, s):
        ahead = s >= n_f
        return (layer, jnp.where(ahead, jnp.minimum(i + 1, N_EXPERTS - 1), i), 0,
                jnp.where(ahead, 0, jnp.minimum(s, n_f - 1)))

    def down_tile(i, s):
        behind = s < n_f
        return (layer, jnp.where(behind, jnp.maximum(i - 1, 0), i), 0, jnp.where(behind, n_d - 1, s - n_f))
    in_specs, operands, out_specs, out_shape, act_scratch, row_scratch = [], [], [], [], [], []
    for xs, gates in sets:
        r = xs.shape[1]
        in_specs += [pl.BlockSpec((1, r, d // 2), lambda i, s: (i, 0, 0)),
                     pl.BlockSpec((1, r, 1), lambda i, s: (i, 0, 0))]
        operands += [xs, gates]
        out_specs.append(pl.BlockSpec((1, r, tn), lambda i, s: (i, 0, d_idx(s))))
        out_shape.append(jax.ShapeDtypeStruct((N_EXPERTS, r, d), BF16))
        act_scratch.append(pltpu.VMEM((n_f, r, tf), BF16))
        row_scratch.append(pltpu.VMEM((r, d), BF16))
    in_specs += [
        pl.BlockSpec((None, None, d, tf), up_tile),
        pl.BlockSpec((None, None, d, tf), up_tile),
        pl.BlockSpec((None, None, ff, tn), down_tile),
    ]
    return pl.pallas_call(
        functools.partial(_ffn_kernel, n_sets=len(sets), n_f=n_f),
        grid=(N_EXPERTS, n_f + n_d),
        in_specs=in_specs,
        out_specs=out_specs,
        out_shape=out_shape,
        scratch_shapes=act_scratch + row_scratch,
        compiler_params=_params(2),
        name="expert_ffn",
    )(*operands, w_gate, w_up, w_down)


def _combine_kernel(y_ref, slot_ref, x_ref, mod_ref, o_ref, onehot_ref, *, cap):
    @pl.when(pl.program_id(1) == 0)
    def _():
        for e in range(N_EXPERTS):
            onehot_ref[e * cap:(e + 1) * cap, :] = _onehot(slot_ref[e], cap)[1].astype(BF16)

    y = y_ref[...].reshape(N_EXPERTS * cap, y_ref.shape[2])
    moe = lax.dot_general(onehot_ref[...], y, (((0,), (0,)), ((), ())), preferred_element_type=F32)
    o_ref[0] = x_ref[0] + mod_ref[0] * moe


def combine(y, slot, x, gate_vec, cap, tn=512):
    b, n, d = x.shape
    e = N_EXPERTS
    return pl.pallas_call(
        functools.partial(_combine_kernel, cap=cap),
        grid=(b, d // tn),
        in_specs=[
            pl.BlockSpec((e, cap, tn), lambda i, j: (0, i, j)),
            pl.BlockSpec((e, 1, n), lambda i, j: (i, 0, 0)),
            pl.BlockSpec((1, n, tn), lambda i, j: (i, 0, j)),
            pl.BlockSpec((1, 1, tn), lambda i, j: (i, 0, j)),
        ],
        out_specs=pl.BlockSpec((1, n, tn), lambda i, j: (i, 0, j)),
        out_shape=jax.ShapeDtypeStruct((b, n, d), F32),
        scratch_shapes=[pltpu.VMEM((e * cap, n), BF16)],
        compiler_params=_params(2),
        name="combine",
    )(y, slot.reshape(b * e, 1, n), x, gate_vec.reshape(b, 1, d))


def moe_route(hw, aff):
    n = hw.shape[1]
    cap = CAPACITY_FACTOR * n // N_EXPERTS
    slot = select_tokens(aff, cap)
    return slot, cap, gather_tokens(hw, slot, aff, cap)


def mixer_block(x, h, attn, mod, n, layer, w_in, conv_w, ln_g, w_s, b_s, w_branch, w_out,
                g2, w_router, b_router, after=None):
    b = x.shape[0] // n
    p = matmul(h, w_in, layer, P_COL0, w_in.shape[2] - P_COL0)
    conv = short_conv(h, w_in, conv_w, layer, n)
    gm = chunk_gmlp(p, ln_g, w_s, b_s, layer)
    merged = merge_branches(attn, conv, gm, p, w_branch, layer, after=after)
    x_new, hw, aff = out_proj_router(merged, w_out, layer, x, mod, g2, w_router, b_router, n)
    return x_new.reshape(b, n, -1), hw.reshape(b, n, -1), aff


def kernel(x, c, ctx, c_ctx, w_mod, b_mod, norm1_g, w_in, na_rpb, conv_w, gmlp_ln_g, w_spatial, b_spatial,
           w_branch, w_out, norm2_g, w_router, b_router, w_e_gate, w_e_up, w_e_down, final_g):
    b, n, d = x.shape
    lc = ctx.shape[1]
    cvec = jnp.concatenate([c, c_ctx[None], jnp.zeros((8 - b - 1, d), F32)], axis=0)
    modv = modvec(cvec, w_mod, b_mod).reshape(DEPTH, 8, 6, d)
    cos_t, sin_t = rope_tables(n)
    mix_w = (w_in, conv_w, gmlp_ln_g, w_spatial, b_spatial, w_branch, w_out)
    ctx_s = ctx
    for layer in range(DEPTH):
        last = layer == DEPTH - 1
        mod = modv[layer, :b]
        mod_c = jnp.broadcast_to(modv[layer, b][None], (b, 6, d))
        layer_w = mix_w + (norm2_g[layer], w_router[layer], b_router[layer])

        hc = norm_mod(ctx_s, norm1_g[layer], mod_c[:, 0], mod_c[:, 1]).reshape(b * lc, d)
        if last:
            kvc = qkv_proj(hc, w_in, layer, lc, cos_t, sin_t, rope=False, first_tile=1, n_tiles=2)
            kc_col = 0
        else:
            kvc = qkv_proj(hc, w_in, layer, lc, cos_t, sin_t, rope=False)
            kc_col = D_BRANCH
            attn_c = ctx_attention(kvc, b, lc)

        h = norm_mod(x, norm1_g[layer], mod[:, 0], mod[:, 1]).reshape(b * n, d)
        qkv = qkv_proj(h, w_in, layer, n, cos_t, sin_t, rope=True)
        if layer == 0:
            bias = na_bias(na_rpb, n // GRID_W, after=qkv)
        attn = na_attention(qkv, kvc, kc_col, bias, layer, b, n)
        x, hw, aff = mixer_block(x.reshape(b * n, d), h, attn, mod, n, layer, *layer_w)

        slot, cap, rows = moe_route(hw, aff)
        if last:
            (y,) = expert_ffn([rows], w_e_gate, w_e_up, w_e_down, layer)
        else:
            ctx_mid, hw_c, aff_c = mixer_block(ctx_s.reshape(b * lc, d), hc, attn_c, mod_c, lc, layer, *layer_w,
                                               after=slot)
            slot_c, cap_c, rows_c = moe_route(hw_c, aff_c)
            y, y_c = expert_ffn([rows, rows_c], w_e_gate, w_e_up, w_e_down, layer)
            ctx_s = combine(y_c, slot_c, ctx_mid, mod_c[:, 5], cap_c)
        x = combine(y, slot, x, mod[:, 5], cap)
    return final_norm(x, final_g)
```

```python
import functools

import numpy as np
import jax
import jax.numpy as jnp
from jax import lax
from jax.experimental import pallas as pl
from jax.experimental.pallas import tpu as pltpu
from jax.experimental.pallas import tpu_sc as plsc

D_MODEL = 2048
DEPTH = 2
GRID_W = 64
D_BRANCH = D_MODEL // 2
N_BRANCHES = 3
HEAD_DIM = 128
N_HEADS = D_BRANCH // HEAD_DIM
NA_KH = 8
NA_KW = 16
ROPE_THETA = 10000.0
ROPE_AXIS_DIM = HEAD_DIM // 2
CHUNK = 128
GMLP_GROUPS = D_BRANCH // 128
N_EXPERTS = 16
CAPACITY_FACTOR = 2
NORM_EPS = 1e-6
NEG_INF = -1e30
LOG2_E = float(np.log2(np.e))

QKV_COLS = 3 * D_BRANCH
COL_XC, COL_BG, COL_CG = (QKV_COLS + i * D_BRANCH for i in range(3))
P_COL0 = QKV_COLS + 3 * D_BRANCH
COL_U, COL_VS, COL_GATE = 0, D_BRANCH, 2 * D_BRANCH

NA_Q_ROWS = 4
NA_BAND_ROWS = 12
NA_TQ = NA_Q_ROWS * GRID_W
NA_TK = NA_BAND_ROWS * GRID_W
NA_HEADS_PER_STEP = N_HEADS
NA_HB = NA_HEADS_PER_STEP * HEAD_DIM
NA_SCORE_LEAD = 3

V7X_VMEM_LIMIT = 56 * 1024 * 1024
V7X_SC_CORES, V7X_SC_SUBCORES, V7X_SC_LANES = 2, 16, 16
V7X_SC_SUBCORE_VMEM_BYTES = 512 * 1024

BF16 = jnp.bfloat16
F32 = jnp.float32


def _params(n_axes, vmem=V7X_VMEM_LIMIT):
    return pltpu.CompilerParams(dimension_semantics=("arbitrary",) * n_axes, vmem_limit_bytes=vmem)


def _modvec_kernel(c_ref, w_ref, b_ref, o_ref):
    s = jax.nn.silu(c_ref[...]).astype(BF16)
    o_ref[0] = jnp.dot(s, w_ref[0].astype(BF16), preferred_element_type=F32) + b_ref[0]


def modvec(cvec, w_mod, b_mod, tn=1024):
    nl, d, n6 = w_mod.shape
    return pl.pallas_call(
        _modvec_kernel,
        grid=(nl, n6 // tn),
        in_specs=[
            pl.BlockSpec((8, d), lambda l, j: (0, 0)),
            pl.BlockSpec((1, d, tn), lambda l, j: (l, 0, j)),
            pl.BlockSpec((1, 1, tn), lambda l, j: (l, 0, j)),
        ],
        out_specs=pl.BlockSpec((1, 8, tn), lambda l, j: (l, 0, j)),
        out_shape=jax.ShapeDtypeStruct((nl, 8, n6), F32),
        compiler_params=_params(2),
        name="modvec",
    )(cvec, w_mod, b_mod.reshape(nl, 1, n6))


def _rms(x, g):
    ms = jnp.mean(x * x, axis=-1, keepdims=True)
    return x * lax.rsqrt(ms + NORM_EPS) * g


def _norm_mod_kernel(x_ref, g_ref, sh_ref, sc_ref, o_ref):
    y = _rms(x_ref[0], g_ref[...])
    o_ref[0] = (y * (1 + sc_ref[0]) + sh_ref[0]).astype(o_ref.dtype)


def _norm_kernel(x_ref, g_ref, o_ref):
    o_ref[0] = _rms(x_ref[0], g_ref[...]).astype(o_ref.dtype)


def norm_mod(x, g, shift, scale, out_dtype=BF16, tm=1024):
    b, n, d = x.shape
    tm = min(tm, n)
    return pl.pallas_call(
        _norm_mod_kernel,
        grid=(b, n // tm),
        in_specs=[
            pl.BlockSpec((1, tm, d), lambda i, j: (i, j, 0)),
            pl.BlockSpec((1, d), lambda i, j: (0, 0)),
            pl.BlockSpec((1, 1, d), lambda i, j: (i, 0, 0)),
            pl.BlockSpec((1, 1, d), lambda i, j: (i, 0, 0)),
        ],
        out_specs=pl.BlockSpec((1, tm, d), lambda i, j: (i, j, 0)),
        out_shape=jax.ShapeDtypeStruct((b, n, d), out_dtype),
        compiler_params=_params(2),
        name="norm_mod",
    )(x, g.reshape(1, d), shift.reshape(b, 1, d), scale.reshape(b, 1, d))


def final_norm(x, g, tm=1024):
    b, n, d = x.shape
    return pl.pallas_call(
        _norm_kernel,
        grid=(b, n // tm),
        in_specs=[
            pl.BlockSpec((1, tm, d), lambda i, j: (i, j, 0)),
            pl.BlockSpec((1, d), lambda i, j: (0, 0)),
        ],
        out_specs=pl.BlockSpec((1, tm, d), lambda i, j: (i, j, 0)),
        out_shape=jax.ShapeDtypeStruct((b, n, d), F32),
        compiler_params=_params(2),
        name="final_norm",
    )(x, g.reshape(1, d))


def _mm_kernel(x_ref, w_ref, o_ref, wbf_ref):
    @pl.when(pl.program_id(1) == 0)
    def _():
        wbf_ref[...] = w_ref[...].astype(BF16)

    o_ref[...] = jnp.dot(x_ref[...], wbf_ref[...], preferred_element_type=F32).astype(o_ref.dtype)


def matmul(x, w, layer, col_off, n_cols, out_dtype=F32, tm=1024, tn=1024):
    m, k = x.shape
    tm = min(tm, m)
    off = col_off // tn
    return pl.pallas_call(
        _mm_kernel,
        grid=(n_cols // tn, m // tm),
        in_specs=[
            pl.BlockSpec((tm, k), lambda j, i: (i, 0)),
            pl.BlockSpec((None, k, tn), lambda j, i: (layer, 0, off + j)),
        ],
        out_specs=pl.BlockSpec((tm, tn), lambda j, i: (i, j)),
        out_shape=jax.ShapeDtypeStruct((m, n_cols), out_dtype),
        scratch_shapes=[pltpu.VMEM((k, tn), BF16)],
        compiler_params=_params(2),
        name="matmul",
    )(x, w)


def _qkv_kernel(x_ref, w_ref, cos_ref, sin_ref, *rest, rope):
    o_ref, wbf_ref = rest[-2:]

    @pl.when(pl.program_id(1) == 0)
    def _():
        wbf_ref[...] = w_ref[...].astype(BF16)

    def plain():
        o_ref[...] = jnp.dot(x_ref[...], wbf_ref[...], preferred_element_type=F32).astype(BF16)

    if not rope:
        plain()
        return
    c = pl.program_id(0)

    @pl.when(c < 2)
    def _():
        cs = cos_ref[...]
        sn = sin_ref[...]
        lane = lax.broadcasted_iota(jnp.int32, cs.shape, 1)
        first_half = (lane % ROPE_AXIS_DIM) < (ROPE_AXIS_DIM // 2)
        for pair in range(N_HEADS // 2):
            cols = slice(pair * 2 * HEAD_DIM, (pair + 1) * 2 * HEAD_DIM)
            acc = jnp.dot(x_ref[...], wbf_ref[:, cols], preferred_element_type=F32)
            for h in range(2):
                t = acc[:, h * HEAD_DIM:(h + 1) * HEAD_DIM]
                partner = jnp.where(first_half,
                                    pltpu.roll(t, HEAD_DIM - ROPE_AXIS_DIM // 2, 1),
                                    pltpu.roll(t, ROPE_AXIS_DIM // 2, 1))
                lo = (pair * 2 + h) * HEAD_DIM
                o_ref[:, lo:lo + HEAD_DIM] = (t * cs + partner * sn).astype(BF16)

    @pl.when(c == 2)
    def _():
        plain()


def rope_tables(n):
    pos = jnp.arange(n)
    rc = jnp.stack([pos // GRID_W, pos % GRID_W], axis=-1).astype(F32)
    inv = ROPE_THETA ** (-jnp.arange(0, ROPE_AXIS_DIM, 2, dtype=F32) / ROPE_AXIS_DIM)
    ang = rc[:, :, None] * inv
    cos, sin = jnp.cos(ang), jnp.sin(ang)
    cos_t = jnp.concatenate([cos, cos], axis=-1).reshape(n, HEAD_DIM)
    sin_t = jnp.concatenate([-sin, sin], axis=-1).reshape(n, HEAD_DIM)
    return cos_t, sin_t


def qkv_proj(x, w, layer, n, cos_t, sin_t, rope, first_tile=0, n_tiles=3, after=(), tm=1024):
    m, k = x.shape
    tm = min(tm, n if rope else m)
    nt = n // tm if rope else 1
    tab = pl.BlockSpec((tm, HEAD_DIM), lambda c, i: (i % nt, 0))
    return pl.pallas_call(
        functools.partial(_qkv_kernel, rope=rope),
        grid=(n_tiles, m // tm),
        in_specs=[
            pl.BlockSpec((tm, k), lambda c, i: (i, 0)),
            pl.BlockSpec((None, k, D_BRANCH), lambda c, i: (layer, 0, first_tile + c)),
            tab, tab,
        ] + [pl.BlockSpec(memory_space=pl.ANY)] * len(after),
        out_specs=pl.BlockSpec((tm, D_BRANCH), lambda c, i: (i, c)),
        out_shape=jax.ShapeDtypeStruct((m, n_tiles * D_BRANCH), BF16),
        scratch_shapes=[pltpu.VMEM((k, D_BRANCH), BF16)],
        compiler_params=_params(2),
        name="qkv_proj",
    )(x, w, cos_t, sin_t, *after)


def _nt_dot(a, b):
    return lax.dot_general(a, b, (((1,), (1,)), ((), ())), preferred_element_type=F32)


def _na_pattern_rows(g, rows):
    band_row = int(np.clip(g * NA_Q_ROWS - NA_KH // 2, 0, rows - NA_BAND_ROWS))
    qr = g * NA_Q_ROWS + np.arange(NA_Q_ROWS)
    rs = np.clip(qr - NA_KH // 2, 0, rows - NA_KH)
    return band_row, qr, rs


def _na_bias_index(rows):
    n_dc = 2 * NA_KW - 1
    masked = (2 * NA_KH - 1) * n_dc
    pats = []
    for g in (0, 1, rows // NA_Q_ROWS - 1):
        band_row, _, _ = _na_pattern_rows(g, rows)
        qi, kj = np.arange(NA_TQ), np.arange(NA_TK)
        qr, qc = (g * NA_Q_ROWS + qi // GRID_W)[:, None], (qi % GRID_W)[:, None]
        kr, kc = (band_row + kj // GRID_W)[None], (kj % GRID_W)[None]
        rs = np.clip(qr - NA_KH // 2, 0, rows - NA_KH)
        ws = np.clip(qc - NA_KW // 2, 0, GRID_W - NA_KW)
        ok = (kr >= rs) & (kr < rs + NA_KH) & (kc >= ws) & (kc < ws + NA_KW)
        pats.append(np.where(ok, (kr - qr + NA_KH - 1) * n_dc + (kc - qc + NA_KW - 1), masked))
    return np.concatenate(pats, axis=0).astype(np.int32), masked


def na_bias(na_rpb, rows, after):
    nl, nh, ndr, ndc = na_rpb.shape
    band1, qr1, rs1 = _na_pattern_rows(1, rows)
    for g in range(2, rows // NA_Q_ROWS - 1):
        band, qr, rs = _na_pattern_rows(g, rows)
        assert (qr - band == qr1 - band1).all() and (rs - band == rs1 - band1).all()
    index, masked = _na_bias_index(rows)
    lanes = V7X_SC_LANES
    tab = -(-(masked + 1) // lanes) * lanes
    workers = V7X_SC_CORES * V7X_SC_SUBCORES
    per_head = workers // (nl * nh)
    rows_w = index.shape[0] // per_head
    chunk = NA_TQ // 8
    assert workers % (nl * nh) == 0 and index.shape[0] % per_head == 0 and rows_w % chunk == 0

    def body(tab_hbm, idx_hbm, after_hbm, out_hbm, tab_v, idx_v, out_v):
        del after_hbm
        worker = lax.axis_index("s") * V7X_SC_CORES + lax.axis_index("c")
        lh = worker // per_head
        part = worker - lh * per_head
        layer = lh // nh
        head = lh - layer * nh
        pltpu.sync_copy(tab_hbm.at[lh], tab_v)
        lane = lax.iota(jnp.int32, lanes)
        for t in range(tab // lanes):
            seg = pl.ds(t * lanes, lanes)
            tab_v[seg] = jnp.where(lane + t * lanes == masked, NEG_INF, tab_v[seg] * LOG2_E)

        @pl.loop(0, rows_w // chunk)
        def _(c):
            r0 = part * rows_w + c * chunk
            pat = r0 // NA_TQ
            pltpu.sync_copy(idx_hbm.at[pl.ds(r0, chunk)], idx_v)

            @pl.loop(0, chunk)
            def _(r):
                for v in range(NA_TK // lanes):
                    cols = pl.ds(v * lanes, lanes)
                    out_v[r, cols] = plsc.load_gather(tab_v, [idx_v[r, cols]])

            dst = ((layer * 3 + pat) * nh + head) * NA_TQ + (r0 - pat * NA_TQ)
            pltpu.sync_copy(out_v, out_hbm.at[pl.ds(dst, chunk)])

    table = jnp.pad(na_rpb.reshape(nl * nh, ndr * ndc), ((0, 0), (0, tab - ndr * ndc)))
    out = pl.kernel(
        body,
        mesh=plsc.VectorSubcoreMesh(core_axis_name="c", subcore_axis_name="s"),
        out_type=jax.ShapeDtypeStruct((nl * 3 * nh * NA_TQ, NA_TK), F32),
        scratch_types=[pltpu.VMEM((tab,), F32), pltpu.VMEM((chunk, NA_TK), jnp.int32),
                       pltpu.VMEM((chunk, NA_TK), F32)],
        compiler_params=pltpu.CompilerParams(needs_layout_passes=False),
        name="na_bias",
    )(table, jnp.asarray(index), after)
    return out.reshape(nl, 3, nh, NA_TQ, NA_TK)


def _na_kernel(q_ref, k_ref, v_ref, kc_ref, vc_ref, bias_ref, o_ref):
    g = pl.program_id(2)
    n_groups = k_ref.shape[1] // NA_TQ
    band_row = jnp.clip(g * NA_Q_ROWS - NA_KH // 2, 0, n_groups * NA_Q_ROWS - NA_BAND_ROWS)
    start = pl.multiple_of(band_row * GRID_W, NA_TQ)
    pat = jnp.where(g == 0, 0, jnp.where(g == n_groups - 1, 2, 1))
    scale = HEAD_DIM ** -0.5 * LOG2_E

    def scores(hh):
        cols = slice(hh * HEAD_DIM, (hh + 1) * HEAD_DIM)
        q = q_ref[0, :, cols]
        return _nt_dot(q, k_ref[0, pl.ds(start, NA_TK), cols]), _nt_dot(q, kc_ref[0, :, cols])

    def finish(hh, qk_loc, qk_ctx):
        cols = slice(hh * HEAD_DIM, (hh + 1) * HEAD_DIM)
        s_loc = qk_loc * scale + bias_ref[0, pat, hh]
        s_ctx = qk_ctx * scale
        m = jnp.maximum(jnp.max(s_loc, axis=1, keepdims=True), jnp.max(s_ctx, axis=1, keepdims=True))
        e_loc = jnp.exp2(s_loc - m)
        e_ctx = jnp.exp2(s_ctx - m)
        inv = 1.0 / (jnp.sum(e_loc, axis=1, keepdims=True) + jnp.sum(e_ctx, axis=1, keepdims=True))
        o = jnp.dot(e_loc.astype(BF16), v_ref[0, pl.ds(start, NA_TK), cols], preferred_element_type=F32)
        o = o + jnp.dot(e_ctx.astype(BF16), vc_ref[0, :, cols], preferred_element_type=F32)
        o_ref[0, :, cols] = (o * inv).astype(o_ref.dtype)

    qk = {}
    for hh in range(NA_HEADS_PER_STEP + NA_SCORE_LEAD):
        if hh < NA_HEADS_PER_STEP:
            qk[hh] = scores(hh)
        if hh >= NA_SCORE_LEAD:
            finish(hh - NA_SCORE_LEAD, *qk.pop(hh - NA_SCORE_LEAD))


def na_attention(qkv, kvc, kc_col, bias, layer, b, n):
    lc = kvc.shape[0] // b
    qkv3 = qkv.reshape(b, n, QKV_COLS)
    kvc3 = kvc.reshape(b, lc, kvc.shape[1])
    hbs = D_BRANCH // NA_HB
    kcb = kc_col // NA_HB
    out = pl.pallas_call(
        _na_kernel,
        grid=(hbs, b, n // NA_TQ),
        in_specs=[
            pl.BlockSpec((1, NA_TQ, NA_HB), lambda h, i, g: (i, g, h)),
            pl.BlockSpec((1, n, NA_HB), lambda h, i, g: (i, 0, hbs + h)),
            pl.BlockSpec((1, n, NA_HB), lambda h, i, g: (i, 0, 2 * hbs + h)),
            pl.BlockSpec((1, lc, NA_HB), lambda h, i, g: (i, 0, kcb + h)),
            pl.BlockSpec((1, lc, NA_HB), lambda h, i, g: (i, 0, kcb + hbs + h)),
            pl.BlockSpec((1, 3, NA_HEADS_PER_STEP, NA_TQ, NA_TK), lambda h, i, g: (layer, 0, h, 0, 0),
                         pipeline_mode=pl.Buffered(1)),
        ],
        out_specs=pl.BlockSpec((1, NA_TQ, NA_HB), lambda h, i, g: (i, g, h)),
        out_shape=jax.ShapeDtypeStruct((b, n, D_BRANCH), BF16),
        compiler_params=_params(3),
        name="na_attention",
    )(qkv3, qkv3, qkv3, kvc3, kvc3, bias)
    return out.reshape(b * n, D_BRANCH)


def _ctx_attn_kernel(q_ref, k_ref, v_ref, o_ref):
    for h in range(N_HEADS):
        cols = slice(h * HEAD_DIM, (h + 1) * HEAD_DIM)
        s = _nt_dot(q_ref[0, :, cols], k_ref[0, :, cols]) * (HEAD_DIM ** -0.5)
        m = jnp.max(s, axis=1, keepdims=True)
        e = jnp.exp(s - m)
        p = e * (1.0 / jnp.sum(e, axis=1, keepdims=True))
        o_ref[0, :, cols] = jnp.dot(p.astype(BF16), v_ref[0, :, cols],
                                    preferred_element_type=F32).astype(o_ref.dtype)


def ctx_attention(qkv, b, lc):
    qkv3 = qkv.reshape(b, lc, QKV_COLS)
    part = lambda k: pl.BlockSpec((1, lc, D_BRANCH), lambda i: (i, 0, k))
    out = pl.pallas_call(
        _ctx_attn_kernel,
        grid=(b,),
        in_specs=[part(0), part(1), part(2)],
        out_specs=part(0),
        out_shape=jax.ShapeDtypeStruct((b, lc, D_BRANCH), BF16),
        compiler_params=_params(1),
        name="ctx_attention",
    )(qkv3, qkv3, qkv3)
    return out.reshape(b * lc, D_BRANCH)


def _conv_kernel(h_ref, wx_ref, wb_ref, wc_ref, w_ref, o_ref, wbf_ref, *, seq):
    @pl.when(pl.program_id(1) == 0)
    def _():
        for k, ref in enumerate((wx_ref, wb_ref, wc_ref)):
            wbf_ref[k] = ref[...].astype(BF16)

    h = h_ref[0]
    xc, bg, cg = (jnp.dot(h, wbf_ref[k], preferred_element_type=F32) for k in range(3))
    z = cg * xc
    rows = z.shape[0]
    pos = lax.broadcasted_iota(jnp.int32, z.shape, 0) % seq
    z_prev = jnp.where(pos == 0, 0.0, pltpu.roll(z, 1, 0))
    z_next = jnp.where(pos == seq - 1, 0.0, pltpu.roll(z, rows - 1, 0))
    y = z_prev * w_ref[0, 0:1, :] + z * w_ref[0, 1:2, :] + z_next * w_ref[0, 2:3, :]
    o_ref[0] = (bg * y).astype(o_ref.dtype)


def short_conv(h, w_in, conv_w, layer, n, tc=256, rows=2048):
    m, d = h.shape
    rows = min(max(rows, n), m)
    assert rows % n == 0 and m % rows == 0
    wcol = lambda col: pl.BlockSpec((None, d, tc), lambda j, i: (layer, 0, col // tc + j))
    out = pl.pallas_call(
        functools.partial(_conv_kernel, seq=n),
        grid=(D_BRANCH // tc, m // rows),
        in_specs=[pl.BlockSpec((1, rows, d), lambda j, i: (i, 0, 0)),
                  wcol(COL_XC), wcol(COL_BG), wcol(COL_CG),
                  pl.BlockSpec((1, 3, tc), lambda j, i: (layer, 0, j))],
        out_specs=pl.BlockSpec((1, rows, tc), lambda j, i: (i, 0, j)),
        out_shape=jax.ShapeDtypeStruct((m // rows, rows, D_BRANCH), BF16),
        scratch_shapes=[pltpu.VMEM((3, d, tc), BF16)],
        compiler_params=_params(2),
        name="short_conv",
    )(h.reshape(m // rows, rows, d), w_in, w_in, w_in, conv_w)
    return out.reshape(m, D_BRANCH)


def _gmlp_kernel(u_ref, v_ref, g_ref, ws_ref, bs_ref, o_ref):
    v = jax.nn.gelu(v_ref[...])
    mu = jnp.mean(v, axis=-1, keepdims=True)
    var = jnp.mean(jnp.square(v - mu), axis=-1, keepdims=True)
    vn = ((v - mu) * lax.rsqrt(var + NORM_EPS) * g_ref[...]).astype(BF16)
    for grp in range(GMLP_GROUPS):
        cols = slice(grp * 128, (grp + 1) * 128)
        ws = ws_ref[0, grp].astype(BF16)
        for ch in range(v.shape[0] // CHUNK):
            rows = slice(ch * CHUNK, (ch + 1) * CHUNK)
            s = jnp.dot(ws, vn[rows, cols], preferred_element_type=F32) + bs_ref[:, grp:grp + 1]
            o_ref[rows, cols] = (jax.nn.gelu(u_ref[rows, cols]) * s).astype(o_ref.dtype)


def chunk_gmlp(p, ln_g, w_s, b_s, layer, tm=8 * CHUNK):
    m = p.shape[0]
    return pl.pallas_call(
        _gmlp_kernel,
        grid=(m // tm,),
        in_specs=[
            pl.BlockSpec((tm, D_BRANCH), lambda i: (i, COL_U // D_BRANCH)),
            pl.BlockSpec((tm, D_BRANCH), lambda i: (i, COL_VS // D_BRANCH)),
            pl.BlockSpec((1, D_BRANCH), lambda i: (0, 0)),
            pl.BlockSpec((1, GMLP_GROUPS, CHUNK, CHUNK), lambda i: (layer, 0, 0, 0)),
            pl.BlockSpec((CHUNK, GMLP_GROUPS), lambda i: (0, 0)),
        ],
        out_specs=pl.BlockSpec((tm, D_BRANCH), lambda i: (i, 0)),
        out_shape=jax.ShapeDtypeStruct((m, D_BRANCH), BF16),
        compiler_params=_params(1),
        name="chunk_gmlp",
    )(p, p, ln_g[layer].reshape(1, D_BRANCH), w_s, b_s[layer].T)


def _merge_kernel(a_ref, c_ref, m_ref, g0_ref, g1_ref, g2_ref, w_ref, *rest):
    o_ref, wbf_ref = rest[-2:]

    @pl.when(pl.program_id(1) == 0)
    def _():
        wbf_ref[...] = w_ref[...].astype(BF16)

    acc = None
    for i, (br, gl) in enumerate(((a_ref, g0_ref), (c_ref, g1_ref), (m_ref, g2_ref))):
        proj = jnp.dot(br[...], wbf_ref[i], preferred_element_type=F32)
        term = jax.nn.sigmoid(gl[...]) * proj
        acc = term if acc is None else acc + term
    o_ref[...] = acc.astype(o_ref.dtype)


def merge_branches(attn, conv, gm, p, w_branch, layer, after=(), tm=512, tn=1024):
    m = attn.shape[0]
    br = pl.BlockSpec((tm, D_BRANCH), lambda j, i: (i, 0))
    gate = lambda k: pl.BlockSpec((tm, tn), lambda j, i: (i, (COL_GATE + k * D_MODEL) // tn + j))
    return pl.pallas_call(
        _merge_kernel,
        grid=(D_MODEL // tn, m // tm),
        in_specs=[br, br, br, gate(0), gate(1), gate(2),
                  pl.BlockSpec((None, N_BRANCHES, D_BRANCH, tn), lambda j, i: (layer, 0, 0, j))]
                 + [pl.BlockSpec(memory_space=pl.ANY)] * len(after),
        out_specs=pl.BlockSpec((tm, tn), lambda j, i: (i, j)),
        out_shape=jax.ShapeDtypeStruct((m, D_MODEL), BF16),
        scratch_shapes=[pltpu.VMEM((N_BRANCHES, D_BRANCH, tn), BF16)],
        compiler_params=_params(2),
        name="merge_branches",
    )(attn, conv, gm, p, p, p, w_branch, *after)


def _out_proj_router_kernel(h_ref, w_ref, x_ref, mod_ref, g_ref, sh_ref, sc_ref, wr_ref, br_ref,
                            o_ref, hw_ref, aff_ref, wbf_ref):
    @pl.when(pl.program_id(0) == 0)
    def _():
        wbf_ref[...] = w_ref[...].astype(BF16)

    x = x_ref[...] + mod_ref[0] * jnp.dot(h_ref[...], wbf_ref[...], preferred_element_type=F32)
    o_ref[...] = x
    y = _rms(x, g_ref[...])
    h = (y * (1 + sc_ref[0]) + sh_ref[0]).astype(BF16)
    bits = pltpu.bitcast(h.astype(F32), jnp.uint32)
    half = bits.shape[1] // 2
    hw_ref[...] = lax.shift_right_logical(bits[:, :half], jnp.uint32(16)) | bits[:, half:]
    logits = _nt_dot(wr_ref[...].astype(BF16), h) + br_ref[...]
    mx = jnp.max(logits, axis=0, keepdims=True)
    e = jnp.exp(logits - mx)
    aff_ref[0] = e / jnp.sum(e, axis=0, keepdims=True)


def out_proj_router(h, w_out, layer, x, mod, g, w_router, b_router, n, tm=512):
    m, d = x.shape
    b = m // n
    tm = min(tm, n)
    per_b = n // tm
    vec = lambda k: (pl.BlockSpec((1, 1, d), lambda i: (i // per_b, 0, 0)), mod[:, k].reshape(b, 1, d))
    (gate_s, gate_v), (sh_s, sh_v), (sc_s, sc_v) = vec(2), vec(3), vec(4)
    return pl.pallas_call(
        _out_proj_router_kernel,
        grid=(m // tm,),
        in_specs=[
            pl.BlockSpec((tm, d), lambda i: (i, 0)),
            pl.BlockSpec((None, d, d), lambda i: (layer, 0, 0), pipeline_mode=pl.Buffered(1)),
            pl.BlockSpec((tm, d), lambda i: (i, 0)),
            gate_s,
            pl.BlockSpec((1, d), lambda i: (0, 0)),
            sh_s, sc_s,
            pl.BlockSpec((N_EXPERTS, d), lambda i: (0, 0)),
            pl.BlockSpec((N_EXPERTS, 1), lambda i: (0, 0)),
        ],
        out_specs=[
            pl.BlockSpec((tm, d), lambda i: (i, 0)),
            pl.BlockSpec((tm, d // 2), lambda i: (i, 0)),
            pl.BlockSpec((1, N_EXPERTS, tm), lambda i: (i // per_b, 0, i % per_b)),
        ],
        out_shape=[jax.ShapeDtypeStruct((m, d), F32),
                   jax.ShapeDtypeStruct((m, d // 2), jnp.uint32),
                   jax.ShapeDtypeStruct((b, N_EXPERTS, n), F32)],
        scratch_shapes=[pltpu.VMEM((d, d), BF16)],
        compiler_params=_params(1),
        name="out_proj_router",
    )(h, w_out, x, gate_v, g.reshape(1, d), sh_v, sc_v, w_router.T, b_router.reshape(N_EXPERTS, 1))


def _select_kernel(aff_ref, tri_ref, slot_ref, *, cap):
    bits = pltpu.bitcast(aff_ref[...], jnp.int32)

    def step(i, t):
        cand = t | lax.shift_left(jnp.int32(1), 30 - i)
        cnt = jnp.sum(jnp.where(bits >= cand, 1.0, 0.0), axis=1, keepdims=True)
        return jnp.where(cnt >= cap, cand, t)

    t = lax.fori_loop(0, 31, step, jnp.zeros((bits.shape[0], 1), jnp.int32))
    gt = bits > t
    eq = bits == t
    need = cap - jnp.sum(jnp.where(gt, 1.0, 0.0), axis=1, keepdims=True)
    tri = tri_ref[...]
    eq_rank = jnp.dot(jnp.where(eq, 1.0, 0.0).astype(BF16), tri, preferred_element_type=F32)
    sel = gt | (eq & (eq_rank < need))
    slot = jnp.dot(jnp.where(sel, 1.0, 0.0).astype(BF16), tri, preferred_element_type=F32)
    slot_ref[...] = jnp.where(sel, slot.astype(jnp.int32), -1)


def select_tokens(aff, cap):
    b, e, n = aff.shape
    tri = jnp.triu(jnp.ones((n, n), BF16), k=1)
    rows = pl.BlockSpec((b * e, n), lambda i: (0, 0))
    return pl.pallas_call(
        functools.partial(_select_kernel, cap=cap),
        grid=(1,),
        in_specs=[rows, pl.BlockSpec((n, n), lambda i: (0, 0))],
        out_specs=rows,
        out_shape=jax.ShapeDtypeStruct((b * e, n), jnp.int32),
        compiler_params=_params(1),
        name="select_tokens",
    )(aff.reshape(b * e, n), tri).reshape(b, e, n)


def _onehot(slot, cap):
    hit = lax.broadcasted_iota(jnp.int32, (cap, slot.shape[1]), 0) == slot
    return hit, jnp.where(hit, 1.0, 0.0)


def gather_tokens(h, slot, aff, cap):
    b, n, w = h.shape
    e = N_EXPERTS
    workers = V7X_SC_CORES * V7X_SC_SUBCORES
    rows_per_worker = (b * e) // workers
    word_bytes = jnp.dtype(h.dtype).itemsize
    chunk = min(cap, V7X_SC_SUBCORE_VMEM_BYTES // 2 // (w * word_bytes))
    assert (b * e) % workers == 0 and cap % chunk == 0 and n % V7X_SC_LANES == 0 and word_bytes == 4

    def body(h_hbm, slot_hbm, aff_hbm, xs_hbm, gates_hbm, slot_v, aff_v, idx_v, gate_v, rows_v, sem):
        worker = lax.axis_index("s") * V7X_SC_CORES + lax.axis_index("c")
        lane = lax.iota(jnp.int32, V7X_SC_LANES)

        @pl.loop(0, rows_per_worker)
        def _(j):
            r = worker * rows_per_worker + j
            bi = r // e
            ei = r - bi * e
            pltpu.sync_copy(slot_hbm.at[r], slot_v)
            pltpu.sync_copy(aff_hbm.at[r], aff_v)

            @pl.loop(0, n // V7X_SC_LANES)
            def _(c):
                tokens = pl.ds(c * V7X_SC_LANES, V7X_SC_LANES)
                s = slot_v[tokens]
                chosen = s >= 0
                plsc.store_scatter(idx_v, [s], lane + (c * V7X_SC_LANES + bi * n), mask=chosen)
                plsc.store_scatter(gate_v, [s], aff_v[tokens], mask=chosen)

            out_row = (ei * b + bi) * cap

            @pl.loop(0, cap // chunk)
            def _(q):
                pltpu.async_copy(h_hbm.at[idx_v.at[pl.ds(q * chunk, chunk)]], rows_v, sem).wait()
                pltpu.sync_copy(rows_v, xs_hbm.at[pl.ds(out_row + q * chunk, chunk)])

            pltpu.sync_copy(gate_v, gates_hbm.at[pl.ds(out_row, cap)])

    xs, gates = pl.kernel(
        body,
        mesh=plsc.VectorSubcoreMesh(core_axis_name="c", subcore_axis_name="s"),
        out_type=[jax.ShapeDtypeStruct((e * b * cap, w), h.dtype),
                  jax.ShapeDtypeStruct((e * b * cap,), F32)],
        scratch_types=[
            pltpu.VMEM((n,), jnp.int32), pltpu.VMEM((n,), F32),
            pltpu.VMEM((cap,), jnp.int32), pltpu.VMEM((cap,), F32),
            pltpu.VMEM((chunk, w), h.dtype),
            pltpu.SemaphoreType.DMA,
        ],
        compiler_params=pltpu.CompilerParams(needs_layout_passes=False),
        name="gather_tokens",
    )(h.reshape(b * n, w), slot.reshape(b * e, n), aff.reshape(b * e, n))
    return xs.reshape(e, b * cap, w), gates.reshape(e, b * cap, 1)


def _ffn_kernel(*refs, n_sets, n_f):
    xs_refs, gate_refs = refs[0:2 * n_sets:2], refs[1:2 * n_sets:2]
    wg_ref, wu_ref, wd_ref = refs[2 * n_sets:2 * n_sets + 3]
    y_refs = refs[2 * n_sets + 3:3 * n_sets + 3]
    act_refs = refs[3 * n_sets + 3:4 * n_sets + 3]
    row_refs = refs[4 * n_sets + 3:]
    s = pl.program_id(1)
    tf = wg_ref.shape[1]

    @pl.when(s == 0)
    def _():
        for xs_ref, row_ref in zip(xs_refs, row_refs):
            words = xs_ref[0]
            half = words.shape[1]
            row_ref[:, :half] = pltpu.bitcast(lax.shift_left(words, jnp.uint32(16)), F32).astype(BF16)
            row_ref[:, half:] = pltpu.bitcast(words & jnp.uint32(0xFFFF0000), F32).astype(BF16)

    @pl.when(s < n_f)
    def _():
        wg = wg_ref[...].astype(BF16)
        wu = wu_ref[...].astype(BF16)
        for row_ref, act_ref in zip(row_refs, act_refs):
            xs = row_ref[...]
            hg = jnp.dot(xs, wg, preferred_element_type=F32)
            hu = jnp.dot(xs, wu, preferred_element_type=F32)
            act_ref[s] = (jax.nn.silu(hg) * hu).astype(BF16)

    @pl.when(s >= n_f)
    def _():
        wd = wd_ref[...].astype(BF16)
        for gate_ref, y_ref, act_ref in zip(gate_refs, y_refs, act_refs):
            y = None
            for f in range(n_f):
                part = jnp.dot(act_ref[f], wd[f * tf:(f + 1) * tf, :], preferred_element_type=F32)
                y = part if y is None else y + part
            y_ref[0] = (y * gate_ref[0]).astype(y_ref.dtype)


def expert_ffn(sets, w_gate, w_up, w_down, layer, tf=512, tn=512):
    d, ff = w_gate.shape[2], w_gate.shape[3]
    n_f, n_d = ff // tf, d // tn
    d_idx = lambda s: jnp.maximum(s - n_f, 0)

    def up_tile(i, s):
        ahead = s >= n_f
        return (layer, jnp.where(ahead, jnp.minimum(i + 1, N_EXPERTS - 1), i), 0,
                jnp.where(ahead, 0, jnp.minimum(s, n_f - 1)))

    def down_tile(i, s):
        behind = s < n_f
        return (layer, jnp.where(behind, jnp.maximum(i - 1, 0), i), 0, jnp.where(behind, n_d - 1, s - n_f))
    in_specs, operands, out_specs, out_shape, act_scratch, row_scratch = [], [], [], [], [], []
    for xs, gates in sets:
        r = xs.shape[1]
        in_specs += [pl.BlockSpec((1, r, d // 2), lambda i, s: (i, 0, 0)),
                     pl.BlockSpec((1, r, 1), lambda i, s: (i, 0, 0))]
        operands += [xs, gates]
        out_specs.append(pl.BlockSpec((1, r, tn), lambda i, s: (i, 0, d_idx(s))))
        out_shape.append(jax.ShapeDtypeStruct((N_EXPERTS, r, d), BF16))
        act_scratch.append(pltpu.VMEM((n_f, r, tf), BF16))
        row_scratch.append(pltpu.VMEM((r, d), BF16))
    in_specs += [
        pl.BlockSpec((None, None, d, tf), up_tile),
        pl.BlockSpec((None, None, d, tf), up_tile),
        pl.BlockSpec((None, None, ff, tn), down_tile),
    ]
    return pl.pallas_call(
        functools.partial(_ffn_kernel, n_sets=len(sets), n_f=n_f),
        grid=(N_EXPERTS, n_f + n_d),
        in_specs=in_specs,
        out_specs=out_specs,
        out_shape=out_shape,
        scratch_shapes=act_scratch + row_scratch,
        compiler_params=_params(2),
        name="expert_ffn",
    )(*operands, w_gate, w_up, w_down)


def _combine_kernel(y_ref, slot_ref, x_ref, mod_ref, o_ref, onehot_ref, *, cap):
    @pl.when(pl.program_id(1) == 0)
    def _():
        for e in range(N_EXPERTS):
            onehot_ref[e * cap:(e + 1) * cap, :] = _onehot(slot_ref[e], cap)[1].astype(BF16)

    y = y_ref[...].reshape(N_EXPERTS * cap, y_ref.shape[2])
    moe = lax.dot_general(onehot_ref[...], y, (((0,), (0,)), ((), ())), preferred_element_type=F32)
    o_ref[0] = x_ref[0] + mod_ref[0] * moe


def combine(y, slot, x, gate_vec, cap, tn=512):
    b, n, d = x.shape
    e = N_EXPERTS
    return pl.pallas_call(
        functools.partial(_combine_kernel, cap=cap),
        grid=(b, d // tn),
        in_specs=[
            pl.BlockSpec((e, cap, tn), lambda i, j: (0, i, j)),
            pl.BlockSpec((e, 1, n), lambda i, j: (i, 0, 0)),
            pl.BlockSpec((1, n, tn), lambda i, j: (i, 0, j)),
            pl.BlockSpec((1, 1, tn), lambda i, j: (i, 0, j)),
        ],
        out_specs=pl.BlockSpec((1, n, tn), lambda i, j: (i, 0, j)),
        out_shape=jax.ShapeDtypeStruct((b, n, d), F32),
        scratch_shapes=[pltpu.VMEM((e * cap, n), BF16)],
        compiler_params=_params(2),
        name="combine",
    )(y, slot.reshape(b * e, 1, n), x, gate_vec.reshape(b, 1, d))


def moe_route(hw, aff):
    n = hw.shape[1]
    cap = CAPACITY_FACTOR * n // N_EXPERTS
    slot = select_tokens(aff, cap)
    return slot, cap, gather_tokens(hw, slot, aff, cap)


def mixer_branches(h, n, layer, w_in, conv_w, ln_g, w_s, b_s):
    p = matmul(h, w_in, layer, P_COL0, w_in.shape[2] - P_COL0)
    return p, short_conv(h, w_in, conv_w, layer, n), chunk_gmlp(p, ln_g, w_s, b_s, layer)


def mixer_merge(x, attn, branches, mod, n, layer, w_branch, w_out, g2, w_router, b_router, after=()):
    b = x.shape[0] // n
    p, conv, gm = branches
    merged = merge_branches(attn, conv, gm, p, w_branch, layer, after=after)
    x_new, hw, aff = out_proj_router(merged, w_out, layer, x, mod, g2, w_router, b_router, n)
    return x_new.reshape(b, n, -1), hw.reshape(b, n, -1), aff


def kernel(x, c, ctx, c_ctx, w_mod, b_mod, norm1_g, w_in, na_rpb, conv_w, gmlp_ln_g, w_spatial, b_spatial,
           w_branch, w_out, norm2_g, w_router, b_router, w_e_gate, w_e_up, w_e_down, final_g):
    b, n, d = x.shape
    lc = ctx.shape[1]
    cvec = jnp.concatenate([c, c_ctx[None], jnp.zeros((8 - b - 1, d), F32)], axis=0)
    modv = modvec(cvec, w_mod, b_mod).reshape(DEPTH, 8, 6, d)
    cos_t, sin_t = rope_tables(n)
    branch_w = (w_in, conv_w, gmlp_ln_g, w_spatial, b_spatial)
    ctx_s = ctx
    for layer in range(DEPTH):
        last = layer == DEPTH - 1
        mod = modv[layer, :b]
        mod_c = jnp.broadcast_to(modv[layer, b][None], (b, 6, d))
        merge_w = (w_branch, w_out, norm2_g[layer], w_router[layer], b_router[layer])

        hc = norm_mod(ctx_s, norm1_g[layer], mod_c[:, 0], mod_c[:, 1]).reshape(b * lc, d)
        if last:
            kvc = qkv_proj(hc, w_in, layer, lc, cos_t, sin_t, rope=False, first_tile=1, n_tiles=2)
            kc_col, early = 0, ()
        else:
            kvc = qkv_proj(hc, w_in, layer, lc, cos_t, sin_t, rope=False)
            kc_col = D_BRANCH
            attn_c = ctx_attention(kvc, b, lc)
            branches_c = mixer_branches(hc, lc, layer, *branch_w)
            early = (attn_c,) + branches_c[1:]

        h = norm_mod(x, norm1_g[layer], mod[:, 0], mod[:, 1]).reshape(b * n, d)
        qkv = qkv_proj(h, w_in, layer, n, cos_t, sin_t, rope=True, after=early)
        if layer == 0:
            bias = na_bias(na_rpb, n // GRID_W, after=qkv)
        attn = na_attention(qkv, kvc, kc_col, bias, layer, b, n)
        x, hw, aff = mixer_merge(x.reshape(b * n, d), attn, mixer_branches(h, n, layer, *branch_w), mod, n, layer,
                                 *merge_w)

        slot, cap, rows = moe_route(hw, aff)
        if last:
            (y,) = expert_ffn([rows], w_e_gate, w_e_up, w_e_down, layer)
        else:
            ctx_mid, hw_c, aff_c = mixer_merge(ctx_s.reshape(b * lc, d), attn_c, branches_c, mod_c, lc, layer,
                                               *merge_w, after=(slot,))
            slot_c, cap_c, rows_c = moe_route(hw_c, aff_c)
            y, y_c = expert_ffn([rows, rows_c], w_e_gate, w_e_up, w_e_down, layer)
            ctx_s = combine(y_c, slot_c, ctx_mid, mod_c[:, 5], cap_c)
        x = combine(y, slot, x, mod[:, 5], cap)
    return final_norm(x, final_g)
```

```python
import functools

import numpy as np
import jax
import jax.numpy as jnp
from jax import lax
from jax.experimental import pallas as pl
from jax.experimental.pallas import tpu as pltpu
from jax.experimental.pallas import tpu_sc as plsc

D_MODEL = 2048
DEPTH = 2
GRID_W = 64
D_BRANCH = D_MODEL // 2
N_BRANCHES = 3
HEAD_DIM = 128
N_HEADS = D_BRANCH // HEAD_DIM
NA_KH = 8
NA_KW = 16
ROPE_THETA = 10000.0
ROPE_AXIS_DIM = HEAD_DIM // 2
CHUNK = 128
GMLP_GROUPS = D_BRANCH // 128
N_EXPERTS = 16
CAPACITY_FACTOR = 2
NORM_EPS = 1e-6
NEG_INF = -1e30
LOG2_E = float(np.log2(np.e))

QKV_COLS = 3 * D_BRANCH
COL_XC, COL_BG, COL_CG = (QKV_COLS + i * D_BRANCH for i in range(3))
P_COL0 = QKV_COLS + 3 * D_BRANCH
COL_U, COL_VS, COL_GATE = 0, D_BRANCH, 2 * D_BRANCH

NA_Q_ROWS = 4
NA_BAND_ROWS = 12
NA_TQ = NA_Q_ROWS * GRID_W
NA_TK = NA_BAND_ROWS * GRID_W
NA_HEADS_PER_STEP = N_HEADS
NA_HB = NA_HEADS_PER_STEP * HEAD_DIM
NA_SCORE_LEAD = 3

V7X_VMEM_LIMIT = 56 * 1024 * 1024
V7X_SC_CORES, V7X_SC_SUBCORES, V7X_SC_LANES = 2, 16, 16
V7X_SC_SUBCORE_VMEM_BYTES = 512 * 1024
GATE_LANES = 128

BF16 = jnp.bfloat16
F32 = jnp.float32


def _params(n_axes, vmem=V7X_VMEM_LIMIT):
    return pltpu.CompilerParams(dimension_semantics=("arbitrary",) * n_axes, vmem_limit_bytes=vmem)


def _modvec_kernel(c_ref, w_ref, b_ref, o_ref):
    s = jax.nn.silu(c_ref[...]).astype(BF16)
    o_ref[0] = jnp.dot(s, w_ref[0].astype(BF16), preferred_element_type=F32) + b_ref[0]


def modvec(cvec, w_mod, b_mod, tn=1024):
    nl, d, n6 = w_mod.shape
    return pl.pallas_call(
        _modvec_kernel,
        grid=(nl, n6 // tn),
        in_specs=[
            pl.BlockSpec((8, d), lambda l, j: (0, 0)),
            pl.BlockSpec((1, d, tn), lambda l, j: (l, 0, j)),
            pl.BlockSpec((1, 1, tn), lambda l, j: (l, 0, j)),
        ],
        out_specs=pl.BlockSpec((1, 8, tn), lambda l, j: (l, 0, j)),
        out_shape=jax.ShapeDtypeStruct((nl, 8, n6), F32),
        compiler_params=_params(2),
        name="modvec",
    )(cvec, w_mod, b_mod.reshape(nl, 1, n6))


def _rms(x, g):
    ms = jnp.mean(x * x, axis=-1, keepdims=True)
    return x * lax.rsqrt(ms + NORM_EPS) * g


def _norm_mod_kernel(x_ref, g_ref, sh_ref, sc_ref, o_ref):
    y = _rms(x_ref[0], g_ref[...])
    o_ref[0] = (y * (1 + sc_ref[0]) + sh_ref[0]).astype(o_ref.dtype)


def _norm_kernel(x_ref, g_ref, o_ref):
    o_ref[0] = _rms(x_ref[0], g_ref[...]).astype(o_ref.dtype)


def norm_mod(x, g, shift, scale, out_dtype=BF16, tm=1024):
    b, n, d = x.shape
    tm = min(tm, n)
    return pl.pallas_call(
        _norm_mod_kernel,
        grid=(b, n // tm),
        in_specs=[
            pl.BlockSpec((1, tm, d), lambda i, j: (i, j, 0)),
            pl.BlockSpec((1, d), lambda i, j: (0, 0)),
            pl.BlockSpec((1, 1, d), lambda i, j: (i, 0, 0)),
            pl.BlockSpec((1, 1, d), lambda i, j: (i, 0, 0)),
        ],
        out_specs=pl.BlockSpec((1, tm, d), lambda i, j: (i, j, 0)),
        out_shape=jax.ShapeDtypeStruct((b, n, d), out_dtype),
        compiler_params=_params(2),
        name="norm_mod",
    )(x, g.reshape(1, d), shift.reshape(b, 1, d), scale.reshape(b, 1, d))


def final_norm(x, g, tm=1024):
    b, n, d = x.shape
    return pl.pallas_call(
        _norm_kernel,
        grid=(b, n // tm),
        in_specs=[
            pl.BlockSpec((1, tm, d), lambda i, j: (i, j, 0)),
            pl.BlockSpec((1, d), lambda i, j: (0, 0)),
        ],
        out_specs=pl.BlockSpec((1, tm, d), lambda i, j: (i, j, 0)),
        out_shape=jax.ShapeDtypeStruct((b, n, d), F32),
        compiler_params=_params(2),
        name="final_norm",
    )(x, g.reshape(1, d))


def _mm_kernel(x_ref, w_ref, o_ref, wbf_ref):
    @pl.when(pl.program_id(1) == 0)
    def _():
        wbf_ref[...] = w_ref[...].astype(BF16)

    o_ref[...] = jnp.dot(x_ref[...], wbf_ref[...], preferred_element_type=F32).astype(o_ref.dtype)


def matmul(x, w, layer, col_off, n_cols, out_dtype=F32, tm=1024, tn=1024):
    m, k = x.shape
    tm = min(tm, m)
    off = col_off // tn
    return pl.pallas_call(
        _mm_kernel,
        grid=(n_cols // tn, m // tm),
        in_specs=[
            pl.BlockSpec((tm, k), lambda j, i: (i, 0)),
            pl.BlockSpec((None, k, tn), lambda j, i: (layer, 0, off + j)),
        ],
        out_specs=pl.BlockSpec((tm, tn), lambda j, i: (i, j)),
        out_shape=jax.ShapeDtypeStruct((m, n_cols), out_dtype),
        scratch_shapes=[pltpu.VMEM((k, tn), BF16)],
        compiler_params=_params(2),
        name="matmul",
    )(x, w)


def _qkv_kernel(x_ref, w_ref, cos_ref, sin_ref, *rest, rope):
    o_ref, wbf_ref = rest[-2:]

    @pl.when(pl.program_id(1) == 0)
    def _():
        wbf_ref[...] = w_ref[...].astype(BF16)

    def plain():
        o_ref[...] = jnp.dot(x_ref[...], wbf_ref[...], preferred_element_type=F32).astype(BF16)

    if not rope:
        plain()
        return
    c = pl.program_id(0)

    @pl.when(c < 2)
    def _():
        cs = cos_ref[...]
        sn = sin_ref[...]
        lane = lax.broadcasted_iota(jnp.int32, cs.shape, 1)
        first_half = (lane % ROPE_AXIS_DIM) < (ROPE_AXIS_DIM // 2)
        for pair in range(N_HEADS // 2):
            cols = slice(pair * 2 * HEAD_DIM, (pair + 1) * 2 * HEAD_DIM)
            acc = jnp.dot(x_ref[...], wbf_ref[:, cols], preferred_element_type=F32)
            for h in range(2):
                t = acc[:, h * HEAD_DIM:(h + 1) * HEAD_DIM]
                partner = jnp.where(first_half,
                                    pltpu.roll(t, HEAD_DIM - ROPE_AXIS_DIM // 2, 1),
                                    pltpu.roll(t, ROPE_AXIS_DIM // 2, 1))
                lo = (pair * 2 + h) * HEAD_DIM
                o_ref[:, lo:lo + HEAD_DIM] = (t * cs + partner * sn).astype(BF16)

    @pl.when(c == 2)
    def _():
        plain()


def rope_tables(n):
    pos = jnp.arange(n)
    rc = jnp.stack([pos // GRID_W, pos % GRID_W], axis=-1).astype(F32)
    inv = ROPE_THETA ** (-jnp.arange(0, ROPE_AXIS_DIM, 2, dtype=F32) / ROPE_AXIS_DIM)
    ang = rc[:, :, None] * inv
    cos, sin = jnp.cos(ang), jnp.sin(ang)
    cos_t = jnp.concatenate([cos, cos], axis=-1).reshape(n, HEAD_DIM)
    sin_t = jnp.concatenate([-sin, sin], axis=-1).reshape(n, HEAD_DIM)
    return cos_t, sin_t


def qkv_proj(x, w, layer, n, cos_t, sin_t, rope, first_tile=0, n_tiles=3, after=(), tm=1024):
    m, k = x.shape
    tm = min(tm, n if rope else m)
    nt = n // tm if rope else 1
    tab = pl.BlockSpec((tm, HEAD_DIM), lambda c, i: (i % nt, 0))
    return pl.pallas_call(
        functools.partial(_qkv_kernel, rope=rope),
        grid=(n_tiles, m // tm),
        in_specs=[
            pl.BlockSpec((tm, k), lambda c, i: (i, 0)),
            pl.BlockSpec((None, k, D_BRANCH), lambda c, i: (layer, 0, first_tile + c)),
            tab, tab,
        ] + [pl.BlockSpec(memory_space=pl.ANY)] * len(after),
        out_specs=pl.BlockSpec((tm, D_BRANCH), lambda c, i: (i, c)),
        out_shape=jax.ShapeDtypeStruct((m, n_tiles * D_BRANCH), BF16),
        scratch_shapes=[pltpu.VMEM((k, D_BRANCH), BF16)],
        compiler_params=_params(2),
        name="qkv_proj",
    )(x, w, cos_t, sin_t, *after)


def _nt_dot(a, b):
    return lax.dot_general(a, b, (((1,), (1,)), ((), ())), preferred_element_type=F32)


def _na_pattern_rows(g, rows):
    band_row = int(np.clip(g * NA_Q_ROWS - NA_KH // 2, 0, rows - NA_BAND_ROWS))
    qr = g * NA_Q_ROWS + np.arange(NA_Q_ROWS)
    rs = np.clip(qr - NA_KH // 2, 0, rows - NA_KH)
    return band_row, qr, rs


def _na_bias_index(rows):
    n_dc = 2 * NA_KW - 1
    masked = (2 * NA_KH - 1) * n_dc
    pats = []
    for g in (0, 1, rows // NA_Q_ROWS - 1):
        band_row, _, _ = _na_pattern_rows(g, rows)
        qi, kj = np.arange(NA_TQ), np.arange(NA_TK)
        qr, qc = (g * NA_Q_ROWS + qi // GRID_W)[:, None], (qi % GRID_W)[:, None]
        kr, kc = (band_row + kj // GRID_W)[None], (kj % GRID_W)[None]
        rs = np.clip(qr - NA_KH // 2, 0, rows - NA_KH)
        ws = np.clip(qc - NA_KW // 2, 0, GRID_W - NA_KW)
        ok = (kr >= rs) & (kr < rs + NA_KH) & (kc >= ws) & (kc < ws + NA_KW)
        pats.append(np.where(ok, (kr - qr + NA_KH - 1) * n_dc + (kc - qc + NA_KW - 1), masked))
    return np.concatenate(pats, axis=0).astype(np.int32), masked


def na_bias(na_rpb, rows, after):
    nl, nh, ndr, ndc = na_rpb.shape
    band1, qr1, rs1 = _na_pattern_rows(1, rows)
    for g in range(2, rows // NA_Q_ROWS - 1):
        band, qr, rs = _na_pattern_rows(g, rows)
        assert (qr - band == qr1 - band1).all() and (rs - band == rs1 - band1).all()
    index, masked = _na_bias_index(rows)
    lanes = V7X_SC_LANES
    tab = -(-(masked + 1) // lanes) * lanes
    workers = V7X_SC_CORES * V7X_SC_SUBCORES
    per_head = workers // (nl * nh)
    rows_w = index.shape[0] // per_head
    chunk = NA_TQ // 8
    assert workers % (nl * nh) == 0 and index.shape[0] % per_head == 0 and rows_w % chunk == 0

    def body(tab_hbm, idx_hbm, after_hbm, out_hbm, tab_v, idx_v, out_v):
        del after_hbm
        worker = lax.axis_index("s") * V7X_SC_CORES + lax.axis_index("c")
        lh = worker // per_head
        part = worker - lh * per_head
        layer = lh // nh
        head = lh - layer * nh
        pltpu.sync_copy(tab_hbm.at[lh], tab_v)
        lane = lax.iota(jnp.int32, lanes)
        for t in range(tab // lanes):
            seg = pl.ds(t * lanes, lanes)
            tab_v[seg] = jnp.where(lane + t * lanes == masked, NEG_INF, tab_v[seg] * LOG2_E)

        @pl.loop(0, rows_w // chunk)
        def _(c):
            r0 = part * rows_w + c * chunk
            pat = r0 // NA_TQ
            pltpu.sync_copy(idx_hbm.at[pl.ds(r0, chunk)], idx_v)

            @pl.loop(0, chunk)
            def _(r):
                for v in range(NA_TK // lanes):
                    cols = pl.ds(v * lanes, lanes)
                    out_v[r, cols] = plsc.load_gather(tab_v, [idx_v[r, cols]])

            dst = ((layer * 3 + pat) * nh + head) * NA_TQ + (r0 - pat * NA_TQ)
            pltpu.sync_copy(out_v, out_hbm.at[pl.ds(dst, chunk)])

    table = jnp.pad(na_rpb.reshape(nl * nh, ndr * ndc), ((0, 0), (0, tab - ndr * ndc)))
    out = pl.kernel(
        body,
        mesh=plsc.VectorSubcoreMesh(core_axis_name="c", subcore_axis_name="s"),
        out_type=jax.ShapeDtypeStruct((nl * 3 * nh * NA_TQ, NA_TK), F32),
        scratch_types=[pltpu.VMEM((tab,), F32), pltpu.VMEM((chunk, NA_TK), jnp.int32),
                       pltpu.VMEM((chunk, NA_TK), F32)],
        compiler_params=pltpu.CompilerParams(needs_layout_passes=False),
        name="na_bias",
    )(table, jnp.asarray(index), after)
    return out.reshape(nl, 3, nh, NA_TQ, NA_TK)


def _na_kernel(q_ref, k_ref, v_ref, kc_ref, vc_ref, bias_ref, o_ref):
    g = pl.program_id(2)
    n_groups = k_ref.shape[1] // NA_TQ
    band_row = jnp.clip(g * NA_Q_ROWS - NA_KH // 2, 0, n_groups * NA_Q_ROWS - NA_BAND_ROWS)
    start = pl.multiple_of(band_row * GRID_W, NA_TQ)
    pat = jnp.where(g == 0, 0, jnp.where(g == n_groups - 1, 2, 1))
    scale = HEAD_DIM ** -0.5 * LOG2_E

    def scores(hh):
        cols = slice(hh * HEAD_DIM, (hh + 1) * HEAD_DIM)
        q = q_ref[0, :, cols]
        return _nt_dot(q, k_ref[0, pl.ds(start, NA_TK), cols]), _nt_dot(q, kc_ref[0, :, cols])

    def finish(hh, qk_loc, qk_ctx):
        cols = slice(hh * HEAD_DIM, (hh + 1) * HEAD_DIM)
        s_loc = qk_loc * scale + bias_ref[0, pat, hh]
        s_ctx = qk_ctx * scale
        m = jnp.maximum(jnp.max(s_loc, axis=1, keepdims=True), jnp.max(s_ctx, axis=1, keepdims=True))
        e_loc = jnp.exp2(s_loc - m)
        e_ctx = jnp.exp2(s_ctx - m)
        inv = 1.0 / (jnp.sum(e_loc, axis=1, keepdims=True) + jnp.sum(e_ctx, axis=1, keepdims=True))
        o = jnp.dot(e_loc.astype(BF16), v_ref[0, pl.ds(start, NA_TK), cols], preferred_element_type=F32)
        o = o + jnp.dot(e_ctx.astype(BF16), vc_ref[0, :, cols], preferred_element_type=F32)
        o_ref[0, :, cols] = (o * inv).astype(o_ref.dtype)

    qk = {}
    for hh in range(NA_HEADS_PER_STEP + NA_SCORE_LEAD):
        if hh < NA_HEADS_PER_STEP:
            qk[hh] = scores(hh)
        if hh >= NA_SCORE_LEAD:
            finish(hh - NA_SCORE_LEAD, *qk.pop(hh - NA_SCORE_LEAD))


def na_attention(qkv, kvc, kc_col, bias, layer, b, n):
    lc = kvc.shape[0] // b
    qkv3 = qkv.reshape(b, n, QKV_COLS)
    kvc3 = kvc.reshape(b, lc, kvc.shape[1])
    hbs = D_BRANCH // NA_HB
    kcb = kc_col // NA_HB
    out = pl.pallas_call(
        _na_kernel,
        grid=(hbs, b, n // NA_TQ),
        in_specs=[
            pl.BlockSpec((1, NA_TQ, NA_HB), lambda h, i, g: (i, g, h)),
            pl.BlockSpec((1, n, NA_HB), lambda h, i, g: (i, 0, hbs + h)),
            pl.BlockSpec((1, n, NA_HB), lambda h, i, g: (i, 0, 2 * hbs + h)),
            pl.BlockSpec((1, lc, NA_HB), lambda h, i, g: (i, 0, kcb + h)),
            pl.BlockSpec((1, lc, NA_HB), lambda h, i, g: (i, 0, kcb + hbs + h)),
            pl.BlockSpec((1, 3, NA_HEADS_PER_STEP, NA_TQ, NA_TK), lambda h, i, g: (layer, 0, h, 0, 0),
                         pipeline_mode=pl.Buffered(1)),
        ],
        out_specs=pl.BlockSpec((1, NA_TQ, NA_HB), lambda h, i, g: (i, g, h)),
        out_shape=jax.ShapeDtypeStruct((b, n, D_BRANCH), BF16),
        compiler_params=_params(3),
        name="na_attention",
    )(qkv3, qkv3, qkv3, kvc3, kvc3, bias)
    return out.reshape(b * n, D_BRANCH)


def _ctx_attn_kernel(q_ref, k_ref, v_ref, o_ref):
    for h in range(N_HEADS):
        cols = slice(h * HEAD_DIM, (h + 1) * HEAD_DIM)
        s = _nt_dot(q_ref[0, :, cols], k_ref[0, :, cols]) * (HEAD_DIM ** -0.5)
        m = jnp.max(s, axis=1, keepdims=True)
        e = jnp.exp(s - m)
        p = e * (1.0 / jnp.sum(e, axis=1, keepdims=True))
        o_ref[0, :, cols] = jnp.dot(p.astype(BF16), v_ref[0, :, cols],
                                    preferred_element_type=F32).astype(o_ref.dtype)


def ctx_attention(qkv, b, lc):
    qkv3 = qkv.reshape(b, lc, QKV_COLS)
    part = lambda k: pl.BlockSpec((1, lc, D_BRANCH), lambda i: (i, 0, k))
    out = pl.pallas_call(
        _ctx_attn_kernel,
        grid=(b,),
        in_specs=[part(0), part(1), part(2)],
        out_specs=part(0),
        out_shape=jax.ShapeDtypeStruct((b, lc, D_BRANCH), BF16),
        compiler_params=_params(1),
        name="ctx_attention",
    )(qkv3, qkv3, qkv3)
    return out.reshape(b * lc, D_BRANCH)


def _conv_kernel(h_ref, wx_ref, wb_ref, wc_ref, w_ref, o_ref, wbf_ref, *, seq):
    @pl.when(pl.program_id(1) == 0)
    def _():
        for k, ref in enumerate((wx_ref, wb_ref, wc_ref)):
            wbf_ref[k] = ref[...].astype(BF16)

    h = h_ref[0]
    xc, bg, cg = (jnp.dot(h, wbf_ref[k], preferred_element_type=F32) for k in range(3))
    z = cg * xc
    rows = z.shape[0]
    pos = lax.broadcasted_iota(jnp.int32, z.shape, 0) % seq
    z_prev = jnp.where(pos == 0, 0.0, pltpu.roll(z, 1, 0))
    z_next = jnp.where(pos == seq - 1, 0.0, pltpu.roll(z, rows - 1, 0))
    y = z_prev * w_ref[0, 0:1, :] + z * w_ref[0, 1:2, :] + z_next * w_ref[0, 2:3, :]
    o_ref[0] = (bg * y).astype(o_ref.dtype)


def short_conv(h, w_in, conv_w, layer, n, tc=256, rows=2048):
    m, d = h.shape
    rows = min(max(rows, n), m)
    assert rows % n == 0 and m % rows == 0
    wcol = lambda col: pl.BlockSpec((None, d, tc), lambda j, i: (layer, 0, col // tc + j))
    out = pl.pallas_call(
        functools.partial(_conv_kernel, seq=n),
        grid=(D_BRANCH // tc, m // rows),
        in_specs=[pl.BlockSpec((1, rows, d), lambda j, i: (i, 0, 0)),
                  wcol(COL_XC), wcol(COL_BG), wcol(COL_CG),
                  pl.BlockSpec((1, 3, tc), lambda j, i: (layer, 0, j))],
        out_specs=pl.BlockSpec((1, rows, tc), lambda j, i: (i, 0, j)),
        out_shape=jax.ShapeDtypeStruct((m // rows, rows, D_BRANCH), BF16),
        scratch_shapes=[pltpu.VMEM((3, d, tc), BF16)],
        compiler_params=_params(2),
        name="short_conv",
    )(h.reshape(m // rows, rows, d), w_in, w_in, w_in, conv_w)
    return out.reshape(m, D_BRANCH)


def _gmlp_kernel(u_ref, v_ref, g_ref, ws_ref, bs_ref, o_ref):
    v = jax.nn.gelu(v_ref[...])
    mu = jnp.mean(v, axis=-1, keepdims=True)
    var = jnp.mean(jnp.square(v - mu), axis=-1, keepdims=True)
    vn = ((v - mu) * lax.rsqrt(var + NORM_EPS) * g_ref[...]).astype(BF16)
    for grp in range(GMLP_GROUPS):
        cols = slice(grp * 128, (grp + 1) * 128)
        ws = ws_ref[0, grp].astype(BF16)
        for ch in range(v.shape[0] // CHUNK):
            rows = slice(ch * CHUNK, (ch + 1) * CHUNK)
            s = jnp.dot(ws, vn[rows, cols], preferred_element_type=F32) + bs_ref[:, grp:grp + 1]
            o_ref[rows, cols] = (jax.nn.gelu(u_ref[rows, cols]) * s).astype(o_ref.dtype)


def chunk_gmlp(p, ln_g, w_s, b_s, layer, tm=8 * CHUNK):
    m = p.shape[0]
    return pl.pallas_call(
        _gmlp_kernel,
        grid=(m // tm,),
        in_specs=[
            pl.BlockSpec((tm, D_BRANCH), lambda i: (i, COL_U // D_BRANCH)),
            pl.BlockSpec((tm, D_BRANCH), lambda i: (i, COL_VS // D_BRANCH)),
            pl.BlockSpec((1, D_BRANCH), lambda i: (0, 0)),
            pl.BlockSpec((1, GMLP_GROUPS, CHUNK, CHUNK), lambda i: (layer, 0, 0, 0)),
            pl.BlockSpec((CHUNK, GMLP_GROUPS), lambda i: (0, 0)),
        ],
        out_specs=pl.BlockSpec((tm, D_BRANCH), lambda i: (i, 0)),
        out_shape=jax.ShapeDtypeStruct((m, D_BRANCH), BF16),
        compiler_params=_params(1),
        name="chunk_gmlp",
    )(p, p, ln_g[layer].reshape(1, D_BRANCH), w_s, b_s[layer].T)


def _merge_kernel(a_ref, c_ref, m_ref, g0_ref, g1_ref, g2_ref, w_ref, *rest):
    o_ref, wbf_ref = rest[-2:]

    @pl.when(pl.program_id(1) == 0)
    def _():
        wbf_ref[...] = w_ref[...].astype(BF16)

    acc = None
    for i, (br, gl) in enumerate(((a_ref, g0_ref), (c_ref, g1_ref), (m_ref, g2_ref))):
        proj = jnp.dot(br[...], wbf_ref[i], preferred_element_type=F32)
        term = jax.nn.sigmoid(gl[...]) * proj
        acc = term if acc is None else acc + term
    o_ref[...] = acc.astype(o_ref.dtype)


def merge_branches(attn, conv, gm, p, w_branch, layer, after=(), tm=512, tn=1024):
    m = attn.shape[0]
    br = pl.BlockSpec((tm, D_BRANCH), lambda j, i: (i, 0))
    gate = lambda k: pl.BlockSpec((tm, tn), lambda j, i: (i, (COL_GATE + k * D_MODEL) // tn + j))
    return pl.pallas_call(
        _merge_kernel,
        grid=(D_MODEL // tn, m // tm),
        in_specs=[br, br, br, gate(0), gate(1), gate(2),
                  pl.BlockSpec((None, N_BRANCHES, D_BRANCH, tn), lambda j, i: (layer, 0, 0, j))]
                 + [pl.BlockSpec(memory_space=pl.ANY)] * len(after),
        out_specs=pl.BlockSpec((tm, tn), lambda j, i: (i, j)),
        out_shape=jax.ShapeDtypeStruct((m, D_MODEL), BF16),
        scratch_shapes=[pltpu.VMEM((N_BRANCHES, D_BRANCH, tn), BF16)],
        compiler_params=_params(2),
        name="merge_branches",
    )(attn, conv, gm, p, p, p, w_branch, *after)


def _out_proj_router_kernel(h_ref, w_ref, x_ref, mod_ref, g_ref, sh_ref, sc_ref, wr_ref, br_ref,
                            o_ref, hw_ref, aff_ref, wbf_ref):
    @pl.when(pl.program_id(0) == 0)
    def _():
        wbf_ref[...] = w_ref[...].astype(BF16)

    x = x_ref[...] + mod_ref[0] * jnp.dot(h_ref[...], wbf_ref[...], preferred_element_type=F32)
    o_ref[...] = x
    y = _rms(x, g_ref[...])
    h = (y * (1 + sc_ref[0]) + sh_ref[0]).astype(BF16)
    bits = pltpu.bitcast(h.astype(F32), jnp.uint32)
    half = bits.shape[1] // 2
    hw_ref[...] = lax.shift_right_logical(bits[:, :half], jnp.uint32(16)) | bits[:, half:]
    logits = _nt_dot(wr_ref[...].astype(BF16), h) + br_ref[...]
    mx = jnp.max(logits, axis=0, keepdims=True)
    e = jnp.exp(logits - mx)
    aff_ref[0] = e / jnp.sum(e, axis=0, keepdims=True)


def out_proj_router(h, w_out, layer, x, mod, g, w_router, b_router, n, tm=512):
    m, d = x.shape
    b = m // n
    tm = min(tm, n)
    per_b = n // tm
    vec = lambda k: (pl.BlockSpec((1, 1, d), lambda i: (i // per_b, 0, 0)), mod[:, k].reshape(b, 1, d))
    (gate_s, gate_v), (sh_s, sh_v), (sc_s, sc_v) = vec(2), vec(3), vec(4)
    return pl.pallas_call(
        _out_proj_router_kernel,
        grid=(m // tm,),
        in_specs=[
            pl.BlockSpec((tm, d), lambda i: (i, 0)),
            pl.BlockSpec((None, d, d), lambda i: (layer, 0, 0), pipeline_mode=pl.Buffered(1)),
            pl.BlockSpec((tm, d), lambda i: (i, 0)),
            gate_s,
            pl.BlockSpec((1, d), lambda i: (0, 0)),
            sh_s, sc_s,
            pl.BlockSpec((N_EXPERTS, d), lambda i: (0, 0)),
            pl.BlockSpec((N_EXPERTS, 1), lambda i: (0, 0)),
        ],
        out_specs=[
            pl.BlockSpec((tm, d), lambda i: (i, 0)),
            pl.BlockSpec((tm, d // 2), lambda i: (i, 0)),
            pl.BlockSpec((1, N_EXPERTS, tm), lambda i: (i // per_b, 0, i % per_b)),
        ],
        out_shape=[jax.ShapeDtypeStruct((m, d), F32),
                   jax.ShapeDtypeStruct((m, d // 2), jnp.uint32),
                   jax.ShapeDtypeStruct((b, N_EXPERTS, n), F32)],
        scratch_shapes=[pltpu.VMEM((d, d), BF16)],
        compiler_params=_params(1),
        name="out_proj_router",
    )(h, w_out, x, gate_v, g.reshape(1, d), sh_v, sc_v, w_router.T, b_router.reshape(N_EXPERTS, 1))


def _select_kernel(aff_ref, tri_ref, slot_ref, *, cap):
    bits = pltpu.bitcast(aff_ref[...], jnp.int32)

    def step(i, t):
        cand = t | lax.shift_left(jnp.int32(1), 30 - i)
        cnt = jnp.sum(jnp.where(bits >= cand, 1.0, 0.0), axis=1, keepdims=True)
        return jnp.where(cnt >= cap, cand, t)

    t = lax.fori_loop(0, 31, step, jnp.zeros((bits.shape[0], 1), jnp.int32))
    gt = bits > t
    eq = bits == t
    need = cap - jnp.sum(jnp.where(gt, 1.0, 0.0), axis=1, keepdims=True)
    tri = tri_ref[...]
    eq_rank = jnp.dot(jnp.where(eq, 1.0, 0.0).astype(BF16), tri, preferred_element_type=F32)
    sel = gt | (eq & (eq_rank < need))
    slot = jnp.dot(jnp.where(sel, 1.0, 0.0).astype(BF16), tri, preferred_element_type=F32)
    slot_ref[...] = jnp.where(sel, slot.astype(jnp.int32), -1)


def select_tokens(aff, cap):
    b, e, n = aff.shape
    tri = jnp.triu(jnp.ones((n, n), BF16), k=1)
    rows = pl.BlockSpec((b * e, n), lambda i: (0, 0))
    return pl.pallas_call(
        functools.partial(_select_kernel, cap=cap),
        grid=(1,),
        in_specs=[rows, pl.BlockSpec((n, n), lambda i: (0, 0))],
        out_specs=rows,
        out_shape=jax.ShapeDtypeStruct((b * e, n), jnp.int32),
        compiler_params=_params(1),
        name="select_tokens",
    )(aff.reshape(b * e, n), tri).reshape(b, e, n)


def _onehot(slot, cap):
    hit = lax.broadcasted_iota(jnp.int32, (cap, slot.shape[1]), 0) == slot
    return hit, jnp.where(hit, 1.0, 0.0)


def gather_tokens(h, slot, aff, cap):
    b, n, w = h.shape
    e = N_EXPERTS
    workers = V7X_SC_CORES * V7X_SC_SUBCORES
    rows_per_worker = (b * e) // workers
    word_bytes = jnp.dtype(h.dtype).itemsize
    chunk = min(cap, V7X_SC_SUBCORE_VMEM_BYTES // 2 // (w * word_bytes))
    assert (b * e) % workers == 0 and cap % chunk == 0 and n % V7X_SC_LANES == 0 and word_bytes == 4

    def body(h_hbm, slot_hbm, aff_hbm, xs_hbm, gates_hbm, slot_v, aff_v, idx_v, gate_v, rows_v, sem):
        worker = lax.axis_index("s") * V7X_SC_CORES + lax.axis_index("c")
        lane = lax.iota(jnp.int32, V7X_SC_LANES)
        first_lane = jnp.zeros((V7X_SC_LANES,), jnp.int32)

        @pl.loop(0, cap)
        def _(s):
            for t in range(GATE_LANES // V7X_SC_LANES):
                gate_v[s, pl.ds(t * V7X_SC_LANES, V7X_SC_LANES)] = jnp.zeros((V7X_SC_LANES,), F32)

        @pl.loop(0, rows_per_worker)
        def _(j):
            r = worker * rows_per_worker + j
            bi = r // e
            ei = r - bi * e
            pltpu.sync_copy(slot_hbm.at[r], slot_v)
            pltpu.sync_copy(aff_hbm.at[r], aff_v)

            @pl.loop(0, n // V7X_SC_LANES)
            def _(c):
                tokens = pl.ds(c * V7X_SC_LANES, V7X_SC_LANES)
                s = slot_v[tokens]
                chosen = s >= 0
                plsc.store_scatter(idx_v, [s], lane + (c * V7X_SC_LANES + bi * n), mask=chosen)
                plsc.store_scatter(gate_v, [s, first_lane], aff_v[tokens], mask=chosen)

            out_row = (ei * b + bi) * cap

            @pl.loop(0, cap // chunk)
            def _(q):
                pltpu.async_copy(h_hbm.at[idx_v.at[pl.ds(q * chunk, chunk)]], rows_v, sem).wait()
                pltpu.sync_copy(rows_v, xs_hbm.at[pl.ds(out_row + q * chunk, chunk)])

            pltpu.sync_copy(gate_v, gates_hbm.at[pl.ds(out_row, cap)])

    xs, gates = pl.kernel(
        body,
        mesh=plsc.VectorSubcoreMesh(core_axis_name="c", subcore_axis_name="s"),
        out_type=[jax.ShapeDtypeStruct((e * b * cap, w), h.dtype),
                  jax.ShapeDtypeStruct((e * b * cap, GATE_LANES), F32)],
        scratch_types=[
            pltpu.VMEM((n,), jnp.int32), pltpu.VMEM((n,), F32),
            pltpu.VMEM((cap,), jnp.int32), pltpu.VMEM((cap, GATE_LANES), F32),
            pltpu.VMEM((chunk, w), h.dtype),
            pltpu.SemaphoreType.DMA,
        ],
        compiler_params=pltpu.CompilerParams(needs_layout_passes=False),
        name="gather_tokens",
    )(h.reshape(b * n, w), slot.reshape(b * e, n), aff.reshape(b * e, n))
    return xs.reshape(e, b * cap, w), gates.reshape(e, b * cap, GATE_LANES)


def _ffn_kernel(*refs, n_sets, n_f):
    xs_refs, gate_refs = refs[0:2 * n_sets:2], refs[1:2 * n_sets:2]
    wg_ref, wu_ref, wd_ref = refs[2 * n_sets:2 * n_sets + 3]
    y_refs = refs[2 * n_sets + 3:3 * n_sets + 3]
    act_refs = refs[3 * n_sets + 3:4 * n_sets + 3]
    row_refs = refs[4 * n_sets + 3:]
    s = pl.program_id(1)
    tf = wg_ref.shape[1]

    @pl.when(s == 0)
    def _():
        for xs_ref, row_ref in zip(xs_refs, row_refs):
            words = xs_ref[0]
            half = words.shape[1]
            row_ref[:, :half] = pltpu.bitcast(lax.shift_left(words, jnp.uint32(16)), F32).astype(BF16)
            row_ref[:, half:] = pltpu.bitcast(words & jnp.uint32(0xFFFF0000), F32).astype(BF16)

    @pl.when(s < n_f)
    def _():
        wg = wg_ref[...].astype(BF16)
        wu = wu_ref[...].astype(BF16)
        for row_ref, act_ref in zip(row_refs, act_refs):
            xs = row_ref[...]
            hg = jnp.dot(xs, wg, preferred_element_type=F32)
            hu = jnp.dot(xs, wu, preferred_element_type=F32)
            act_ref[s] = (jax.nn.silu(hg) * hu).astype(BF16)

    @pl.when(s >= n_f)
    def _():
        wd = wd_ref[...].astype(BF16)
        for gate_ref, y_ref, act_ref in zip(gate_refs, y_refs, act_refs):
            y = None
            for f in range(n_f):
                part = jnp.dot(act_ref[f], wd[f * tf:(f + 1) * tf, :], preferred_element_type=F32)
                y = part if y is None else y + part
            y_ref[0] = (y * gate_ref[0, :, 0:1]).astype(y_ref.dtype)


def expert_ffn(sets, w_gate, w_up, w_down, layer, tf=512, tn=512):
    d, ff = w_gate.shape[2], w_gate.shape[3]
    n_f, n_d = ff // tf, d // tn
    d_idx = lambda s: jnp.maximum(s - n_f, 0)

    def up_tile(i, s):
        ahead = s >= n_f
        return (layer, jnp.where(ahead, jnp.minimum(i + 1, N_EXPERTS - 1), i), 0,
                jnp.where(ahead, 0, jnp.minimum(s, n_f - 1)))

    def down_tile(i, s):
        behind = s < n_f
        return (layer, jnp.where(behind, jnp.maximum(i - 1, 0), i), 0, jnp.where(behind, n_d - 1, s - n_f))
    in_specs, operands, out_specs, out_shape, act_scratch, row_scratch = [], [], [], [], [], []
    for xs, gates in sets:
        r = xs.shape[1]
        in_specs += [pl.BlockSpec((1, r, d // 2), lambda i, s: (i, 0, 0)),
                     pl.BlockSpec((1, r, GATE_LANES), lambda i, s: (i, 0, 0))]
        operands += [xs, gates]
        out_specs.append(pl.BlockSpec((1, r, tn), lambda i, s: (i, 0, d_idx(s))))
        out_shape.append(jax.ShapeDtypeStruct((N_EXPERTS, r, d), BF16))
        act_scratch.append(pltpu.VMEM((n_f, r, tf), BF16))
        row_scratch.append(pltpu.VMEM((r, d), BF16))
    in_specs += [
        pl.BlockSpec((None, None, d, tf), up_tile),
        pl.BlockSpec((None, None, d, tf), up_tile),
        pl.BlockSpec((None, None, ff, tn), down_tile),
    ]
    return pl.pallas_call(
        functools.partial(_ffn_kernel, n_sets=len(sets), n_f=n_f),
        grid=(N_EXPERTS, n_f + n_d),
        in_specs=in_specs,
        out_specs=out_specs,
        out_shape=out_shape,
        scratch_shapes=act_scratch + row_scratch,
        compiler_params=_params(2),
        name="expert_ffn",
    )(*operands, w_gate, w_up, w_down)


def _combine_kernel(y_ref, slot_ref, x_ref, mod_ref, o_ref, onehot_ref, *, cap):
    @pl.when(pl.program_id(1) == 0)
    def _():
        for e in range(N_EXPERTS):
            onehot_ref[e * cap:(e + 1) * cap, :] = _onehot(slot_ref[e], cap)[1].astype(BF16)

    y = y_ref[...].reshape(N_EXPERTS * cap, y_ref.shape[2])
    moe = lax.dot_general(onehot_ref[...], y, (((0,), (0,)), ((), ())), preferred_element_type=F32)
    o_ref[0] = x_ref[0] + mod_ref[0] * moe


def combine(y, slot, x, gate_vec, cap, tn=512):
    b, n, d = x.shape
    e = N_EXPERTS
    return pl.pallas_call(
        functools.partial(_combine_kernel, cap=cap),
        grid=(b, d // tn),
        in_specs=[
            pl.BlockSpec((e, cap, tn), lambda i, j: (0, i, j)),
            pl.BlockSpec((e, 1, n), lambda i, j: (i, 0, 0)),
            pl.BlockSpec((1, n, tn), lambda i, j: (i, 0, j)),
            pl.BlockSpec((1, 1, tn), lambda i, j: (i, 0, j)),
        ],
        out_specs=pl.BlockSpec((1, n, tn), lambda i, j: (i, 0, j)),
        out_shape=jax.ShapeDtypeStruct((b, n, d), F32),
        scratch_shapes=[pltpu.VMEM((e * cap, n), BF16)],
        compiler_params=_params(2),
        name="combine",
    )(y, slot.reshape(b * e, 1, n), x, gate_vec.reshape(b, 1, d))


def moe_route(hw, aff):
    n = hw.shape[1]
    cap = CAPACITY_FACTOR * n // N_EXPERTS
    slot = select_tokens(aff, cap)
    return slot, cap, gather_tokens(hw, slot, aff, cap)


def mixer_branches(h, n, layer, w_in, conv_w, ln_g, w_s, b_s):
    p = matmul(h, w_in, layer, P_COL0, w_in.shape[2] - P_COL0)
    return p, short_conv(h, w_in, conv_w, layer, n), chunk_gmlp(p, ln_g, w_s, b_s, layer)


def mixer_merge(x, attn, branches, mod, n, layer, w_branch, w_out, g2, w_router, b_router, after=()):
    b = x.shape[0] // n
    p, conv, gm = branches
    merged = merge_branches(attn, conv, gm, p, w_branch, layer, after=after)
    x_new, hw, aff = out_proj_router(merged, w_out, layer, x, mod, g2, w_router, b_router, n)
    return x_new.reshape(b, n, -1), hw.reshape(b, n, -1), aff


def kernel(x, c, ctx, c_ctx, w_mod, b_mod, norm1_g, w_in, na_rpb, conv_w, gmlp_ln_g, w_spatial, b_spatial,
           w_branch, w_out, norm2_g, w_router, b_router, w_e_gate, w_e_up, w_e_down, final_g):
    b, n, d = x.shape
    lc = ctx.shape[1]
    cvec = jnp.concatenate([c, c_ctx[None], jnp.zeros((8 - b - 1, d), F32)], axis=0)
    modv = modvec(cvec, w_mod, b_mod).reshape(DEPTH, 8, 6, d)
    cos_t, sin_t = rope_tables(n)
    branch_w = (w_in, conv_w, gmlp_ln_g, w_spatial, b_spatial)
    ctx_s = ctx
    for layer in range(DEPTH):
        last = layer == DEPTH - 1
        mod = modv[layer, :b]
        mod_c = jnp.broadcast_to(modv[layer, b][None], (b, 6, d))
        merge_w = (w_branch, w_out, norm2_g[layer], w_router[layer], b_router[layer])

        hc = norm_mod(ctx_s, norm1_g[layer], mod_c[:, 0], mod_c[:, 1]).reshape(b * lc, d)
        if last:
            kvc = qkv_proj(hc, w_in, layer, lc, cos_t, sin_t, rope=False, first_tile=1, n_tiles=2)
            kc_col, early = 0, ()
        else:
            kvc = qkv_proj(hc, w_in, layer, lc, cos_t, sin_t, rope=False)
            kc_col = D_BRANCH
            attn_c = ctx_attention(kvc, b, lc)
            branches_c = mixer_branches(hc, lc, layer, *branch_w)
            early = (attn_c,) + branches_c[1:]

        h = norm_mod(x, norm1_g[layer], mod[:, 0], mod[:, 1]).reshape(b * n, d)
        qkv = qkv_proj(h, w_in, layer, n, cos_t, sin_t, rope=True, after=early)
        if layer == 0:
            bias = na_bias(na_rpb, n // GRID_W, after=qkv)
        attn = na_attention(qkv, kvc, kc_col, bias, layer, b, n)
        x, hw, aff = mixer_merge(x.reshape(b * n, d), attn, mixer_branches(h, n, layer, *branch_w), mod, n, layer,
                                 *merge_w)

        slot, cap, rows = moe_route(hw, aff)
        if last:
            (y,) = expert_ffn([rows], w_e_gate, w_e_up, w_e_down, layer)
        else:
            ctx_mid, hw_c, aff_c = mixer_merge(ctx_s.reshape(b * lc, d), attn_c, branches_c, mod_c, lc, layer,
                                               *merge_w, after=(slot,))
            slot_c, cap_c, rows_c = moe_route(hw_c, aff_c)
            y, y_c = expert_ffn([rows, rows_c], w_e_gate, w_e_up, w_e_down, layer)
            ctx_s = combine(y_c, slot_c, ctx_mid, mod_c[:, 5], cap_c)
        x = combine(y, slot, x, mod[:, 5], cap)
    return final_norm(x, final_g)
```
